```python
import math
import jax, jax.numpy as jnp
from jax import lax
import numpy as np

D_MODEL = 2048
BATCH = 8
SEQ = 4096
DEPTH = 4

N_MIXERS = 3
HEAD_DIM = 128
N_MAPS = D_MODEL // HEAD_DIM
BLOCK_Q = 128
A_HEADS = N_MAPS
A_KV_RANK = D_MODEL // 4
IDX_HEADS = A_HEADS // 2
IDX_DIM = 64
TOPK_MAX = 256
A_COLS = A_HEADS * HEAD_DIM + A_KV_RANK + IDX_HEADS * IDX_DIM + IDX_DIM + IDX_HEADS
B_HEADS = N_MAPS
B_COLS = 3 * B_HEADS * HEAD_DIM + B_HEADS
C_HEADS = N_MAPS // 2
C_COLS = 3 * C_HEADS * 2 * HEAD_DIM
N_EXPERTS = 32
TOP_K = 4
D_FF = D_MODEL // 4
SWIGLU_LIMIT = 7.0
SWIGLU_ALPHA = 1.702
NUM_BUCKETS = 32
MAX_DISTANCE = 128
RMS_EPS = 1e-6
NEG_INF = -1e30

kernel_name = "hybrid_dsa_fox_diff_moe_adaln"


def _layer_counts():
    return (DEPTH + 2) // 3, (DEPTH + 1) // 3, DEPTH // 3


def rmsnorm(x, g):
    xf = x.astype(jnp.float32)
    y = xf * lax.rsqrt(jnp.mean(xf * xf, axis=-1, keepdims=True) + RMS_EPS)
    return (y * g.astype(jnp.float32)).astype(x.dtype)


def t5_bucket(dist):
    n = jnp.maximum(dist, 0)
    max_exact = NUM_BUCKETS // 2
    nf = jnp.maximum(n, 1).astype(jnp.float32)
    large = max_exact + (jnp.log(nf / max_exact) / math.log(MAX_DISTANCE / max_exact)
                         * (NUM_BUCKETS - max_exact)).astype(jnp.int32)
    large = jnp.minimum(large, NUM_BUCKETS - 1)
    return jnp.where(n < max_exact, n, large)


def dsa_attention(h, w_in, kv_gain, w_uk, w_uv, w_out, rel_table):
    B, S, _ = h.shape
    n_sel = min(TOPK_MAX, S // 4)
    o1 = A_HEADS * HEAD_DIM
    o2 = o1 + A_KV_RANK
    o3 = o2 + IDX_HEADS * IDX_DIM
    o4 = o3 + IDX_DIM
    proj = h @ w_in
    q = proj[..., :o1].reshape(B, S, A_HEADS, HEAD_DIM)
    ckv = rmsnorm(proj[..., o1:o2], kv_gain)
    q_idx = proj[..., o2:o3].reshape(B, S, IDX_HEADS, IDX_DIM)
    k_idx = proj[..., o3:o4]
    w_idx = proj[..., o4:]
    kpos = jnp.arange(S)
    scale = HEAD_DIM ** -0.5
    gather = jax.vmap(lambda src, idx: src[idx])

    def block(j):
        t0 = j * BLOCK_Q
        qpos = t0 + jnp.arange(BLOCK_Q)
        qb = lax.dynamic_slice_in_dim(q, t0, BLOCK_Q, axis=1)
        qib = lax.dynamic_slice_in_dim(q_idx, t0, BLOCK_Q, axis=1)
        wib = lax.dynamic_slice_in_dim(w_idx, t0, BLOCK_Q, axis=1)
        dots = jnp.einsum('bthe,bse->bths', qib, k_idx).astype(jnp.float32)
        score = jnp.einsum('bth,bths->bts', wib.astype(jnp.float32), jax.nn.relu(dots))
        score = jnp.where(kpos[None, None, :] <= qpos[None, :, None], score, NEG_INF)
        _, sel = lax.top_k(score, n_sel)
        kv = gather(ckv, sel)
        q_lat = jnp.einsum('bthd,hdr->bthr', qb, w_uk)
        logits = jnp.einsum('bthr,btkr->bthk', q_lat, kv).astype(jnp.float32) * scale
        dist = qpos[None, :, None] - sel
        bias = rel_table[t5_bucket(dist)]
        logits = logits + jnp.moveaxis(bias, -1, 2).astype(jnp.float32)
        logits = jnp.where((dist >= 0)[:, :, None, :], logits, NEG_INF)
        p = jax.nn.softmax(logits, axis=-1).astype(kv.dtype)
        o_lat = jnp.einsum('bthk,btkr->bthr', p, kv)
        o = jnp.einsum('bthr,hrd->bthd', o_lat, w_uv)
        return o.reshape(B, BLOCK_Q, A_HEADS * HEAD_DIM)

    out = lax.map(block, jnp.arange(S // BLOCK_Q))
    out = jnp.moveaxis(out, 0, 1).reshape(B, S, A_HEADS * HEAD_DIM)
    return out @ w_out


def fox_attention(h, w_in, f_bias, w_out):
    B, S, _ = h.shape
    hd = B_HEADS * HEAD_DIM
    proj = h @ w_in
    q = proj[..., :hd].reshape(B, S, B_HEADS, HEAD_DIM)
    k = proj[..., hd:2 * hd].reshape(B, S, B_HEADS, HEAD_DIM)
    v = proj[..., 2 * hd:3 * hd].reshape(B, S, B_HEADS, HEAD_DIM)
    logf = jax.nn.log_sigmoid(proj[..., 3 * hd:].astype(jnp.float32) + f_bias.astype(jnp.float32))
    cum = lax.cumsum(logf, axis=1)
    cum_k = jnp.moveaxis(cum, 1, 2)
    kpos = jnp.arange(S)
    scale = HEAD_DIM ** -0.5

    def block(j):
        t0 = j * BLOCK_Q
        qpos = t0 + jnp.arange(BLOCK_Q)
        qb = lax.dynamic_slice_in_dim(q, t0, BLOCK_Q, axis=1)
        cum_q = jnp.moveaxis(lax.dynamic_slice_in_dim(cum, t0, BLOCK_Q, axis=1), 1, 2)
        logits = jnp.einsum('bthd,bshd->bhts', qb, k).astype(jnp.float32) * scale
        logits = logits + cum_q[..., None] - cum_k[:, :, None, :]
        logits = jnp.where((kpos[None, :] <= qpos[:, None])[None, None], logits, NEG_INF)
        p = jax.nn.softmax(logits, axis=-1).astype(v.dtype)
        o = jnp.einsum('bhts,bshd->bthd', p, v)
        return o.reshape(B, BLOCK_Q, hd)

    out = lax.map(block, jnp.arange(S // BLOCK_Q))
    out = jnp.moveaxis(out, 0, 1).reshape(B, S, hd)
    return out @ w_out


def diff_attention(h, w_in, lam_p, sub_gain, w_out, rel_table, lam_init):
    B, S, _ = h.shape
    w2 = C_HEADS * 2 * HEAD_DIM
    proj = h @ w_in
    q = proj[..., :w2].reshape(B, S, C_HEADS, 2, HEAD_DIM)
    k = proj[..., w2:2 * w2].reshape(B, S, C_HEADS, 2, HEAD_DIM)
    v = proj[..., 2 * w2:].reshape(B, S, C_HEADS, 2 * HEAD_DIM)
    lp = lam_p.astype(jnp.float32)
    lam = jnp.exp(jnp.sum(lp[0] * lp[1])) - jnp.exp(jnp.sum(lp[2] * lp[3])) + lam_init
    kpos = jnp.arange(S)
    scale = HEAD_DIM ** -0.5

    def block(j):
        t0 = j * BLOCK_Q
        qpos = t0 + jnp.arange(BLOCK_Q)
        qb = lax.dynamic_slice_in_dim(q, t0, BLOCK_Q, axis=1)
        dist = qpos[:, None] - kpos[None, :]
        bias = rel_table[t5_bucket(dist)].reshape(BLOCK_Q, S, C_HEADS, 2)
        bias = jnp.transpose(bias, (2, 3, 0, 1)).astype(jnp.float32)
        logits = jnp.einsum('bthjd,bshjd->bhjts', qb, k).astype(jnp.float32) * scale + bias[None]
        logits = jnp.where((dist >= 0)[None, None, None], logits, NEG_INF)
        p = jax.nn.softmax(logits, axis=-1)
        a = (p[:, :, 0] - lam * p[:, :, 1]).astype(v.dtype)
        o = jnp.einsum('bhts,bshe->bthe', a, v)
        o = rmsnorm(o, sub_gain) * (1.0 - lam_init)
        return o.reshape(B, BLOCK_Q, w2)

    out = lax.map(block, jnp.arange(S // BLOCK_Q))
    out = jnp.moveaxis(out, 0, 1).reshape(B, S, w2)
    return out @ w_out


def moe_ffn(h, w_router, b_router, w_gu, b_gu, w_dn, b_dn):
    B, S, D = h.shape
    xf = h.reshape(B * S, D)
    logits = (xf @ w_router + b_router).astype(jnp.float32)
    top_v, top_i = lax.top_k(logits, TOP_K)
    top_w = jax.nn.softmax(top_v, axis=-1)
    combine = jnp.sum(jax.nn.one_hot(top_i, N_EXPERTS, dtype=jnp.float32) * top_w[..., None], axis=1)
    combine = combine.astype(xf.dtype)
    out = jnp.zeros_like(xf)
    for e in range(N_EXPERTS):
        gu = xf @ w_gu[e] + b_gu[e]
        gate = jnp.minimum(gu[:, ::2], SWIGLU_LIMIT)
        up = jnp.clip(gu[:, 1::2], -SWIGLU_LIMIT, SWIGLU_LIMIT)
        glu = gate * jax.nn.sigmoid(gate * SWIGLU_ALPHA)
        y = ((up + 1.0) * glu) @ w_dn[e] + b_dn[e]
        out = out + combine[:, e:e + 1] * y
    return out.reshape(B, S, D)


def setup_inputs(seed: int = 0) -> dict:
    key = jax.random.key(seed)
    ks = jax.random.split(key, 28)
    n_a, n_b, n_c = _layer_counts()
    D, H = D_MODEL, HEAD_DIM

    def nrm(k, shape, scale):
        return jax.random.normal(k, shape, jnp.float32) * scale

    return {
        "x": nrm(ks[0], (BATCH, SEQ, D), 1.0),
        "c": nrm(ks[1], (BATCH, D), 1.0),
        "rel_bias": nrm(ks[2], (NUM_BUCKETS, N_MAPS), 0.5),
        "norm_mix": 1.0 + nrm(ks[3], (DEPTH, D), 0.05),
        "norm_ffn": 1.0 + nrm(ks[4], (DEPTH, D), 0.05),
        "w_mod": nrm(ks[5], (DEPTH, D, 6 * D), 0.5 * D ** -0.5),
        "b_mod": nrm(ks[6], (DEPTH, 6 * D), 0.02),
        "dsa_w_in": nrm(ks[7], (n_a, D, A_COLS), D ** -0.5),
        "dsa_kv_gain": 1.0 + nrm(ks[8], (n_a, A_KV_RANK), 0.05),
        "dsa_w_uk": nrm(ks[9], (n_a, A_HEADS, H, A_KV_RANK), A_KV_RANK ** -0.5),
        "dsa_w_uv": nrm(ks[10], (n_a, A_HEADS, A_KV_RANK, H), A_KV_RANK ** -0.5),
        "dsa_w_out": nrm(ks[11], (n_a, A_HEADS * H, D), (A_HEADS * H) ** -0.5),
        "fox_w_in": nrm(ks[12], (n_b, D, B_COLS), D ** -0.5),
        "fox_forget_bias": 3.0 + nrm(ks[13], (n_b, B_HEADS), 0.5),
        "fox_w_out": nrm(ks[14], (n_b, B_HEADS * H, D), (B_HEADS * H) ** -0.5),
        "diff_w_in": nrm(ks[15], (n_c, D, C_COLS), D ** -0.5),
        "diff_lambda": nrm(ks[16], (n_c, 4, H), 0.1),
        "diff_subln_gain": 1.0 + nrm(ks[17], (n_c, 2 * H), 0.05),
        "diff_w_out": nrm(ks[18], (n_c, C_HEADS * 2 * H, D), (C_HEADS * 2 * H) ** -0.5),
        "w_router": nrm(ks[19], (DEPTH, D, N_EXPERTS), D ** -0.5),
        "b_router": nrm(ks[20], (DEPTH, N_EXPERTS), 0.01),
        "w_gate_up": nrm(ks[21], (DEPTH, N_EXPERTS, D, 2 * D_FF), D ** -0.5),
        "b_gate_up": nrm(ks[22], (DEPTH, N_EXPERTS, 2 * D_FF), 0.01),
        "w_down": nrm(ks[23], (DEPTH, N_EXPERTS, D_FF, D), D_FF ** -0.5),
        "b_down": nrm(ks[24], (DEPTH, N_EXPERTS, D), 0.01),
        "final_norm": 1.0 + nrm(ks[25], (D,), 0.05),
    }


def reference(x, c, rel_bias, norm_mix, norm_ffn, w_mod, b_mod,
              dsa_w_in, dsa_kv_gain, dsa_w_uk, dsa_w_uv, dsa_w_out,
              fox_w_in, fox_forget_bias, fox_w_out,
              diff_w_in, diff_lambda, diff_subln_gain, diff_w_out,
              w_router, b_router, w_gate_up, b_gate_up, w_down, b_down, final_norm):
    cs = jax.nn.silu(c)
    ia = ib = ic = 0
    for i in range(DEPTH):
        mod = cs @ w_mod[i] + b_mod[i]
        sh1, sc1, g1, sh2, sc2, g2 = jnp.split(mod, 6, axis=-1)
        h = rmsnorm(x, norm_mix[i]) * (1.0 + sc1[:, None]) + sh1[:, None]
        kind = i % N_MIXERS
        if kind == 0:
            y = dsa_attention(h, dsa_w_in[ia], dsa_kv_gain[ia], dsa_w_uk[ia], dsa_w_uv[ia],
                              dsa_w_out[ia], rel_bias)
            ia += 1
        elif kind == 1:
            y = fox_attention(h, fox_w_in[ib], fox_forget_bias[ib], fox_w_out[ib])
            ib += 1
        else:
            lam_init = 0.8 - 0.6 * math.exp(-0.3 * i)
            y = diff_attention(h, diff_w_in[ic], diff_lambda[ic], diff_subln_gain[ic],
                               diff_w_out[ic], rel_bias, lam_init)
            ic += 1
        x = x + g1[:, None] * y
        h = rmsnorm(x, norm_ffn[i]) * (1.0 + sc2[:, None]) + sh2[:, None]
        x = x + g2[:, None] * moe_ffn(h, w_router[i], b_router[i], w_gate_up[i], b_gate_up[i],
                                      w_down[i], b_down[i])
    return rmsnorm(x, final_norm)
```

```python
import functools
import math

import numpy as np
import jax
import jax.numpy as jnp
from jax import lax
from jax.experimental import pallas as pl
from jax.experimental.pallas import tpu as pltpu

HEAD_DIM = 128
IDX_DIM = 64
TOPK_MAX = 256
TOP_K = 4
SWIGLU_LIMIT = 7.0
SWIGLU_ALPHA = 1.702
NUM_BUCKETS = 32
MAX_DISTANCE = 128
RMS_EPS = 1e-6
NEG_INF = -1e30
N_MIXERS = 3
LANES = 128
INT_MIN = -(2 ** 31)

F32 = jnp.float32
BF16 = jnp.bfloat16

VMEM_LIMIT = 56 * 1024 * 1024


def _cparams(sem):
    return pltpu.CompilerParams(dimension_semantics=sem, vmem_limit_bytes=VMEM_LIMIT)


def _sigmoid(x):
    return 1.0 / (1.0 + jnp.exp(-x))


def _mod_kernel(c_ref, w_ref, b_ref, o_ref):
    c = c_ref[...]
    cs = (c * _sigmoid(c)).astype(BF16)
    o_ref[0] = jnp.dot(cs, w_ref[0].astype(BF16), preferred_element_type=F32) + b_ref[0]


def _modulation(c, w_mod, b_mod):
    depth, d, n = w_mod.shape
    b = c.shape[0]
    tn = min(n, 1024)
    return pl.pallas_call(
        _mod_kernel,
        grid=(depth, n // tn),
        in_specs=[pl.BlockSpec((b, d), lambda i, j: (0, 0)),
                  pl.BlockSpec((1, d, tn), lambda i, j: (i, 0, j)),
                  pl.BlockSpec((1, 1, tn), lambda i, j: (i, 0, j))],
        out_specs=pl.BlockSpec((1, b, tn), lambda i, j: (i, 0, j)),
        out_shape=jax.ShapeDtypeStruct((depth, b, n), F32),
        compiler_params=_cparams(("parallel", "parallel")),
        name="adaln_modulation",
    )(c, w_mod, b_mod.reshape(depth, 1, n))


def _norm_mod_kernel(x_ref, g_ref, sc_ref, sh_ref, o_ref):
    x = x_ref[0]
    ms = jnp.mean(x * x, axis=-1, keepdims=True)
    y = x * lax.rsqrt(ms + RMS_EPS) * g_ref[...]
    o_ref[0] = (y * (1.0 + sc_ref[0]) + sh_ref[0]).astype(o_ref.dtype)


def _norm_kernel(x_ref, g_ref, o_ref):
    x = x_ref[0]
    ms = jnp.mean(x * x, axis=-1, keepdims=True)
    o_ref[0] = (x * lax.rsqrt(ms + RMS_EPS) * g_ref[...]).astype(o_ref.dtype)


def _row_tile(s, want):
    t = min(s, want)
    assert s % t == 0
    return t


def _norm_mod(x, g, sc, sh, out_dtype=BF16):
    b, s, d = x.shape
    ts = _row_tile(s, 512)
    vec = pl.BlockSpec((1, 1, d), lambda i, j: (i, 0, 0))
    return pl.pallas_call(
        _norm_mod_kernel,
        grid=(b, s // ts),
        in_specs=[pl.BlockSpec((1, ts, d), lambda i, j: (i, j, 0)),
                  pl.BlockSpec((1, d), lambda i, j: (0, 0)), vec, vec],
        out_specs=pl.BlockSpec((1, ts, d), lambda i, j: (i, j, 0)),
        out_shape=jax.ShapeDtypeStruct((b, s, d), out_dtype),
        compiler_params=_cparams(("parallel", "parallel")),
        name="rmsnorm_adaln",
    )(x, g.reshape(1, d), sc.reshape(b, 1, d), sh.reshape(b, 1, d))


def _norm(x, g, out_dtype, col_block=0):
    b, s, _ = x.shape
    d = g.shape[-1]
    ts = _row_tile(s, 512)
    return pl.pallas_call(
        _norm_kernel,
        grid=(b, s // ts),
        in_specs=[pl.BlockSpec((1, ts, d), lambda i, j: (i, j, col_block)),
                  pl.BlockSpec((1, d), lambda i, j: (0, 0))],
        out_specs=pl.BlockSpec((1, ts, d), lambda i, j: (i, j, 0)),
        out_shape=jax.ShapeDtypeStruct((b, s, d), out_dtype),
        compiler_params=_cparams(("parallel", "parallel")),
        name="rmsnorm",
    )(x, g.reshape(1, d))


def _mm_kernel(a_ref, b_ref, o_ref):
    o_ref[...] = jnp.dot(a_ref[...], b_ref[...], preferred_element_type=F32).astype(o_ref.dtype)


def _mm_res_kernel(a_ref, b_ref, x_ref, g_ref, o_ref):
    y = jnp.dot(a_ref[...], b_ref[...], preferred_element_type=F32)
    o_ref[...] = x_ref[...] + g_ref[0] * y


def _matmul(a, w, out_dtype, tm=1024, tn=512):
    m, k = a.shape
    n = w.shape[1]
    tm = _row_tile(m, tm)
    tn = _row_tile(n, tn)
    return pl.pallas_call(
        _mm_kernel,
        grid=(m // tm, n // tn),
        in_specs=[pl.BlockSpec((tm, k), lambda i, j: (i, 0)),
                  pl.BlockSpec((k, tn), lambda i, j: (0, j))],
        out_specs=pl.BlockSpec((tm, tn), lambda i, j: (i, j)),
        out_shape=jax.ShapeDtypeStruct((m, n), out_dtype),
        compiler_params=_cparams(("parallel", "parallel")),
        name="matmul",
    )(a, w)


def _matmul_residual(a, w, x, gate, rows_per_batch, tm=1024, tn=512):
    m, k = a.shape
    n = w.shape[1]
    tm = _row_tile(rows_per_batch, tm)
    tn = _row_tile(n, tn)
    tiles_per_batch = rows_per_batch // tm
    nb = gate.shape[0]
    return pl.pallas_call(
        _mm_res_kernel,
        grid=(m // tm, n // tn),
        in_specs=[pl.BlockSpec((tm, k), lambda i, j: (i, 0)),
                  pl.BlockSpec((k, tn), lambda i, j: (0, j)),
                  pl.BlockSpec((tm, tn), lambda i, j: (i, j)),
                  pl.BlockSpec((1, 1, tn), lambda i, j: (i // tiles_per_batch, 0, j))],
        out_specs=pl.BlockSpec((tm, tn), lambda i, j: (i, j)),
        out_shape=jax.ShapeDtypeStruct((m, n), F32),
        compiler_params=_cparams(("parallel", "parallel")),
        name="matmul_residual",
    )(a, w, x, gate.reshape(nb, 1, n))


def _pad_cols(w, mult=LANES):
    n = w.shape[-1]
    pad = (-n) % mult
    if pad:
        w = jnp.pad(w, [(0, 0)] * (w.ndim - 1) + [(0, pad)])
    return w


def _softmax_step(s, v, m_ref, l_ref, acc_ref):
    m_old = m_ref[...]
    m_new = jnp.maximum(m_old, jnp.max(s, axis=-1, keepdims=True))
    alpha = jnp.exp(m_old - m_new)
    p = jnp.exp(s - m_new)
    l_ref[...] = alpha * l_ref[...] + jnp.sum(p, axis=-1, keepdims=True)
    acc_ref[...] = alpha * acc_ref[...] + jnp.dot(p.astype(v.dtype), v, preferred_element_type=F32)
    m_ref[...] = m_new


def _qk(q, k, scale):
    return lax.dot_general(q, k, (((1,), (1,)), ((), ())), preferred_element_type=F32) * scale


def _causal_tile_mask(t):
    r = lax.broadcasted_iota(jnp.int32, (t, t), 0)
    c = lax.broadcasted_iota(jnp.int32, (t, t), 1)
    return r >= c


def _t5_bucket_np(dist):
    n = np.maximum(dist, 0)
    max_exact = NUM_BUCKETS // 2
    nf = np.maximum(n, 1).astype(np.float32)
    large = max_exact + (np.log(nf / np.float32(max_exact)) / np.float32(math.log(MAX_DISTANCE / max_exact))
                         * np.float32(NUM_BUCKETS - max_exact)).astype(np.int32)
    large = np.minimum(large, NUM_BUCKETS - 1)
    return np.where(n < max_exact, n, large).astype(np.int32)


def _bias_tile_kernel(bk_ref, rel_ref, o_ref):
    m = pl.program_id(0)
    bk = bk_ref[...]
    acc = jnp.zeros(bk.shape, F32)
    for b in range(NUM_BUCKETS):
        acc = jnp.where(bk == b, rel_ref[b, m], acc)
    o_ref[0] = acc


def _bias_tiles(rel_bias, t):
    n_maps = rel_bias.shape[1]
    i = np.arange(t)[:, None]
    j = np.arange(t)[None, :]
    buckets = np.stack([_t5_bucket_np(i - j), _t5_bucket_np(t + i - j)])
    return pl.pallas_call(
        _bias_tile_kernel,
        grid=(n_maps,),
        in_specs=[pl.BlockSpec((2, t, t), lambda m: (0, 0, 0)),
                  pl.BlockSpec(memory_space=pltpu.SMEM)],
        out_specs=pl.BlockSpec((1, 2, t, t), lambda m: (m, 0, 0, 0)),
        out_shape=jax.ShapeDtypeStruct((n_maps, 2, t, t), F32),
        compiler_params=_cparams(("parallel",)),
        name="t5_bias_tiles",
    )(jnp.asarray(buckets), rel_bias)


def _logsig_cumsum_kernel(f_ref, fb_ref, o_ref, carry_ref):
    @pl.when(pl.program_id(1) == 0)
    def _():
        carry_ref[...] = jnp.zeros_like(carry_ref)

    z = f_ref[0] + fb_ref[...]
    logf = -(jnp.maximum(-z, 0.0) + jnp.log(1.0 + jnp.exp(-jnp.abs(z))))
    t = z.shape[0]
    tri = _causal_tile_mask(t).astype(BF16)
    hi = logf.astype(BF16)
    r1 = logf - hi.astype(F32)
    mid = r1.astype(BF16)
    lo = (r1 - mid.astype(F32)).astype(BF16)
    cum = (jnp.dot(tri, hi, preferred_element_type=F32)
           + jnp.dot(tri, mid, preferred_element_type=F32)
           + jnp.dot(tri, lo, preferred_element_type=F32)) + carry_ref[...]
    o_ref[0] = cum
    carry_ref[...] = cum[t - 1:t, :]


def _logsig_cumsum(fg, f_bias):
    b, s, w = fg.shape
    tc = _row_tile(s, 512)
    fb = jnp.pad(f_bias, (0, w - f_bias.shape[0])).reshape(1, w)
    return pl.pallas_call(
        _logsig_cumsum_kernel,
        grid=(b, s // tc),
        in_specs=[pl.BlockSpec((1, tc, w), lambda i, j: (i, j, 0)),
                  pl.BlockSpec((1, w), lambda i, j: (0, 0))],
        out_specs=pl.BlockSpec((1, tc, w), lambda i, j: (i, j, 0)),
        out_shape=jax.ShapeDtypeStruct((b, s, w), F32),
        scratch_shapes=[pltpu.VMEM((1, w), F32)],
        compiler_params=_cparams(("parallel", "arbitrary")),
        name="fox_logsig_cumsum",
    )(fg, fb)


def _fox_kernel(q_ref, k_ref, v_ref, cq_ref, ck_ref, o_ref, m_ref, l_ref, acc_ref, *, t, scale):
    qi = pl.program_id(2)
    m_ref[...] = jnp.full_like(m_ref, NEG_INF)
    l_ref[...] = jnp.zeros_like(l_ref)
    acc_ref[...] = jnp.zeros_like(acc_ref)
    q = q_ref[0]
    cq = cq_ref[0, 0]

    def tile(kj, diag):
        k0 = pl.multiple_of(kj * t, t)
        k = k_ref[0, pl.ds(k0, t), :]
        v = v_ref[0, pl.ds(k0, t), :]
        s = _qk(q, k, scale) + (cq - ck_ref[0, 0, :, pl.ds(k0, t)])
        if diag:
            s = jnp.where(_causal_tile_mask(t), s, NEG_INF)
        _softmax_step(s, v, m_ref, l_ref, acc_ref)

    def body(kj, carry):
        tile(kj, False)
        return carry

    lax.fori_loop(0, qi, body, 0)
    tile(qi, True)
    o_ref[0] = (acc_ref[...] / l_ref[...]).astype(o_ref.dtype)


def _fox_attention(qkv, cum, n_heads):
    b, s, _ = qkv.shape
    t = _row_tile(s, 256)
    cum_q = jnp.transpose(cum[:, :, :n_heads], (0, 2, 1))[..., None]
    cum_k = jnp.transpose(cum[:, :, :n_heads], (0, 2, 1))[:, :, None, :]
    hd = HEAD_DIM
    return pl.pallas_call(
        functools.partial(_fox_kernel, t=t, scale=hd ** -0.5),
        grid=(b, n_heads, s // t),
        in_specs=[pl.BlockSpec((1, t, hd), lambda i, h, j: (i, j, h)),
                  pl.BlockSpec((1, s, hd), lambda i, h, j: (i, 0, n_heads + h)),
                  pl.BlockSpec((1, s, hd), lambda i, h, j: (i, 0, 2 * n_heads + h)),
                  pl.BlockSpec((1, 1, t, 1), lambda i, h, j: (i, h, j, 0)),
                  pl.BlockSpec((1, 1, 1, s), lambda i, h, j: (i, h, 0, 0))],
        out_specs=pl.BlockSpec((1, t, hd), lambda i, h, j: (i, j, h)),
        out_shape=jax.ShapeDtypeStruct((b, s, n_heads * hd), BF16),
        scratch_shapes=[pltpu.VMEM((t, 1), F32), pltpu.VMEM((t, 1), F32), pltpu.VMEM((t, hd), F32)],
        compiler_params=_cparams(("parallel", "parallel", "parallel")),
        name="fox_attention",
    )(qkv, qkv, qkv, cum_q, cum_k)


def _diff_kernel(q_ref, k_ref, v_ref, bias_ref, rel_ref, lam_ref, gain_ref, o_ref,
                 m_ref, l_ref, acc_ref, *, t, scale, lam_init):
    hh = pl.program_id(1)
    qi = pl.program_id(2)
    m_ref[...] = jnp.full_like(m_ref, NEG_INF)
    l_ref[...] = jnp.zeros_like(l_ref)
    acc_ref[...] = jnp.zeros_like(acc_ref)
    q = q_ref[0]
    hd = HEAD_DIM

    def tile(kj, kind):
        k0 = pl.multiple_of(kj * t, t)
        k = k_ref[0, pl.ds(k0, t), :]
        v = v_ref[0, pl.ds(k0, t), :]
        for j in range(2):
            s = _qk(q[:, j * hd:(j + 1) * hd], k[:, j * hd:(j + 1) * hd], scale)
            if kind == "far":
                s = s + rel_ref[NUM_BUCKETS - 1, 2 * hh + j]
            elif kind == "near":
                s = s + bias_ref[j, 1]
            else:
                s = jnp.where(_causal_tile_mask(t), s + bias_ref[j, 0], NEG_INF)
            _softmax_step(s, v, m_ref.at[j], l_ref.at[j], acc_ref.at[j])

    def body(kj, carry):
        tile(kj, "far")
        return carry

    lax.fori_loop(0, jnp.maximum(qi - 1, 0), body, 0)

    @pl.when(qi >= 1)
    def _():
        tile(qi - 1, "near")

    tile(qi, "diag")

    lp = lam_ref[...]
    lam = (jnp.exp(jnp.sum(lp[0:1] * lp[1:2], axis=-1, keepdims=True))
           - jnp.exp(jnp.sum(lp[2:3] * lp[3:4], axis=-1, keepdims=True)) + lam_init)
    o = acc_ref[0] / l_ref[0] - lam * (acc_ref[1] / l_ref[1])
    ms = jnp.mean(o * o, axis=-1, keepdims=True)
    y = o * lax.rsqrt(ms + RMS_EPS) * gain_ref[...] * (1.0 - lam_init)
    o_ref[0] = y.astype(o_ref.dtype)


def _diff_attention(qkv, bias_tiles, rel_bias, lam_p, sub_gain, n_heads, lam_init):
    b, s, _ = qkv.shape
    t = bias_tiles.shape[-1]
    w = 2 * HEAD_DIM
    return pl.pallas_call(
        functools.partial(_diff_kernel, t=t, scale=HEAD_DIM ** -0.5, lam_init=lam_init),
        grid=(b, n_heads, s // t),
        in_specs=[pl.BlockSpec((1, t, w), lambda i, h, j: (i, j, h)),
                  pl.BlockSpec((1, s, w), lambda i, h, j: (i, 0, n_heads + h)),
                  pl.BlockSpec((1, s, w), lambda i, h, j: (i, 0, 2 * n_heads + h)),
                  pl.BlockSpec((2, 2, t, t), lambda i, h, j: (h, 0, 0, 0)),
                  pl.BlockSpec(memory_space=pltpu.SMEM),
                  pl.BlockSpec((4, HEAD_DIM), lambda i, h, j: (0, 0)),
                  pl.BlockSpec((1, w), lambda i, h, j: (0, 0))],
        out_specs=pl.BlockSpec((1, t, w), lambda i, h, j: (i, j, h)),
        out_shape=jax.ShapeDtypeStruct((b, s, n_heads * w), BF16),
        scratch_shapes=[pltpu.VMEM((2, t, 1), F32), pltpu.VMEM((2, t, 1), F32),
                        pltpu.VMEM((2, t, w), F32)],
        compiler_params=_cparams(("parallel", "parallel", "parallel")),
        name="diff_attention",
    )(qkv, qkv, qkv, bias_tiles, rel_bias, lam_p, sub_gain.reshape(1, w))


def _float_key(x):
    bits = pltpu.bitcast(x, jnp.int32)
    return jnp.where(bits < 0, bits ^ jnp.int32(0x7FFFFFFF), bits)


def _count(mask):
    return jnp.sum(jnp.where(mask, 1.0, 0.0), axis=-1, keepdims=True)


def _indexer_kernel(qidx_ref, kx_ref, wq_ref, mask_ref, key_ref, *, t, s, n_heads, n_sel, kc):
    qi = pl.program_id(1)
    row = qi * t + lax.broadcasted_iota(jnp.int32, (t, 1), 0)
    wq = wq_ref[0]
    qh = [qidx_ref[0][:, h * IDX_DIM:(h + 1) * IDX_DIM].astype(BF16) for h in range(n_heads)]

    for c in range(s // kc):
        kk = kx_ref[0, c * kc:(c + 1) * kc, :][:, :IDX_DIM].astype(BF16)
        score = jnp.zeros((t, kc), F32)
        for h in range(n_heads):
            d = lax.dot_general(qh[h], kk, (((1,), (1,)), ((), ())), preferred_element_type=F32)
            score = score + wq[:, IDX_DIM + h:IDX_DIM + h + 1] * jnp.maximum(d, 0.0)
        col = c * kc + lax.broadcasted_iota(jnp.int32, (1, kc), 1)
        score = jnp.where(col <= row, score + 0.0, NEG_INF)
        key_ref[:, c * kc:(c + 1) * kc] = _float_key(score)

    def value_bit(it, tau_u):
        cand_u = tau_u | jnp.left_shift(jnp.int32(1), 31 - it)
        cnt = _count(key_ref[...] >= (cand_u ^ jnp.int32(INT_MIN)))
        return jnp.where(cnt >= n_sel, cand_u, tau_u)

    tau_u = lax.fori_loop(0, 32, value_bit, jnp.zeros((t, 1), jnp.int32))
    tau = tau_u ^ jnp.int32(INT_MIN)

    need = n_sel - _count(key_ref[...] > tau)
    col = lax.broadcasted_iota(jnp.int32, (1, s), 1)
    n_bits = int(math.log2(s))

    def index_bit(it, bound):
        cand = bound | jnp.left_shift(jnp.int32(1), n_bits - 1 - it)
        below = _count((key_ref[...] == tau) & (col < cand))
        return jnp.where(below < need, cand, bound)

    bound = lax.fori_loop(0, n_bits, index_bit, jnp.zeros((t, 1), jnp.int32))
    key = key_ref[...]
    sel = (key > tau) | ((key == tau) & (col <= bound))
    mask_ref[0] = jnp.where(sel & (col <= row), 1, 0).astype(jnp.int8)


def _indexer_mask(rest, t, n_heads, n_sel, qidx_block, kx_block):
    b, s, _ = rest.shape
    assert s & (s - 1) == 0
    kc = min(s, 512)
    wq = n_heads * IDX_DIM
    return pl.pallas_call(
        functools.partial(_indexer_kernel, t=t, s=s, n_heads=n_heads, n_sel=n_sel, kc=kc),
        grid=(b, s // t),
        in_specs=[pl.BlockSpec((1, t, wq), lambda i, j: (i, j, qidx_block)),
                  pl.BlockSpec((1, s, LANES), lambda i, j: (i, 0, kx_block)),
                  pl.BlockSpec((1, t, LANES), lambda i, j: (i, j, kx_block))],
        out_specs=pl.BlockSpec((1, t, s), lambda i, j: (i, j, 0)),
        out_shape=jax.ShapeDtypeStruct((b, s, s), jnp.int8),
        scratch_shapes=[pltpu.VMEM((t, s), jnp.int32)],
        compiler_params=_cparams(("parallel", "parallel")),
        name="dsa_indexer_topk_mask",
    )(rest, rest, rest)


def _dsa_kernel(q_ref, k_ref, v_ref, bias_ref, mask_ref, rel_ref, o_ref, m_ref, l_ref, acc_ref, *, t, scale):
    h = pl.program_id(1)
    qi = pl.program_id(2)
    m_ref[...] = jnp.full_like(m_ref, NEG_INF)
    l_ref[...] = jnp.zeros_like(l_ref)
    acc_ref[...] = jnp.zeros_like(acc_ref)
    q = q_ref[0]

    def tile(kj, kind):
        k0 = pl.multiple_of(kj * t, t)
        k = k_ref[0, pl.ds(k0, t), :]
        v = v_ref[0, pl.ds(k0, t), :]
        s = _qk(q, k, scale)
        if kind == "far":
            s = s + rel_ref[NUM_BUCKETS - 1, h]
        elif kind == "near":
            s = s + bias_ref[0, 1]
        else:
            s = s + bias_ref[0, 0]
        keep = mask_ref[0, :, pl.ds(k0, t)].astype(F32) > 0.5
        _softmax_step(jnp.where(keep, s, NEG_INF), v, m_ref, l_ref, acc_ref)

    def body(kj, carry):
        tile(kj, "far")
        return carry

    lax.fori_loop(0, jnp.maximum(qi - 1, 0), body, 0)

    @pl.when(qi >= 1)
    def _():
        tile(qi - 1, "near")

    tile(qi, "diag")
    o_ref[0] = (acc_ref[...] / l_ref[...]).astype(o_ref.dtype)


def _dsa_attention(q, kv, mask, bias_tiles, rel_bias, n_heads):
    b, s, _ = q.shape
    t = bias_tiles.shape[-1]
    hd = HEAD_DIM
    return pl.pallas_call(
        functools.partial(_dsa_kernel, t=t, scale=hd ** -0.5),
        grid=(b, n_heads, s // t),
        in_specs=[pl.BlockSpec((1, t, hd), lambda i, h, j: (i, j, h)),
                  pl.BlockSpec((1, s, hd), lambda i, h, j: (i, 0, h)),
                  pl.BlockSpec((1, s, hd), lambda i, h, j: (i, 0, n_heads + h)),
                  pl.BlockSpec((1, 2, t, t), lambda i, h, j: (h, 0, 0, 0)),
                  pl.BlockSpec((1, t, s), lambda i, h, j: (i, j, 0)),
                  pl.BlockSpec(memory_space=pltpu.SMEM)],
        out_specs=pl.BlockSpec((1, t, hd), lambda i, h, j: (i, j, h)),
        out_shape=jax.ShapeDtypeStruct((b, s, n_heads * hd), BF16),
        scratch_shapes=[pltpu.VMEM((t, 1), F32), pltpu.VMEM((t, 1), F32), pltpu.VMEM((t, hd), F32)],
        compiler_params=_cparams(("parallel", "parallel", "parallel")),
        name="dsa_attention",
    )(q, kv, kv, bias_tiles, mask, rel_bias)


def _router_kernel(h_ref, w_ref, b_ref, o_ref):
    logits = jnp.dot(h_ref[...], w_ref[...], preferred_element_type=F32) + b_ref[...]
    lane = lax.broadcasted_iota(jnp.int32, logits.shape, 1).astype(F32)
    top_v, top_hot = [], []
    for _ in range(TOP_K):
        mx = jnp.max(logits, axis=-1, keepdims=True)
        first = jnp.min(jnp.where(logits == mx, lane, float(LANES)), axis=-1, keepdims=True)
        hot = lane == first
        top_v.append(mx)
        top_hot.append(hot)
        logits = jnp.where(hot, -jnp.inf, logits)
    e = [jnp.exp(v - top_v[0]) for v in top_v]
    denom = e[0] + e[1] + e[2] + e[3]
    comb = jnp.zeros(logits.shape, F32)
    for k in range(TOP_K):
        comb = comb + jnp.where(top_hot[k], e[k] / denom, 0.0)
    o_ref[...] = comb


def _router(h2, w_router, b_router):
    n, d = h2.shape
    ne = w_router.shape[1]
    tm = _row_tile(n, 1024)
    w = _pad_cols(w_router).astype(BF16)
    bias = jnp.concatenate([b_router, jnp.full((LANES - ne,), NEG_INF, F32)]).reshape(1, LANES)
    return pl.pallas_call(
        _router_kernel,
        grid=(n // tm,),
        in_specs=[pl.BlockSpec((tm, d), lambda i: (i, 0)),
                  pl.BlockSpec((d, LANES), lambda i: (0, 0)),
                  pl.BlockSpec((1, LANES), lambda i: (0, 0))],
        out_specs=pl.BlockSpec((tm, LANES), lambda i: (i, 0)),
        out_shape=jax.ShapeDtypeStruct((n, LANES), F32),
        compiler_params=_cparams(("parallel",)),
        name="moe_router_top4",
    )(h2, w, bias)


def _expert_ffn(h, wg, wu, bg, bu, wd, bd):
    g = jnp.dot(h, wg, preferred_element_type=F32) + bg
    u = jnp.dot(h, wu, preferred_element_type=F32) + bu
    g = jnp.minimum(g, SWIGLU_LIMIT)
    u = jnp.clip(u, -SWIGLU_LIMIT, SWIGLU_LIMIT)
    act = (u + 1.0) * (g * _sigmoid(g * SWIGLU_ALPHA))
    return jnp.dot(act.astype(BF16), wd, preferred_element_type=F32) + bd


def _moe_dense_kernel(h_ref, comb_ref, wg_ref, wu_ref, bg_ref, bu_ref, wd_ref, bd_ref, x_ref, g2_ref,
                      o_ref, acc_ref):
    e = pl.program_id(1)

    @pl.when(e == 0)
    def _():
        acc_ref[...] = jnp.zeros_like(acc_ref)

    y = _expert_ffn(h_ref[...], wg_ref[0], wu_ref[0], bg_ref[0], bu_ref[0], wd_ref[0], bd_ref[0])
    comb = comb_ref[...]
    lane = lax.broadcasted_iota(jnp.int32, comb.shape, 1)
    wcol = jnp.sum(jnp.where(lane == e, comb, 0.0), axis=-1, keepdims=True)
    acc_ref[...] += wcol * y

    @pl.when(e == pl.num_programs(1) - 1)
    def _():
        o_ref[...] = x_ref[...] + g2_ref[0] * acc_ref[...]


def _moe_dense(h2, comb, wg, wu, bg, bu, wd, bd, x2, g2, rows_per_batch):
    n, d = h2.shape
    ne, _, f = wg.shape
    tm = _row_tile(rows_per_batch, 512)
    tiles_per_batch = rows_per_batch // tm
    nb = g2.shape[0]
    row = lambda i, e: (i, 0)
    exp3 = lambda i, e: (e, 0, 0)
    return pl.pallas_call(
        _moe_dense_kernel,
        grid=(n // tm, ne),
        in_specs=[pl.BlockSpec((tm, d), row),
                  pl.BlockSpec((tm, LANES), row),
                  pl.BlockSpec((1, d, f), exp3), pl.BlockSpec((1, d, f), exp3),
                  pl.BlockSpec((1, 1, f), exp3), pl.BlockSpec((1, 1, f), exp3),
                  pl.BlockSpec((1, f, d), exp3), pl.BlockSpec((1, 1, d), exp3),
                  pl.BlockSpec((tm, d), row),
                  pl.BlockSpec((1, 1, d), lambda i, e: (i // tiles_per_batch, 0, 0))],
        out_specs=pl.BlockSpec((tm, d), row),
        out_shape=jax.ShapeDtypeStruct((n, d), F32),
        scratch_shapes=[pltpu.VMEM((tm, d), F32)],
        compiler_params=_cparams(("parallel", "arbitrary")),
        name="moe_experts",
    )(h2, comb, wg, wu, bg.reshape(ne, 1, f), bu.reshape(ne, 1, f), wd, bd.reshape(ne, 1, d),
      x2, g2.reshape(nb, 1, d))


def _dsa_layer(h, x, g1, w_in, kv_gain, w_uk, w_uv, w_out, rel_bias, bias_tiles):
    b, s, d = h.shape
    n_heads = d // HEAD_DIM
    rank = kv_gain.shape[0]
    idx_heads = n_heads // 2
    n_sel = min(TOPK_MAX, s // 4)
    o1 = n_heads * HEAD_DIM
    h2 = h.reshape(b * s, d)
    q = _matmul(h2, w_in[:, :o1].astype(BF16), BF16).reshape(b, s, o1)
    w_rest = _pad_cols(w_in[:, o1:]).astype(BF16)
    rest = _matmul(h2, w_rest, F32, tn=w_rest.shape[1]).reshape(b, s, -1)
    assert rank % LANES == 0 and (idx_heads * IDX_DIM) == rank
    ckv = _norm(rest, kv_gain, BF16)
    w_k = jnp.transpose(w_uk, (2, 0, 1)).reshape(rank, o1)
    w_v = jnp.transpose(w_uv, (1, 0, 2)).reshape(rank, o1)
    w_kv = jnp.concatenate([w_k, w_v], axis=1).astype(BF16)
    kv = _matmul(ckv.reshape(b * s, rank), w_kv, BF16).reshape(b, s, 2 * o1)
    t = bias_tiles.shape[-1]
    mask = _indexer_mask(rest, t, idx_heads, n_sel, qidx_block=1, kx_block=2 * rank // LANES)
    o = _dsa_attention(q, kv, mask, bias_tiles, rel_bias, n_heads)
    return _matmul_residual(o.reshape(b * s, o1), w_out.astype(BF16), x.reshape(b * s, d), g1, s)


def _fox_layer(h, x, g1, w_in, f_bias, w_out):
    b, s, d = h.shape
    n_heads = f_bias.shape[0]
    hd3 = 3 * n_heads * HEAD_DIM
    h2 = h.reshape(b * s, d)
    qkv = _matmul(h2, w_in[:, :hd3].astype(BF16), BF16).reshape(b, s, hd3)
    w_f = _pad_cols(w_in[:, hd3:]).astype(BF16)
    fg = _matmul(h2, w_f, F32, tn=w_f.shape[1]).reshape(b, s, -1)
    cum = _logsig_cumsum(fg, f_bias)
    o = _fox_attention(qkv, cum, n_heads)
    return _matmul_residual(o.reshape(b * s, -1), w_out.astype(BF16), x.reshape(b * s, d), g1, s)


def _diff_layer(h, x, g1, w_in, lam_p, sub_gain, w_out, rel_bias, bias_tiles, lam_init):
    b, s, d = h.shape
    n_heads = w_in.shape[1] // (6 * HEAD_DIM)
    h2 = h.reshape(b * s, d)
    qkv = _matmul(h2, w_in.astype(BF16), BF16).reshape(b, s, -1)
    o = _diff_attention(qkv, bias_tiles, rel_bias, lam_p, sub_gain, n_heads, lam_init)
    return _matmul_residual(o.reshape(b * s, -1), w_out.astype(BF16), x.reshape(b * s, d), g1, s)


def _moe_layer(h, x2, g2, w_router, b_router, w_gu, b_gu, w_dn, b_dn):
    b, s, d = h.shape
    h2 = h.reshape(b * s, d)
    comb = _router(h2, w_router, b_router)
    wg = w_gu[:, :, 0::2].astype(BF16)
    wu = w_gu[:, :, 1::2].astype(BF16)
    return _moe_dense(h2, comb, wg, wu, b_gu[:, 0::2], b_gu[:, 1::2], w_dn.astype(BF16), b_dn, x2, g2, s)


def kernel(x, c, rel_bias, norm_mix, norm_ffn, w_mod, b_mod, dsa_w_in, dsa_kv_gain, dsa_w_uk, dsa_w_uv, dsa_w_out, fox_w_in, fox_forget_bias, fox_w_out, diff_w_in, diff_lambda, diff_subln_gain, diff_w_out, w_router, b_router, w_gate_up, b_gate_up, w_down, b_down, final_norm):
    b, s, d = x.shape
    depth = w_mod.shape[0]
    mod = _modulation(c, w_mod, b_mod)
    t_attn = _row_tile(s, 256)
    bias_tiles = _bias_tiles(rel_bias, t_attn)
    ia = ib = ic = 0
    for i in range(depth):
        sh1, sc1, g1, sh2, sc2, g2 = (mod[i, :, k * d:(k + 1) * d] for k in range(6))
        h = _norm_mod(x, norm_mix[i], sc1, sh1)
        kind = i % N_MIXERS
        if kind == 0:
            x2 = _dsa_layer(h, x, g1, dsa_w_in[ia], dsa_kv_gain[ia], dsa_w_uk[ia], dsa_w_uv[ia],
                            dsa_w_out[ia], rel_bias, bias_tiles)
            ia += 1
        elif kind == 1:
            x2 = _fox_layer(h, x, g1, fox_w_in[ib], fox_forget_bias[ib], fox_w_out[ib])
            ib += 1
        else:
            lam_init = 0.8 - 0.6 * math.exp(-0.3 * i)
            x2 = _diff_layer(h, x, g1, diff_w_in[ic], diff_lambda[ic], diff_subln_gain[ic],
                             diff_w_out[ic], rel_bias, bias_tiles, lam_init)
            ic += 1
        x = x2.reshape(b, s, d)
        h = _norm_mod(x, norm_ffn[i], sc2, sh2)
        x = _moe_layer(h, x2, g2, w_router[i], b_router[i], w_gate_up[i], b_gate_up[i],
                       w_down[i], b_down[i]).reshape(b, s, d)
    return _norm(x, final_norm, x.dtype)
```

```python
import functools
import math

import numpy as np
import jax
import jax.numpy as jnp
from jax import lax
from jax.experimental import pallas as pl
from jax.experimental.pallas import tpu as pltpu

HEAD_DIM = 128
IDX_DIM = 64
TOPK_MAX = 256
TOP_K = 4
SWIGLU_LIMIT = 7.0
SWIGLU_ALPHA = 1.702
NUM_BUCKETS = 32
MAX_DISTANCE = 128
RMS_EPS = 1e-6
NEG_INF = -1e30
N_MIXERS = 3
LANES = 128
SUBLANES = 8
INT_MIN = -(2 ** 31)

F32 = jnp.float32
BF16 = jnp.bfloat16

VMEM_LIMIT = 56 * 1024 * 1024
HEADS_PER_STEP = 4


def _cparams(sem):
    return pltpu.CompilerParams(dimension_semantics=sem, vmem_limit_bytes=VMEM_LIMIT)


def _sigmoid(x):
    return 1.0 / (1.0 + jnp.exp(-x))


def _row_tile(s, want):
    t = min(s, want)
    assert s % t == 0
    return t


def _pad_cols(w, mult=LANES):
    n = w.shape[-1]
    pad = (-n) % mult
    if pad:
        w = jnp.pad(w, [(0, 0)] * (w.ndim - 1) + [(0, pad)])
    return w


def _mod_kernel(c_ref, w_ref, b_ref, o_ref):
    c = c_ref[...]
    cs = (c * _sigmoid(c)).astype(BF16)
    o_ref[0] = jnp.dot(cs, w_ref[0].astype(BF16), preferred_element_type=F32) + b_ref[0]


def _modulation(c, w_mod, b_mod):
    depth, d, n = w_mod.shape
    b = c.shape[0]
    tn = min(n, 1024)
    return pl.pallas_call(
        _mod_kernel,
        grid=(depth, n // tn),
        in_specs=[pl.BlockSpec((b, d), lambda i, j: (0, 0)),
                  pl.BlockSpec((1, d, tn), lambda i, j: (i, 0, j)),
                  pl.BlockSpec((1, 1, tn), lambda i, j: (i, 0, j))],
        out_specs=pl.BlockSpec((1, b, tn), lambda i, j: (i, 0, j)),
        out_shape=jax.ShapeDtypeStruct((depth, b, n), F32),
        compiler_params=_cparams(("parallel", "parallel")),
        name="adaln_modulation",
    )(c, w_mod, b_mod.reshape(depth, 1, n))


def _norm_mod_kernel(x_ref, g_ref, sc_ref, sh_ref, o_ref):
    x = x_ref[0]
    ms = jnp.mean(x * x, axis=-1, keepdims=True)
    y = x * lax.rsqrt(ms + RMS_EPS) * g_ref[...]
    o_ref[0] = (y * (1.0 + sc_ref[0]) + sh_ref[0]).astype(o_ref.dtype)


def _norm_kernel(x_ref, g_ref, o_ref):
    x = x_ref[0]
    ms = jnp.mean(x * x, axis=-1, keepdims=True)
    o_ref[0] = (x * lax.rsqrt(ms + RMS_EPS) * g_ref[...]).astype(o_ref.dtype)


def _norm_mod(x, g, sc, sh, out_dtype=BF16):
    b, s, d = x.shape
    ts = _row_tile(s, 512)
    vec = pl.BlockSpec((1, 1, d), lambda i, j: (i, 0, 0))
    return pl.pallas_call(
        _norm_mod_kernel,
        grid=(b, s // ts),
        in_specs=[pl.BlockSpec((1, ts, d), lambda i, j: (i, j, 0)),
                  pl.BlockSpec((1, d), lambda i, j: (0, 0)), vec, vec],
        out_specs=pl.BlockSpec((1, ts, d), lambda i, j: (i, j, 0)),
        out_shape=jax.ShapeDtypeStruct((b, s, d), out_dtype),
        compiler_params=_cparams(("parallel", "parallel")),
        name="rmsnorm_adaln",
    )(x, g.reshape(1, d), sc.reshape(b, 1, d), sh.reshape(b, 1, d))


def _norm(x, g, out_dtype, col_block=0):
    b, s, _ = x.shape
    d = g.shape[-1]
    ts = _row_tile(s, 512)
    return pl.pallas_call(
        _norm_kernel,
        grid=(b, s // ts),
        in_specs=[pl.BlockSpec((1, ts, d), lambda i, j: (i, j, col_block)),
                  pl.BlockSpec((1, d), lambda i, j: (0, 0))],
        out_specs=pl.BlockSpec((1, ts, d), lambda i, j: (i, j, 0)),
        out_shape=jax.ShapeDtypeStruct((b, s, d), out_dtype),
        compiler_params=_cparams(("parallel", "parallel")),
        name="rmsnorm",
    )(x, g.reshape(1, d))


def _mm_kernel(a_ref, b_ref, o_ref):
    o_ref[...] = jnp.dot(a_ref[...], b_ref[...], preferred_element_type=F32).astype(o_ref.dtype)


def _mm_res_kernel(a_ref, b_ref, x_ref, g_ref, o_ref):
    y = jnp.dot(a_ref[...], b_ref[...], preferred_element_type=F32)
    o_ref[...] = x_ref[...] + g_ref[0] * y


def _matmul(a, w, out_dtype, tm=1024, tn=512):
    m, k = a.shape
    n = w.shape[1]
    tm = _row_tile(m, tm)
    tn = _row_tile(n, tn)
    return pl.pallas_call(
        _mm_kernel,
        grid=(m // tm, n // tn),
        in_specs=[pl.BlockSpec((tm, k), lambda i, j: (i, 0)),
                  pl.BlockSpec((k, tn), lambda i, j: (0, j))],
        out_specs=pl.BlockSpec((tm, tn), lambda i, j: (i, j)),
        out_shape=jax.ShapeDtypeStruct((m, n), out_dtype),
        compiler_params=_cparams(("parallel", "parallel")),
        name="matmul",
    )(a, w)


def _matmul_residual(a, w, x, gate, rows_per_batch, tm=1024, tn=512):
    m, k = a.shape
    n = w.shape[1]
    tm = _row_tile(rows_per_batch, tm)
    tn = _row_tile(n, tn)
    tiles_per_batch = rows_per_batch // tm
    nb = gate.shape[0]
    return pl.pallas_call(
        _mm_res_kernel,
        grid=(m // tm, n // tn),
        in_specs=[pl.BlockSpec((tm, k), lambda i, j: (i, 0)),
                  pl.BlockSpec((k, tn), lambda i, j: (0, j)),
                  pl.BlockSpec((tm, tn), lambda i, j: (i, j)),
                  pl.BlockSpec((1, 1, tn), lambda i, j: (i // tiles_per_batch, 0, j))],
        out_specs=pl.BlockSpec((tm, tn), lambda i, j: (i, j)),
        out_shape=jax.ShapeDtypeStruct((m, n), F32),
        compiler_params=_cparams(("parallel", "parallel")),
        name="matmul_residual",
    )(a, w, x, gate.reshape(nb, 1, n))


def _softmax_step_t(st, vt, carry, acc_ref):
    m_old, l_old = carry
    m_new = jnp.maximum(m_old, jnp.max(st, axis=0, keepdims=True))
    alpha = jnp.exp(m_old - m_new)
    p = jnp.exp(st - m_new)
    l_new = alpha * l_old + jnp.sum(p, axis=0, keepdims=True)
    acc_ref[...] = alpha * acc_ref[...] + jnp.dot(vt, p.astype(vt.dtype), preferred_element_type=F32)
    return m_new, l_new


def _softmax_init(t):
    return jnp.full((1, t), NEG_INF, F32), jnp.zeros((1, t), F32)


def _kq(k, q, scale):
    return lax.dot_general(k, q, (((1,), (1,)), ((), ())), preferred_element_type=F32) * scale


def _causal_tile_mask_t(t):
    r = lax.broadcasted_iota(jnp.int32, (t, t), 0)
    c = lax.broadcasted_iota(jnp.int32, (t, t), 1)
    return r <= c


def _t5_bucket_np(dist):
    n = np.maximum(dist, 0)
    max_exact = NUM_BUCKETS // 2
    nf = np.maximum(n, 1).astype(np.float32)
    large = max_exact + (np.log(nf / np.float32(max_exact)) / np.float32(math.log(MAX_DISTANCE / max_exact))
                         * np.float32(NUM_BUCKETS - max_exact)).astype(np.int32)
    large = np.minimum(large, NUM_BUCKETS - 1)
    return np.where(n < max_exact, n, large).astype(np.int32)


def _bias_tile_kernel(bk_ref, rel_ref, o_ref):
    m = pl.program_id(0)
    bk = bk_ref[...]
    acc = jnp.zeros(bk.shape, F32)
    for b in range(NUM_BUCKETS):
        acc = jnp.where(bk == b, rel_ref[b, m], acc)
    o_ref[0] = acc


def _bias_tiles(rel_bias, t):
    assert t >= MAX_DISTANCE
    n_maps = rel_bias.shape[1]
    kk = np.arange(t)[:, None]
    qq = np.arange(t)[None, :]
    buckets = np.stack([_t5_bucket_np(qq - kk), _t5_bucket_np(t + qq - kk)])
    return pl.pallas_call(
        _bias_tile_kernel,
        grid=(n_maps,),
        in_specs=[pl.BlockSpec((2, t, t), lambda m: (0, 0, 0)),
                  pl.BlockSpec(memory_space=pltpu.SMEM)],
        out_specs=pl.BlockSpec((1, 2, t, t), lambda m: (m, 0, 0, 0)),
        out_shape=jax.ShapeDtypeStruct((n_maps, 2, t, t), F32),
        compiler_params=_cparams(("parallel",)),
        name="t5_bias_tiles",
    )(jnp.asarray(buckets), rel_bias)


def _key_tile_sweep(qi, tile, carry):
    n_far = jnp.maximum(qi - 1, 0)
    carry = lax.fori_loop(0, n_far, lambda kj, c: tile(kj, c, "far"), carry)
    carry = lax.fori_loop(n_far, qi, lambda kj, c: tile(kj, c, "near"), carry)
    return tile(qi, carry, "diag")


def _logsig_cumsum_kernel(f_ref, fb_ref, o_ref, carry_ref):
    @pl.when(pl.program_id(1) == 0)
    def _():
        carry_ref[...] = jnp.zeros_like(carry_ref)

    z = f_ref[0] + fb_ref[...]
    logf = -(jnp.maximum(-z, 0.0) + jnp.log(1.0 + jnp.exp(-jnp.abs(z))))
    t = z.shape[0]
    r = lax.broadcasted_iota(jnp.int32, (t, t), 0)
    c = lax.broadcasted_iota(jnp.int32, (t, t), 1)
    tri = jnp.where(r >= c, 1.0, 0.0).astype(BF16)
    hi = logf.astype(BF16)
    r1 = logf - hi.astype(F32)
    mid = r1.astype(BF16)
    lo = (r1 - mid.astype(F32)).astype(BF16)
    cum = (jnp.dot(tri, hi, preferred_element_type=F32)
           + jnp.dot(tri, mid, preferred_element_type=F32)
           + jnp.dot(tri, lo, preferred_element_type=F32)) + carry_ref[...]
    o_ref[0] = cum
    carry_ref[...] = cum[t - 1:t, :]


def _logsig_cumsum(fg, f_bias):
    b, s, w = fg.shape
    tc = _row_tile(s, 512)
    fb = jnp.pad(f_bias, (0, w - f_bias.shape[0])).reshape(1, w)
    return pl.pallas_call(
        _logsig_cumsum_kernel,
        grid=(b, s // tc),
        in_specs=[pl.BlockSpec((1, tc, w), lambda i, j: (i, j, 0)),
                  pl.BlockSpec((1, w), lambda i, j: (0, 0))],
        out_specs=pl.BlockSpec((1, tc, w), lambda i, j: (i, j, 0)),
        out_shape=jax.ShapeDtypeStruct((b, s, w), F32),
        scratch_shapes=[pltpu.VMEM((1, w), F32)],
        compiler_params=_cparams(("parallel", "arbitrary")),
        name="fox_logsig_cumsum",
    )(fg, fb)


def _fox_kernel(q_ref, k_ref, vt_ref, cq_ref, ck_ref, o_ref, acc_ref, *, t, scale, group):
    qi = pl.program_id(2)
    acc_ref[...] = jnp.zeros_like(acc_ref)
    hd = HEAD_DIM
    q = q_ref[0]

    def tile(kj, carry, diag):
        k0 = pl.multiple_of(kj * t, t)
        k = k_ref[0, pl.ds(k0, t), :]
        sts = [_kq(k[:, g * hd:(g + 1) * hd], q[:, g * hd:(g + 1) * hd], scale) for g in range(group)]
        out = []
        for g in range(group):
            ck = jnp.tile(ck_ref[0, g, pl.ds(k0, t), :], (1, t // LANES))
            st = sts[g] + (cq_ref[0, g] - ck)
            if diag:
                st = jnp.where(_causal_tile_mask_t(t), st, NEG_INF)
            out.append(_softmax_step_t(st, vt_ref[0, g * hd:(g + 1) * hd, pl.ds(k0, t)], carry[g], acc_ref.at[g]))
        return tuple(out)

    carry = lax.fori_loop(0, qi, lambda kj, c: tile(kj, c, False), (_softmax_init(t),) * group)
    carry = tile(qi, carry, True)
    o_ref[0] = jnp.concatenate([(acc_ref[g] / carry[g][1]).T for g in range(group)], axis=1).astype(o_ref.dtype)


def _fox_attention(qkv, cum, n_heads, group=HEADS_PER_STEP):
    b, s, _ = qkv.shape
    t = _row_tile(s, 256)
    hd = HEAD_DIM
    group = min(group, n_heads)
    w = group * hd
    ng = n_heads // group
    cum_t = jnp.transpose(cum[:, :, :n_heads], (0, 2, 1))
    cum_q = cum_t[:, :, None, :]
    cum_k = jnp.broadcast_to(cum_t[..., None], (b, n_heads, s, LANES))
    vt = jnp.transpose(qkv[:, :, 2 * n_heads * hd:], (0, 2, 1))
    return pl.pallas_call(
        functools.partial(_fox_kernel, t=t, scale=hd ** -0.5, group=group),
        grid=(b, ng, s // t),
        in_specs=[pl.BlockSpec((1, t, w), lambda i, h, j: (i, j, h)),
                  pl.BlockSpec((1, s, w), lambda i, h, j: (i, 0, ng + h)),
                  pl.BlockSpec((1, w, s), lambda i, h, j: (i, h, 0)),
                  pl.BlockSpec((1, group, 1, t), lambda i, h, j: (i, h, 0, j)),
                  pl.BlockSpec((1, group, s, LANES), lambda i, h, j: (i, h, 0, 0))],
        out_specs=pl.BlockSpec((1, t, w), lambda i, h, j: (i, j, h)),
        out_shape=jax.ShapeDtypeStruct((b, s, n_heads * hd), BF16),
        scratch_shapes=[pltpu.VMEM((group, hd, t), F32)],
        compiler_params=_cparams(("parallel", "parallel", "arbitrary")),
        name="fox_attention",
    )(qkv, qkv, vt, cum_q, cum_k)


def _diff_kernel(q_ref, k_ref, vt_ref, bias_ref, rel_ref, lam_ref, gain_ref, o_ref, acc_ref,
                 *, t, scale, lam_init, group):
    hg = pl.program_id(1)
    qi = pl.program_id(2)
    acc_ref[...] = jnp.zeros_like(acc_ref)
    q = q_ref[0]
    hd = HEAD_DIM
    n_chain = 2 * group

    def tile(kj, carry, kind):
        k0 = pl.multiple_of(kj * t, t)
        k = k_ref[0, pl.ds(k0, t), :]
        sts = [_kq(k[:, c * hd:(c + 1) * hd], q[:, c * hd:(c + 1) * hd], scale) for c in range(n_chain)]
        out = []
        for c in range(n_chain):
            if kind == "far":
                st = sts[c] + rel_ref[NUM_BUCKETS - 1, n_chain * hg + c]
            elif kind == "near":
                st = sts[c] + bias_ref[c, 1]
            else:
                st = jnp.where(_causal_tile_mask_t(t), sts[c] + bias_ref[c, 0], NEG_INF)
            vt = vt_ref[0, (c // 2) * 2 * hd:(c // 2 + 1) * 2 * hd, pl.ds(k0, t)]
            out.append(_softmax_step_t(st, vt, carry[c], acc_ref.at[c]))
        return tuple(out)

    carry = _key_tile_sweep(qi, tile, (_softmax_init(t),) * n_chain)

    lp = lam_ref[...]
    lam = (jnp.exp(jnp.sum(lp[0:1] * lp[1:2], axis=-1, keepdims=True))
           - jnp.exp(jnp.sum(lp[2:3] * lp[3:4], axis=-1, keepdims=True)) + lam_init)
    outs = []
    for g in range(group):
        o = (acc_ref[2 * g] / carry[2 * g][1] - lam * (acc_ref[2 * g + 1] / carry[2 * g + 1][1])).T
        ms = jnp.mean(o * o, axis=-1, keepdims=True)
        outs.append(o * lax.rsqrt(ms + RMS_EPS) * gain_ref[...] * (1.0 - lam_init))
    o_ref[0] = jnp.concatenate(outs, axis=1).astype(o_ref.dtype)


def _diff_attention(qkv, bias_tiles, rel_bias, lam_p, sub_gain, n_heads, lam_init, group=HEADS_PER_STEP // 2):
    b, s, _ = qkv.shape
    t = bias_tiles.shape[-1]
    group = min(group, n_heads)
    hw = 2 * HEAD_DIM
    w = group * hw
    ng = n_heads // group
    vt = jnp.transpose(qkv[:, :, 2 * n_heads * hw:], (0, 2, 1))
    return pl.pallas_call(
        functools.partial(_diff_kernel, t=t, scale=HEAD_DIM ** -0.5, lam_init=lam_init, group=group),
        grid=(b, ng, s // t),
        in_specs=[pl.BlockSpec((1, t, w), lambda i, h, j: (i, j, h)),
                  pl.BlockSpec((1, s, w), lambda i, h, j: (i, 0, ng + h)),
                  pl.BlockSpec((1, w, s), lambda i, h, j: (i, h, 0)),
                  pl.BlockSpec((2 * group, 2, t, t), lambda i, h, j: (h, 0, 0, 0)),
                  pl.BlockSpec(memory_space=pltpu.SMEM),
                  pl.BlockSpec((4, HEAD_DIM), lambda i, h, j: (0, 0)),
                  pl.BlockSpec((1, hw), lambda i, h, j: (0, 0))],
        out_specs=pl.BlockSpec((1, t, w), lambda i, h, j: (i, j, h)),
        out_shape=jax.ShapeDtypeStruct((b, s, n_heads * hw), BF16),
        scratch_shapes=[pltpu.VMEM((2 * group, hw, t), F32)],
        compiler_params=_cparams(("parallel", "parallel", "arbitrary")),
        name="diff_attention",
    )(qkv, qkv, vt, bias_tiles, rel_bias, lam_p, sub_gain.reshape(1, hw))


def _float_key(x):
    bits = pltpu.bitcast(x, jnp.int32)
    return jnp.where(bits < 0, bits ^ jnp.int32(0x7FFFFFFF), bits)


def _indexer_kernel(q_ref, kx_ref, wt_ref, mask_ref, key_ref, *, t, s, n_heads, n_sel, kc):
    qi = pl.program_id(1)
    qpos = qi * t + lax.broadcasted_iota(jnp.int32, (1, t), 1)
    n_chunks = ((qi + 1) * t + kc - 1) // kc
    wt = wt_ref[0]
    qh = [q_ref[0][:, h * IDX_DIM:(h + 1) * IDX_DIM].astype(BF16) for h in range(n_heads)]

    def kpos_of(c):
        return c * kc + lax.broadcasted_iota(jnp.int32, (kc, 1), 0)

    def rows(c):
        return pl.ds(pl.multiple_of(c * kc, kc), kc)

    def score_chunk(c, carry):
        kk = kx_ref[0, rows(c), :][:, :IDX_DIM].astype(BF16)
        score = jnp.zeros((kc, t), F32)
        for h in range(n_heads):
            d = lax.dot_general(kk, qh[h], (((1,), (1,)), ((), ())), preferred_element_type=F32)
            score = score + wt[h:h + 1, :] * jnp.maximum(d, 0.0)
        score = jnp.where(kpos_of(c) <= qpos, score + 0.0, NEG_INF)
        key_ref[rows(c), :] = _float_key(score)
        return carry

    lax.fori_loop(0, n_chunks, score_chunk, 0)

    def count(pred):
        def body(c, acc):
            hit = jnp.where(pred(key_ref[rows(c), :], kpos_of(c)), 1.0, 0.0)
            return acc + jnp.sum(hit.reshape(kc // SUBLANES, SUBLANES, t), axis=0)
        acc = lax.fori_loop(0, n_chunks, body, jnp.zeros((SUBLANES, t), F32))
        return jnp.sum(acc, axis=0, keepdims=True)

    def value_bit(it, tau_u):
        cand_u = tau_u | jnp.left_shift(jnp.int32(1), 31 - it)
        cand = cand_u ^ jnp.int32(INT_MIN)
        return jnp.where(count(lambda key, kpos: key >= cand) >= n_sel, cand_u, tau_u)

    tau = lax.fori_loop(0, 32, value_bit, jnp.zeros((1, t), jnp.int32)) ^ jnp.int32(INT_MIN)

    need = n_sel - count(lambda key, kpos: key > tau)
    n_bits = int(math.log2(s))

    def index_bit(it, bound):
        cand = bound | jnp.left_shift(jnp.int32(1), n_bits - 1 - it)
        below = count(lambda key, kpos: (key == tau) & (kpos < cand))
        return jnp.where(below < need, cand, bound)

    tied = jnp.max(count(lambda key, kpos: key >= tau)) > n_sel
    bound = lax.cond(tied,
                     lambda: lax.fori_loop(0, n_bits, index_bit, jnp.zeros((1, t), jnp.int32)),
                     lambda: jnp.full((1, t), s - 1, jnp.int32))

    def write_chunk(c, carry):
        key = key_ref[rows(c), :]
        kpos = kpos_of(c)
        sel = ((key > tau) | ((key == tau) & (kpos <= bound))) & (kpos <= qpos)
        mask_ref[0, rows(c), :] = jnp.where(sel, 1, 0).astype(jnp.int8)
        return carry

    def zero_chunk(c, carry):
        mask_ref[0, rows(c), :] = jnp.zeros((kc, t), jnp.int8)
        return carry

    lax.fori_loop(0, n_chunks, write_chunk, 0)
    lax.fori_loop(n_chunks, s // kc, zero_chunk, 0)


def _indexer_mask(rest, w_t, t, n_heads, n_sel, qidx_block, kx_block):
    b, s, _ = rest.shape
    assert s & (s - 1) == 0
    kc = min(s, 512)
    assert kc >= n_sel and kc % t == 0
    wq = n_heads * IDX_DIM
    return pl.pallas_call(
        functools.partial(_indexer_kernel, t=t, s=s, n_heads=n_heads, n_sel=n_sel, kc=kc),
        grid=(b, s // t),
        in_specs=[pl.BlockSpec((1, t, wq), lambda i, j: (i, j, qidx_block)),
                  pl.BlockSpec((1, s, LANES), lambda i, j: (i, 0, kx_block)),
                  pl.BlockSpec((1, n_heads, t), lambda i, j: (i, 0, j))],
        out_specs=pl.BlockSpec((1, s, t), lambda i, j: (i, 0, j)),
        out_shape=jax.ShapeDtypeStruct((b, s, s), jnp.int8),
        scratch_shapes=[pltpu.VMEM((s, t), jnp.int32)],
        compiler_params=_cparams(("parallel", "parallel")),
        name="dsa_indexer_topk_mask",
    )(rest, rest, w_t)


def _dsa_kernel(q_ref, k_ref, vt_ref, bias_ref, mask_ref, rel_ref, o_ref, acc_ref, *, t, scale, group):
    hg = pl.program_id(1)
    qi = pl.program_id(2)
    acc_ref[...] = jnp.zeros_like(acc_ref)
    q = q_ref[0]
    hd = HEAD_DIM

    def tile(kj, carry, kind):
        k0 = pl.multiple_of(kj * t, t)
        k = k_ref[0, pl.ds(k0, t), :]
        sts = [_kq(k[:, g * hd:(g + 1) * hd], q[:, g * hd:(g + 1) * hd], scale) for g in range(group)]
        keep = mask_ref[0, pl.ds(k0, t), :].astype(F32) > 0.5
        out = []
        for g in range(group):
            if kind == "far":
                st = sts[g] + rel_ref[NUM_BUCKETS - 1, group * hg + g]
            elif kind == "near":
                st = sts[g] + bias_ref[g, 1]
            else:
                st = sts[g] + bias_ref[g, 0]
            out.append(_softmax_step_t(jnp.where(keep, st, NEG_INF), vt_ref[0, g * hd:(g + 1) * hd, pl.ds(k0, t)],
                                       carry[g], acc_ref.at[g]))
        return tuple(out)

    carry = _key_tile_sweep(qi, tile, (_softmax_init(t),) * group)
    o_ref[0] = jnp.concatenate([(acc_ref[g] / carry[g][1]).T for g in range(group)], axis=1).astype(o_ref.dtype)


def _dsa_attention(q, kv, mask_t, bias_tiles, rel_bias, n_heads, group=HEADS_PER_STEP):
    b, s, _ = q.shape
    t = bias_tiles.shape[-1]
    hd = HEAD_DIM
    group = min(group, n_heads)
    w = group * hd
    ng = n_heads // group
    vt = jnp.transpose(kv[:, :, n_heads * hd:], (0, 2, 1))
    return pl.pallas_call(
        functools.partial(_dsa_kernel, t=t, scale=hd ** -0.5, group=group),
        grid=(b, ng, s // t),
        in_specs=[pl.BlockSpec((1, t, w), lambda i, h, j: (i, j, h)),
                  pl.BlockSpec((1, s, w), lambda i, h, j: (i, 0, h)),
                  pl.BlockSpec((1, w, s), lambda i, h, j: (i, h, 0)),
                  pl.BlockSpec((group, 2, t, t), lambda i, h, j: (h, 0, 0, 0)),
                  pl.BlockSpec((1, s, t), lambda i, h, j: (i, 0, j)),
                  pl.BlockSpec(memory_space=pltpu.SMEM)],
        out_specs=pl.BlockSpec((1, t, w), lambda i, h, j: (i, j, h)),
        out_shape=jax.ShapeDtypeStruct((b, s, n_heads * hd), BF16),
        scratch_shapes=[pltpu.VMEM((group, hd, t), F32)],
        compiler_params=_cparams(("parallel", "parallel", "arbitrary")),
        name="dsa_attention",
    )(q, kv, vt, bias_tiles, mask_t, rel_bias)


def _router_kernel(h_ref, w_ref, b_ref, o_ref):
    logits = jnp.dot(h_ref[...], w_ref[...], preferred_element_type=F32) + b_ref[...]
    lane = lax.broadcasted_iota(jnp.int32, logits.shape, 1).astype(F32)
    top_v, top_hot = [], []
    for _ in range(TOP_K):
        mx = jnp.max(logits, axis=-1, keepdims=True)
        first = jnp.min(jnp.where(logits == mx, lane, float(LANES)), axis=-1, keepdims=True)
        hot = lane == first
        top_v.append(mx)
        top_hot.append(hot)
        logits = jnp.where(hot, -jnp.inf, logits)
    e = [jnp.exp(v - top_v[0]) for v in top_v]
    denom = e[0] + e[1] + e[2] + e[3]
    comb = jnp.zeros(logits.shape, F32)
    for k in range(TOP_K):
        comb = comb + jnp.where(top_hot[k], e[k] / denom, 0.0)
    o_ref[...] = comb


def _router(h2, w_router, b_router):
    n, d = h2.shape
    ne = w_router.shape[1]
    tm = _row_tile(n, 1024)
    w = _pad_cols(w_router).astype(BF16)
    bias = jnp.concatenate([b_router, jnp.full((LANES - ne,), NEG_INF, F32)]).reshape(1, LANES)
    return pl.pallas_call(
        _router_kernel,
        grid=(n // tm,),
        in_specs=[pl.BlockSpec((tm, d), lambda i: (i, 0)),
                  pl.BlockSpec((d, LANES), lambda i: (0, 0)),
                  pl.BlockSpec((1, LANES), lambda i: (0, 0))],
        out_specs=pl.BlockSpec((tm, LANES), lambda i: (i, 0)),
        out_shape=jax.ShapeDtypeStruct((n, LANES), F32),
        compiler_params=_cparams(("parallel",)),
        name="moe_router_top4",
    )(h2, w, bias)


def _split_even_odd_kernel(w_ref, p_ref, o_ref):
    o_ref[0] = jnp.dot(w_ref[0].astype(BF16), p_ref[...], preferred_element_type=F32).astype(o_ref.dtype)


def _split_even_odd_cols(w):
    e, d, n = w.shape
    f = n // 2
    perm = np.zeros((n, n), np.float32)
    perm[2 * np.arange(f), np.arange(f)] = 1.0
    perm[2 * np.arange(f) + 1, f + np.arange(f)] = 1.0
    td = _row_tile(d, 512)
    return pl.pallas_call(
        _split_even_odd_kernel,
        grid=(e, d // td),
        in_specs=[pl.BlockSpec((1, td, n), lambda i, j: (i, j, 0)),
                  pl.BlockSpec((n, n), lambda i, j: (0, 0))],
        out_specs=pl.BlockSpec((1, td, n), lambda i, j: (i, j, 0)),
        out_shape=jax.ShapeDtypeStruct((e, d, n), BF16),
        compiler_params=_cparams(("parallel", "parallel")),
        name="gate_up_split_columns",
    )(w, jnp.asarray(perm, BF16))


def _expert_ffn(h, wgu, bgu, wd, bd):
    f = wgu.shape[1] // 2
    gu = jnp.dot(h, wgu, preferred_element_type=F32) + bgu
    g = jnp.minimum(gu[:, :f], SWIGLU_LIMIT)
    u = jnp.clip(gu[:, f:], -SWIGLU_LIMIT, SWIGLU_LIMIT)
    act = (u + 1.0) * (g * _sigmoid(g * SWIGLU_ALPHA))
    return jnp.dot(act.astype(BF16), wd, preferred_element_type=F32) + bd


def _moe_dense_kernel(h_ref, comb_ref, wgu_ref, bgu_ref, wd_ref, bd_ref, x_ref, g2_ref, o_ref, acc_ref):
    e = pl.program_id(1)

    @pl.when(e == 0)
    def _():
        acc_ref[...] = jnp.zeros_like(acc_ref)

    y = _expert_ffn(h_ref[...], wgu_ref[0], bgu_ref[0], wd_ref[0], bd_ref[0])
    comb = comb_ref[...]
    lane = lax.broadcasted_iota(jnp.int32, comb.shape, 1)
    wcol = jnp.sum(jnp.where(lane == e, comb, 0.0), axis=-1, keepdims=True)
    acc_ref[...] += wcol * y

    @pl.when(e == pl.num_programs(1) - 1)
    def _():
        o_ref[...] = x_ref[...] + g2_ref[0] * acc_ref[...]


def _moe_dense(h2, comb, wgu, bgu, wd, bd, x2, g2, rows_per_batch):
    n, d = h2.shape
    ne, _, f2 = wgu.shape
    tm = _row_tile(rows_per_batch, 512)
    tiles_per_batch = rows_per_batch // tm
    nb = g2.shape[0]
    row = lambda i, e: (i, 0)
    exp3 = lambda i, e: (e, 0, 0)
    return pl.pallas_call(
        _moe_dense_kernel,
        grid=(n // tm, ne),
        in_specs=[pl.BlockSpec((tm, d), row),
                  pl.BlockSpec((tm, LANES), row),
                  pl.BlockSpec((1, d, f2), exp3), pl.BlockSpec((1, 1, f2), exp3),
                  pl.BlockSpec((1, f2 // 2, d), exp3), pl.BlockSpec((1, 1, d), exp3),
                  pl.BlockSpec((tm, d), row),
                  pl.BlockSpec((1, 1, d), lambda i, e: (i // tiles_per_batch, 0, 0))],
        out_specs=pl.BlockSpec((tm, d), row),
        out_shape=jax.ShapeDtypeStruct((n, d), F32),
        scratch_shapes=[pltpu.VMEM((tm, d), F32)],
        compiler_params=_cparams(("parallel", "arbitrary")),
        name="moe_experts",
    )(h2, comb, wgu, bgu.reshape(ne, 1, f2), wd, bd.reshape(ne, 1, d), x2, g2.reshape(nb, 1, d))


def _dsa_layer(h, x, g1, w_in, kv_gain, w_uk, w_uv, w_out, rel_bias, bias_tiles):
    b, s, d = h.shape
    n_heads = d // HEAD_DIM
    rank = kv_gain.shape[0]
    idx_heads = n_heads // 2
    n_sel = min(TOPK_MAX, s // 4)
    o1 = n_heads * HEAD_DIM
    h2 = h.reshape(b * s, d)
    q = _matmul(h2, w_in[:, :o1].astype(BF16), BF16).reshape(b, s, o1)
    w_rest = _pad_cols(w_in[:, o1:]).astype(BF16)
    rest = _matmul(h2, w_rest, F32, tn=w_rest.shape[1]).reshape(b, s, -1)
    assert rank % LANES == 0 and (idx_heads * IDX_DIM) == rank
    ckv = _norm(rest, kv_gain, BF16)
    w_k = jnp.transpose(w_uk, (2, 0, 1)).reshape(rank, o1)
    w_v = jnp.transpose(w_uv, (1, 0, 2)).reshape(rank, o1)
    w_kv = jnp.concatenate([w_k, w_v], axis=1).astype(BF16)
    kv = _matmul(ckv.reshape(b * s, rank), w_kv, BF16).reshape(b, s, 2 * o1)
    t = bias_tiles.shape[-1]
    w_off = 2 * rank + IDX_DIM
    w_t = jnp.transpose(rest[:, :, w_off:w_off + idx_heads], (0, 2, 1))
    mask_t = _indexer_mask(rest, w_t, t, idx_heads, n_sel, qidx_block=1, kx_block=2 * rank // LANES)
    o = _dsa_attention(q, kv, mask_t, bias_tiles, rel_bias, n_heads)
    return _matmul_residual(o.reshape(b * s, o1), w_out.astype(BF16), x.reshape(b * s, d), g1, s)


def _fox_layer(h, x, g1, w_in, f_bias, w_out):
    b, s, d = h.shape
    n_heads = f_bias.shape[0]
    hd3 = 3 * n_heads * HEAD_DIM
    h2 = h.reshape(b * s, d)
    qkv = _matmul(h2, w_in[:, :hd3].astype(BF16), BF16).reshape(b, s, hd3)
    w_f = _pad_cols(w_in[:, hd3:]).astype(BF16)
    fg = _matmul(h2, w_f, F32, tn=w_f.shape[1]).reshape(b, s, -1)
    cum = _logsig_cumsum(fg, f_bias)
    o = _fox_attention(qkv, cum, n_heads)
    return _matmul_residual(o.reshape(b * s, -1), w_out.astype(BF16), x.reshape(b * s, d), g1, s)


def _diff_layer(h, x, g1, w_in, lam_p, sub_gain, w_out, rel_bias, bias_tiles, lam_init):
    b, s, d = h.shape
    n_heads = w_in.shape[1] // (6 * HEAD_DIM)
    h2 = h.reshape(b * s, d)
    qkv = _matmul(h2, w_in.astype(BF16), BF16).reshape(b, s, -1)
    o = _diff_attention(qkv, bias_tiles, rel_bias, lam_p, sub_gain, n_heads, lam_init)
    return _matmul_residual(o.reshape(b * s, -1), w_out.astype(BF16), x.reshape(b * s, d), g1, s)


def _moe_layer(h, x2, g2, w_router, b_router, wgu, b_gu, w_dn, b_dn):
    b, s, d = h.shape
    h2 = h.reshape(b * s, d)
    comb = _router(h2, w_router, b_router)
    bgu = jnp.concatenate([b_gu[:, 0::2], b_gu[:, 1::2]], axis=1)
    return _moe_dense(h2, comb, wgu, bgu, w_dn.astype(BF16), b_dn, x2, g2, s)


def kernel(x, c, rel_bias, norm_mix, norm_ffn, w_mod, b_mod, dsa_w_in, dsa_kv_gain, dsa_w_uk, dsa_w_uv, dsa_w_out, fox_w_in, fox_forget_bias, fox_w_out, diff_w_in, diff_lambda, diff_subln_gain, diff_w_out, w_router, b_router, w_gate_up, b_gate_up, w_down, b_down, final_norm):
    b, s, d = x.shape
    depth = w_mod.shape[0]
    mod = _modulation(c, w_mod, b_mod)
    t_attn = _row_tile(s, 256)
    bias_tiles = _bias_tiles(rel_bias, t_attn)
    ne, ff2 = w_gate_up.shape[1], w_gate_up.shape[3]
    wgu_all = _split_even_odd_cols(w_gate_up.reshape(depth * ne, d, ff2)).reshape(depth, ne, d, ff2)
    ia = ib = ic = 0
    for i in range(depth):
        sh1, sc1, g1, sh2, sc2, g2 = (mod[i, :, k * d:(k + 1) * d] for k in range(6))
        h = _norm_mod(x, norm_mix[i], sc1, sh1)
        kind = i % N_MIXERS
        if kind == 0:
            x2 = _dsa_layer(h, x, g1, dsa_w_in[ia], dsa_kv_gain[ia], dsa_w_uk[ia], dsa_w_uv[ia],
                            dsa_w_out[ia], rel_bias, bias_tiles)
            ia += 1
        elif kind == 1:
            x2 = _fox_layer(h, x, g1, fox_w_in[ib], fox_forget_bias[ib], fox_w_out[ib])
            ib += 1
        else:
            lam_init = 0.8 - 0.6 * math.exp(-0.3 * i)
            x2 = _diff_layer(h, x, g1, diff_w_in[ic], diff_lambda[ic], diff_subln_gain[ic],
                             diff_w_out[ic], rel_bias, bias_tiles, lam_init)
            ic += 1
        x = x2.reshape(b, s, d)
        h = _norm_mod(x, norm_ffn[i], sc2, sh2)
        x = _moe_layer(h, x2, g2, w_router[i], b_router[i], wgu_all[i], b_gate_up[i],
                       w_down[i], b_down[i]).reshape(b, s, d)
    return _norm(x, final_norm, x.dtype)
```

```python
import functools
import math

import numpy as np
import jax
import jax.numpy as jnp
from jax import lax
from jax.experimental import pallas as pl
from jax.experimental.pallas import tpu as pltpu

HEAD_DIM = 128
IDX_DIM = 64
TOPK_MAX = 256
TOP_K = 4
SWIGLU_LIMIT = 7.0
SWIGLU_ALPHA = 1.702
NUM_BUCKETS = 32
MAX_DISTANCE = 128
RMS_EPS = 1e-6
NEG_INF = -1e30
N_MIXERS = 3
LANES = 128
SUBLANES = 8
INT_MIN = -(2 ** 31)

F32 = jnp.float32
BF16 = jnp.bfloat16

VMEM_LIMIT = 56 * 1024 * 1024
HEADS_PER_STEP = 4
MOE_ROW_TILE = 256
MOE_TOKEN_TILE = 128


def _cparams(sem):
    return pltpu.CompilerParams(dimension_semantics=sem, vmem_limit_bytes=VMEM_LIMIT)


def _sigmoid(x):
    return 1.0 / (1.0 + jnp.exp(-x))


def _row_tile(s, want):
    t = min(s, want)
    assert s % t == 0
    return t


def _pad_cols(w, mult=LANES):
    n = w.shape[-1]
    pad = (-n) % mult
    if pad:
        w = jnp.pad(w, [(0, 0)] * (w.ndim - 1) + [(0, pad)])
    return w


def _mod_kernel(c_ref, w_ref, b_ref, o_ref):
    c = c_ref[...]
    cs = (c * _sigmoid(c)).astype(BF16)
    o_ref[0] = jnp.dot(cs, w_ref[0].astype(BF16), preferred_element_type=F32) + b_ref[0]


def _modulation(c, w_mod, b_mod):
    depth, d, n = w_mod.shape
    b = c.shape[0]
    tn = min(n, 1024)
    return pl.pallas_call(
        _mod_kernel,
        grid=(depth, n // tn),
        in_specs=[pl.BlockSpec((b, d), lambda i, j: (0, 0)),
                  pl.BlockSpec((1, d, tn), lambda i, j: (i, 0, j)),
                  pl.BlockSpec((1, 1, tn), lambda i, j: (i, 0, j))],
        out_specs=pl.BlockSpec((1, b, tn), lambda i, j: (i, 0, j)),
        out_shape=jax.ShapeDtypeStruct((depth, b, n), F32),
        compiler_params=_cparams(("parallel", "parallel")),
        name="adaln_modulation",
    )(c, w_mod, b_mod.reshape(depth, 1, n))


def _norm_mod_kernel(x_ref, g_ref, sc_ref, sh_ref, o_ref):
    x = x_ref[0]
    ms = jnp.mean(x * x, axis=-1, keepdims=True)
    y = x * lax.rsqrt(ms + RMS_EPS) * g_ref[...]
    o_ref[0] = (y * (1.0 + sc_ref[0]) + sh_ref[0]).astype(o_ref.dtype)


def _norm_kernel(x_ref, g_ref, o_ref):
    x = x_ref[0]
    ms = jnp.mean(x * x, axis=-1, keepdims=True)
    o_ref[0] = (x * lax.rsqrt(ms + RMS_EPS) * g_ref[...]).astype(o_ref.dtype)


def _norm_mod(x, g, sc, sh, out_dtype=BF16):
    b, s, d = x.shape
    ts = _row_tile(s, 512)
    vec = pl.BlockSpec((1, 1, d), lambda i, j: (i, 0, 0))
    return pl.pallas_call(
        _norm_mod_kernel,
        grid=(b, s // ts),
        in_specs=[pl.BlockSpec((1, ts, d), lambda i, j: (i, j, 0)),
                  pl.BlockSpec((1, d), lambda i, j: (0, 0)), vec, vec],
        out_specs=pl.BlockSpec((1, ts, d), lambda i, j: (i, j, 0)),
        out_shape=jax.ShapeDtypeStruct((b, s, d), out_dtype),
        compiler_params=_cparams(("parallel", "parallel")),
        name="rmsnorm_adaln",
    )(x, g.reshape(1, d), sc.reshape(b, 1, d), sh.reshape(b, 1, d))


def _norm(x, g, out_dtype, col_block=0):
    b, s, _ = x.shape
    d = g.shape[-1]
    ts = _row_tile(s, 512)
    return pl.pallas_call(
        _norm_kernel,
        grid=(b, s // ts),
        in_specs=[pl.BlockSpec((1, ts, d), lambda i, j: (i, j, col_block)),
                  pl.BlockSpec((1, d), lambda i, j: (0, 0))],
        out_specs=pl.BlockSpec((1, ts, d), lambda i, j: (i, j, 0)),
        out_shape=jax.ShapeDtypeStruct((b, s, d), out_dtype),
        compiler_params=_cparams(("parallel", "parallel")),
        name="rmsnorm",
    )(x, g.reshape(1, d))


def _mm_kernel(a_ref, b_ref, o_ref):
    o_ref[...] = jnp.dot(a_ref[...], b_ref[...], preferred_element_type=F32).astype(o_ref.dtype)


def _mm_res_kernel(a_ref, b_ref, x_ref, g_ref, o_ref):
    y = jnp.dot(a_ref[...], b_ref[...], preferred_element_type=F32)
    o_ref[...] = x_ref[...] + g_ref[0] * y


def _matmul(a, w, out_dtype, tm=1024, tn=512):
    m, k = a.shape
    n = w.shape[1]
    tm = _row_tile(m, tm)
    tn = _row_tile(n, tn)
    return pl.pallas_call(
        _mm_kernel,
        grid=(m // tm, n // tn),
        in_specs=[pl.BlockSpec((tm, k), lambda i, j: (i, 0)),
                  pl.BlockSpec((k, tn), lambda i, j: (0, j))],
        out_specs=pl.BlockSpec((tm, tn), lambda i, j: (i, j)),
        out_shape=jax.ShapeDtypeStruct((m, n), out_dtype),
        compiler_params=_cparams(("parallel", "parallel")),
        name="matmul",
    )(a, w)


def _matmul_residual(a, w, x, gate, rows_per_batch, tm=1024, tn=512):
    m, k = a.shape
    n = w.shape[1]
    tm = _row_tile(rows_per_batch, tm)
    tn = _row_tile(n, tn)
    tiles_per_batch = rows_per_batch // tm
    nb = gate.shape[0]
    return pl.pallas_call(
        _mm_res_kernel,
        grid=(m // tm, n // tn),
        in_specs=[pl.BlockSpec((tm, k), lambda i, j: (i, 0)),
                  pl.BlockSpec((k, tn), lambda i, j: (0, j)),
                  pl.BlockSpec((tm, tn), lambda i, j: (i, j)),
                  pl.BlockSpec((1, 1, tn), lambda i, j: (i // tiles_per_batch, 0, j))],
        out_specs=pl.BlockSpec((tm, tn), lambda i, j: (i, j)),
        out_shape=jax.ShapeDtypeStruct((m, n), F32),
        compiler_params=_cparams(("parallel", "parallel")),
        name="matmul_residual",
    )(a, w, x, gate.reshape(nb, 1, n))


def _softmax_step_t(st, vt, carry, acc_ref):
    m_old, l_old = carry
    m_new = jnp.maximum(m_old, jnp.max(st, axis=0, keepdims=True))
    alpha = jnp.exp(m_old - m_new)
    p = jnp.exp(st - m_new)
    l_new = alpha * l_old + jnp.sum(p, axis=0, keepdims=True)
    acc_ref[...] = alpha * acc_ref[...] + jnp.dot(vt, p.astype(vt.dtype), preferred_element_type=F32)
    return m_new, l_new


def _softmax_init(t):
    return jnp.full((1, t), NEG_INF, F32), jnp.zeros((1, t), F32)


def _kq(k, q, scale):
    return lax.dot_general(k, q, (((1,), (1,)), ((), ())), preferred_element_type=F32) * scale


def _causal_tile_mask_t(t):
    r = lax.broadcasted_iota(jnp.int32, (t, t), 0)
    c = lax.broadcasted_iota(jnp.int32, (t, t), 1)
    return r <= c


def _t5_bucket_np(dist):
    n = np.maximum(dist, 0)
    max_exact = NUM_BUCKETS // 2
    nf = np.maximum(n, 1).astype(np.float32)
    large = max_exact + (np.log(nf / np.float32(max_exact)) / np.float32(math.log(MAX_DISTANCE / max_exact))
                         * np.float32(NUM_BUCKETS - max_exact)).astype(np.int32)
    large = np.minimum(large, NUM_BUCKETS - 1)
    return np.where(n < max_exact, n, large).astype(np.int32)


def _bias_tile_kernel(bk_ref, rel_ref, o_ref):
    m = pl.program_id(0)
    bk = bk_ref[...]
    acc = jnp.zeros(bk.shape, F32)
    for b in range(NUM_BUCKETS):
        acc = jnp.where(bk == b, rel_ref[b, m], acc)
    o_ref[0] = acc


def _bias_tiles(rel_bias, t):
    assert t >= MAX_DISTANCE
    n_maps = rel_bias.shape[1]
    kk = np.arange(t)[:, None]
    qq = np.arange(t)[None, :]
    buckets = np.stack([_t5_bucket_np(qq - kk), _t5_bucket_np(t + qq - kk)])
    return pl.pallas_call(
        _bias_tile_kernel,
        grid=(n_maps,),
        in_specs=[pl.BlockSpec((2, t, t), lambda m: (0, 0, 0)),
                  pl.BlockSpec(memory_space=pltpu.SMEM)],
        out_specs=pl.BlockSpec((1, 2, t, t), lambda m: (m, 0, 0, 0)),
        out_shape=jax.ShapeDtypeStruct((n_maps, 2, t, t), F32),
        compiler_params=_cparams(("parallel",)),
        name="t5_bias_tiles",
    )(jnp.asarray(buckets), rel_bias)


def _key_tile_sweep(qi, tile, carry):
    n_far = jnp.maximum(qi - 1, 0)
    carry = lax.fori_loop(0, n_far, lambda kj, c: tile(kj, c, "far"), carry)
    carry = lax.fori_loop(n_far, qi, lambda kj, c: tile(kj, c, "near"), carry)
    return tile(qi, carry, "diag")


def _logsig_cumsum_kernel(f_ref, fb_ref, o_ref, carry_ref):
    @pl.when(pl.program_id(1) == 0)
    def _():
        carry_ref[...] = jnp.zeros_like(carry_ref)

    z = f_ref[0] + fb_ref[...]
    logf = -(jnp.maximum(-z, 0.0) + jnp.log(1.0 + jnp.exp(-jnp.abs(z))))
    t = z.shape[0]
    r = lax.broadcasted_iota(jnp.int32, (t, t), 0)
    c = lax.broadcasted_iota(jnp.int32, (t, t), 1)
    tri = jnp.where(r >= c, 1.0, 0.0).astype(BF16)
    hi = logf.astype(BF16)
    r1 = logf - hi.astype(F32)
    mid = r1.astype(BF16)
    lo = (r1 - mid.astype(F32)).astype(BF16)
    cum = (jnp.dot(tri, hi, preferred_element_type=F32)
           + jnp.dot(tri, mid, preferred_element_type=F32)
           + jnp.dot(tri, lo, preferred_element_type=F32)) + carry_ref[...]
    o_ref[0] = cum
    carry_ref[...] = cum[t - 1:t, :]


def _logsig_cumsum(fg, f_bias):
    b, s, w = fg.shape
    tc = _row_tile(s, 512)
    fb = jnp.pad(f_bias, (0, w - f_bias.shape[0])).reshape(1, w)
    return pl.pallas_call(
        _logsig_cumsum_kernel,
        grid=(b, s // tc),
        in_specs=[pl.BlockSpec((1, tc, w), lambda i, j: (i, j, 0)),
                  pl.BlockSpec((1, w), lambda i, j: (0, 0))],
        out_specs=pl.BlockSpec((1, tc, w), lambda i, j: (i, j, 0)),
        out_shape=jax.ShapeDtypeStruct((b, s, w), F32),
        scratch_shapes=[pltpu.VMEM((1, w), F32)],
        compiler_params=_cparams(("parallel", "arbitrary")),
        name="fox_logsig_cumsum",
    )(fg, fb)


def _fox_kernel(q_ref, k_ref, vt_ref, cq_ref, ck_ref, o_ref, acc_ref, *, t, scale, group):
    qi = pl.program_id(2)
    acc_ref[...] = jnp.zeros_like(acc_ref)
    hd = HEAD_DIM
    q = q_ref[0]

    def tile(kj, carry, diag):
        k0 = pl.multiple_of(kj * t, t)
        k = k_ref[0, pl.ds(k0, t), :]
        sts = [_kq(k[:, g * hd:(g + 1) * hd], q[:, g * hd:(g + 1) * hd], scale) for g in range(group)]
        out = []
        for g in range(group):
            ck = jnp.tile(ck_ref[0, g, pl.ds(k0, t), :], (1, t // LANES))
            st = sts[g] + (cq_ref[0, g] - ck)
            if diag:
                st = jnp.where(_causal_tile_mask_t(t), st, NEG_INF)
            out.append(_softmax_step_t(st, vt_ref[0, g * hd:(g + 1) * hd, pl.ds(k0, t)], carry[g], acc_ref.at[g]))
        return tuple(out)

    carry = lax.fori_loop(0, qi, lambda kj, c: tile(kj, c, False), (_softmax_init(t),) * group)
    carry = tile(qi, carry, True)
    o_ref[0] = jnp.concatenate([(acc_ref[g] / carry[g][1]).T for g in range(group)], axis=1).astype(o_ref.dtype)


def _fox_attention(qkv, cum, n_heads, group=HEADS_PER_STEP):
    b, s, _ = qkv.shape
    t = _row_tile(s, 256)
    hd = HEAD_DIM
    group = min(group, n_heads)
    w = group * hd
    ng = n_heads // group
    cum_t = jnp.transpose(cum[:, :, :n_heads], (0, 2, 1))
    cum_q = cum_t[:, :, None, :]
    cum_k = jnp.broadcast_to(cum_t[..., None], (b, n_heads, s, LANES))
    vt = jnp.transpose(qkv[:, :, 2 * n_heads * hd:], (0, 2, 1))
    return pl.pallas_call(
        functools.partial(_fox_kernel, t=t, scale=hd ** -0.5, group=group),
        grid=(b, ng, s // t),
        in_specs=[pl.BlockSpec((1, t, w), lambda i, h, j: (i, j, h)),
                  pl.BlockSpec((1, s, w), lambda i, h, j: (i, 0, ng + h)),
                  pl.BlockSpec((1, w, s), lambda i, h, j: (i, h, 0)),
                  pl.BlockSpec((1, group, 1, t), lambda i, h, j: (i, h, 0, j)),
                  pl.BlockSpec((1, group, s, LANES), lambda i, h, j: (i, h, 0, 0))],
        out_specs=pl.BlockSpec((1, t, w), lambda i, h, j: (i, j, h)),
        out_shape=jax.ShapeDtypeStruct((b, s, n_heads * hd), BF16),
        scratch_shapes=[pltpu.VMEM((group, hd, t), F32)],
        compiler_params=_cparams(("parallel", "parallel", "arbitrary")),
        name="fox_attention",
    )(qkv, qkv, vt, cum_q, cum_k)


def _diff_kernel(q_ref, k_ref, vt_ref, bias_ref, rel_ref, lam_ref, gain_ref, o_ref, acc_ref,
                 *, t, scale, lam_init, group):
    hg = pl.program_id(1)
    qi = pl.program_id(2)
    acc_ref[...] = jnp.zeros_like(acc_ref)
    q = q_ref[0]
    hd = HEAD_DIM
    n_chain = 2 * group

    def tile(kj, carry, kind):
        k0 = pl.multiple_of(kj * t, t)
        k = k_ref[0, pl.ds(k0, t), :]
        sts = [_kq(k[:, c * hd:(c + 1) * hd], q[:, c * hd:(c + 1) * hd], scale) for c in range(n_chain)]
        out = []
        for c in range(n_chain):
            if kind == "far":
                st = sts[c] + rel_ref[NUM_BUCKETS - 1, n_chain * hg + c]
            elif kind == "near":
                st = sts[c] + bias_ref[c, 1]
            else:
                st = jnp.where(_causal_tile_mask_t(t), sts[c] + bias_ref[c, 0], NEG_INF)
            vt = vt_ref[0, (c // 2) * 2 * hd:(c // 2 + 1) * 2 * hd, pl.ds(k0, t)]
            out.append(_softmax_step_t(st, vt, carry[c], acc_ref.at[c]))
        return tuple(out)

    carry = _key_tile_sweep(qi, tile, (_softmax_init(t),) * n_chain)

    lp = lam_ref[...]
    lam = (jnp.exp(jnp.sum(lp[0:1] * lp[1:2], axis=-1, keepdims=True))
           - jnp.exp(jnp.sum(lp[2:3] * lp[3:4], axis=-1, keepdims=True)) + lam_init)
    outs = []
    for g in range(group):
        o = (acc_ref[2 * g] / carry[2 * g][1] - lam * (acc_ref[2 * g + 1] / carry[2 * g + 1][1])).T
        ms = jnp.mean(o * o, axis=-1, keepdims=True)
        outs.append(o * lax.rsqrt(ms + RMS_EPS) * gain_ref[...] * (1.0 - lam_init))
    o_ref[0] = jnp.concatenate(outs, axis=1).astype(o_ref.dtype)


def _diff_attention(qkv, bias_tiles, rel_bias, lam_p, sub_gain, n_heads, lam_init, group=HEADS_PER_STEP // 2):
    b, s, _ = qkv.shape
    t = bias_tiles.shape[-1]
    group = min(group, n_heads)
    hw = 2 * HEAD_DIM
    w = group * hw
    ng = n_heads // group
    vt = jnp.transpose(qkv[:, :, 2 * n_heads * hw:], (0, 2, 1))
    return pl.pallas_call(
        functools.partial(_diff_kernel, t=t, scale=HEAD_DIM ** -0.5, lam_init=lam_init, group=group),
        grid=(b, ng, s // t),
        in_specs=[pl.BlockSpec((1, t, w), lambda i, h, j: (i, j, h)),
                  pl.BlockSpec((1, s, w), lambda i, h, j: (i, 0, ng + h)),
                  pl.BlockSpec((1, w, s), lambda i, h, j: (i, h, 0)),
                  pl.BlockSpec((2 * group, 2, t, t), lambda i, h, j: (h, 0, 0, 0)),
                  pl.BlockSpec(memory_space=pltpu.SMEM),
                  pl.BlockSpec((4, HEAD_DIM), lambda i, h, j: (0, 0)),
                  pl.BlockSpec((1, hw), lambda i, h, j: (0, 0))],
        out_specs=pl.BlockSpec((1, t, w), lambda i, h, j: (i, j, h)),
        out_shape=jax.ShapeDtypeStruct((b, s, n_heads * hw), BF16),
        scratch_shapes=[pltpu.VMEM((2 * group, hw, t), F32)],
        compiler_params=_cparams(("parallel", "parallel", "arbitrary")),
        name="diff_attention",
    )(qkv, qkv, vt, bias_tiles, rel_bias, lam_p, sub_gain.reshape(1, hw))


def _float_key(x):
    bits = pltpu.bitcast(x, jnp.int32)
    return jnp.where(bits < 0, bits ^ jnp.int32(0x7FFFFFFF), bits)


def _indexer_kernel(q_ref, kx_ref, wt_ref, mask_ref, key_ref, *, t, s, n_heads, n_sel, kc):
    qi = pl.program_id(1)
    qpos = qi * t + lax.broadcasted_iota(jnp.int32, (1, t), 1)
    n_chunks = ((qi + 1) * t + kc - 1) // kc
    wt = wt_ref[0]
    qh = [q_ref[0][:, h * IDX_DIM:(h + 1) * IDX_DIM].astype(BF16) for h in range(n_heads)]

    def kpos_of(c):
        return c * kc + lax.broadcasted_iota(jnp.int32, (kc, 1), 0)

    def rows(c):
        return pl.ds(pl.multiple_of(c * kc, kc), kc)

    def score_chunk(c, carry):
        kk = kx_ref[0, rows(c), :][:, :IDX_DIM].astype(BF16)
        score = jnp.zeros((kc, t), F32)
        for h in range(n_heads):
            d = lax.dot_general(kk, qh[h], (((1,), (1,)), ((), ())), preferred_element_type=F32)
            score = score + wt[h:h + 1, :] * jnp.maximum(d, 0.0)
        score = jnp.where(kpos_of(c) <= qpos, score + 0.0, NEG_INF)
        key_ref[rows(c), :] = _float_key(score)
        return carry

    lax.fori_loop(0, n_chunks, score_chunk, 0)

    def count(pred):
        def body(c, acc):
            hit = jnp.where(pred(key_ref[rows(c), :], kpos_of(c)), 1.0, 0.0)
            return acc + jnp.sum(hit.reshape(kc // SUBLANES, SUBLANES, t), axis=0)
        acc = lax.fori_loop(0, n_chunks, body, jnp.zeros((SUBLANES, t), F32))
        return jnp.sum(acc, axis=0, keepdims=True)

    def value_bit(it, tau_u):
        cand_u = tau_u | jnp.left_shift(jnp.int32(1), 31 - it)
        cand = cand_u ^ jnp.int32(INT_MIN)
        return jnp.where(count(lambda key, kpos: key >= cand) >= n_sel, cand_u, tau_u)

    tau = lax.fori_loop(0, 32, value_bit, jnp.zeros((1, t), jnp.int32)) ^ jnp.int32(INT_MIN)

    need = n_sel - count(lambda key, kpos: key > tau)
    n_bits = int(math.log2(s))

    def index_bit(it, bound):
        cand = bound | jnp.left_shift(jnp.int32(1), n_bits - 1 - it)
        below = count(lambda key, kpos: (key == tau) & (kpos < cand))
        return jnp.where(below < need, cand, bound)

    tied = jnp.max(count(lambda key, kpos: key >= tau)) > n_sel
    bound = lax.cond(tied,
                     lambda: lax.fori_loop(0, n_bits, index_bit, jnp.zeros((1, t), jnp.int32)),
                     lambda: jnp.full((1, t), s - 1, jnp.int32))

    def write_chunk(c, carry):
        key = key_ref[rows(c), :]
        kpos = kpos_of(c)
        sel = ((key > tau) | ((key == tau) & (kpos <= bound))) & (kpos <= qpos)
        mask_ref[0, rows(c), :] = jnp.where(sel, 1, 0).astype(jnp.int8)
        return carry

    def zero_chunk(c, carry):
        mask_ref[0, rows(c), :] = jnp.zeros((kc, t), jnp.int8)
        return carry

    lax.fori_loop(0, n_chunks, write_chunk, 0)
    lax.fori_loop(n_chunks, s // kc, zero_chunk, 0)


def _indexer_mask(rest, w_t, t, n_heads, n_sel, qidx_block, kx_block):
    b, s, _ = rest.shape
    assert s & (s - 1) == 0
    kc = min(s, 512)
    assert kc >= n_sel and kc % t == 0
    wq = n_heads * IDX_DIM
    return pl.pallas_call(
        functools.partial(_indexer_kernel, t=t, s=s, n_heads=n_heads, n_sel=n_sel, kc=kc),
        grid=(b, s // t),
        in_specs=[pl.BlockSpec((1, t, wq), lambda i, j: (i, j, qidx_block)),
                  pl.BlockSpec((1, s, LANES), lambda i, j: (i, 0, kx_block)),
                  pl.BlockSpec((1, n_heads, t), lambda i, j: (i, 0, j))],
        out_specs=pl.BlockSpec((1, s, t), lambda i, j: (i, 0, j)),
        out_shape=jax.ShapeDtypeStruct((b, s, s), jnp.int8),
        scratch_shapes=[pltpu.VMEM((s, t), jnp.int32)],
        compiler_params=_cparams(("parallel", "parallel")),
        name="dsa_indexer_topk_mask",
    )(rest, rest, w_t)


def _dsa_kernel(q_ref, k_ref, vt_ref, bias_ref, mask_ref, rel_ref, o_ref, acc_ref, *, t, scale, group):
    hg = pl.program_id(1)
    qi = pl.program_id(2)
    acc_ref[...] = jnp.zeros_like(acc_ref)
    q = q_ref[0]
    hd = HEAD_DIM

    def tile(kj, carry, kind):
        k0 = pl.multiple_of(kj * t, t)
        k = k_ref[0, pl.ds(k0, t), :]
        sts = [_kq(k[:, g * hd:(g + 1) * hd], q[:, g * hd:(g + 1) * hd], scale) for g in range(group)]
        keep = mask_ref[0, pl.ds(k0, t), :].astype(F32) > 0.5
        out = []
        for g in range(group):
            if kind == "far":
                st = sts[g] + rel_ref[NUM_BUCKETS - 1, group * hg + g]
            elif kind == "near":
                st = sts[g] + bias_ref[g, 1]
            else:
                st = sts[g] + bias_ref[g, 0]
            out.append(_softmax_step_t(jnp.where(keep, st, NEG_INF), vt_ref[0, g * hd:(g + 1) * hd, pl.ds(k0, t)],
                                       carry[g], acc_ref.at[g]))
        return tuple(out)

    carry = _key_tile_sweep(qi, tile, (_softmax_init(t),) * group)
    o_ref[0] = jnp.concatenate([(acc_ref[g] / carry[g][1]).T for g in range(group)], axis=1).astype(o_ref.dtype)


def _dsa_attention(q, kv, mask_t, bias_tiles, rel_bias, n_heads, group=HEADS_PER_STEP):
    b, s, _ = q.shape
    t = bias_tiles.shape[-1]
    hd = HEAD_DIM
    group = min(group, n_heads)
    w = group * hd
    ng = n_heads // group
    vt = jnp.transpose(kv[:, :, n_heads * hd:], (0, 2, 1))
    return pl.pallas_call(
        functools.partial(_dsa_kernel, t=t, scale=hd ** -0.5, group=group),
        grid=(b, ng, s // t),
        in_specs=[pl.BlockSpec((1, t, w), lambda i, h, j: (i, j, h)),
                  pl.BlockSpec((1, s, w), lambda i, h, j: (i, 0, h)),
                  pl.BlockSpec((1, w, s), lambda i, h, j: (i, h, 0)),
                  pl.BlockSpec((group, 2, t, t), lambda i, h, j: (h, 0, 0, 0)),
                  pl.BlockSpec((1, s, t), lambda i, h, j: (i, 0, j)),
                  pl.BlockSpec(memory_space=pltpu.SMEM)],
        out_specs=pl.BlockSpec((1, t, w), lambda i, h, j: (i, j, h)),
        out_shape=jax.ShapeDtypeStruct((b, s, n_heads * hd), BF16),
        scratch_shapes=[pltpu.VMEM((group, hd, t), F32)],
        compiler_params=_cparams(("parallel", "parallel", "arbitrary")),
        name="dsa_attention",
    )(q, kv, vt, bias_tiles, mask_t, rel_bias)


def _router_kernel(h_ref, w_ref, b_ref, idx_ref, wt_ref):
    logits = jnp.dot(h_ref[...].astype(BF16), w_ref[...], preferred_element_type=F32) + b_ref[...]
    lane = lax.broadcasted_iota(jnp.int32, logits.shape, 1).astype(F32)
    top_v, top_i = [], []
    for _ in range(TOP_K):
        mx = jnp.max(logits, axis=-1, keepdims=True)
        first = jnp.min(jnp.where(logits == mx, lane, float(LANES)), axis=-1, keepdims=True)
        top_v.append(mx)
        top_i.append(first)
        logits = jnp.where(lane == first, -jnp.inf, logits)
    e = [jnp.exp(v - top_v[0]) for v in top_v]
    denom = e[0] + e[1] + e[2] + e[3]
    idx = jnp.zeros(logits.shape, F32)
    wts = jnp.zeros(logits.shape, F32)
    for k in range(TOP_K):
        idx = jnp.where(lane == float(k), top_i[k], idx)
        wts = jnp.where(lane == float(k), e[k] / denom, wts)
    idx_ref[...] = idx.astype(jnp.int32)
    wt_ref[...] = wts


def _router(h2, w_router, b_router):
    n, d = h2.shape
    ne = w_router.shape[1]
    tm = _row_tile(n, 1024)
    w = _pad_cols(w_router).astype(BF16)
    bias = jnp.concatenate([b_router, jnp.full((LANES - ne,), NEG_INF, F32)]).reshape(1, LANES)
    out = pl.BlockSpec((tm, LANES), lambda i: (i, 0))
    return pl.pallas_call(
        _router_kernel,
        grid=(n // tm,),
        in_specs=[pl.BlockSpec((tm, d), lambda i: (i, 0)),
                  pl.BlockSpec((d, LANES), lambda i: (0, 0)),
                  pl.BlockSpec((1, LANES), lambda i: (0, 0))],
        out_specs=[out, out],
        out_shape=[jax.ShapeDtypeStruct((n, LANES), jnp.int32), jax.ShapeDtypeStruct((n, LANES), F32)],
        compiler_params=_cparams(("parallel",)),
        name="moe_router_top4",
    )(h2, w, bias)


def _split_even_odd_kernel(w_ref, p_ref, o_ref):
    o_ref[0] = jnp.dot(w_ref[0].astype(BF16), p_ref[...], preferred_element_type=F32).astype(o_ref.dtype)


def _split_even_odd_cols(w):
    e, d, n = w.shape
    f = n // 2
    perm = np.zeros((n, n), np.float32)
    perm[2 * np.arange(f), np.arange(f)] = 1.0
    perm[2 * np.arange(f) + 1, f + np.arange(f)] = 1.0
    td = _row_tile(d, 512)
    return pl.pallas_call(
        _split_even_odd_kernel,
        grid=(e, d // td),
        in_specs=[pl.BlockSpec((1, td, n), lambda i, j: (i, j, 0)),
                  pl.BlockSpec((n, n), lambda i, j: (0, 0))],
        out_specs=pl.BlockSpec((1, td, n), lambda i, j: (i, j, 0)),
        out_shape=jax.ShapeDtypeStruct((e, d, n), BF16),
        compiler_params=_cparams(("parallel", "parallel")),
        name="gate_up_split_columns",
    )(w, jnp.asarray(perm, BF16))


def _expert_ffn(h, wgu, bgu, wd, bd):
    f = wgu.shape[1] // 2
    gu = jnp.dot(h, wgu, preferred_element_type=F32) + bgu
    g = jnp.minimum(gu[:, :f], SWIGLU_LIMIT)
    u = jnp.clip(gu[:, f:], -SWIGLU_LIMIT, SWIGLU_LIMIT)
    act = (u + 1.0) * (g * _sigmoid(g * SWIGLU_ALPHA))
    return jnp.dot(act.astype(BF16), wd, preferred_element_type=F32) + bd


def _row_gather_start(src_hbm, rows_ref, dst, sem, n_rows):
    for r in range(n_rows):
        pltpu.make_async_copy(src_hbm.at[pl.ds(rows_ref[0, 0, r], 1)], dst.at[pl.ds(r, 1)], sem).start()


def _row_gather_wait(src_hbm, dst, sem):
    pltpu.make_async_copy(src_hbm.at[pl.ds(0, dst.shape[0])], dst, sem).wait()


def _moe_expert_kernel(te_ref, tok_ref, tok_next_ref, wrow_ref, wgu_ref, bgu_ref, wd_ref, bd_ref, h_hbm,
                       o_ref, xbuf, sem, *, tm):
    i = pl.program_id(0)
    slot = i % 2

    @pl.when(i == 0)
    def _():
        _row_gather_start(h_hbm, tok_ref, xbuf.at[0], sem.at[0], tm)

    _row_gather_start(h_hbm, tok_next_ref, xbuf.at[1 - slot], sem.at[1 - slot], tm)
    _row_gather_wait(h_hbm, xbuf.at[slot], sem.at[slot])
    y = _expert_ffn(xbuf[slot].astype(BF16), wgu_ref[0], bgu_ref[0], wd_ref[0], bd_ref[0])
    o_ref[...] = y * jnp.tile(wrow_ref[...], (1, y.shape[1] // LANES))

    @pl.when(i == pl.num_programs(0) - 1)
    def _():
        _row_gather_wait(h_hbm, xbuf.at[1 - slot], sem.at[1 - slot])


def _moe_experts_sparse(h2, tile_expert, tok_tiles, w_rows, wgu, bgu, wd, bd, tm):
    n, d = h2.shape
    ne, _, f2 = wgu.shape
    n_tiles = tile_expert.shape[0]
    exp3 = lambda i, te: (te[i], 0, 0)
    smem_rows = lambda off: pl.BlockSpec((1, 1, tm), lambda i, te: (i + off, 0, 0), memory_space=pltpu.SMEM)
    grid_spec = pltpu.PrefetchScalarGridSpec(
        num_scalar_prefetch=1,
        grid=(n_tiles,),
        in_specs=[smem_rows(0), smem_rows(1),
                  pl.BlockSpec((tm, LANES), lambda i, te: (i, 0)),
                  pl.BlockSpec((1, d, f2), exp3), pl.BlockSpec((1, 1, f2), exp3),
                  pl.BlockSpec((1, f2 // 2, d), exp3), pl.BlockSpec((1, 1, d), exp3),
                  pl.BlockSpec(memory_space=pl.ANY)],
        out_specs=pl.BlockSpec((tm, d), lambda i, te: (i, 0)),
        scratch_shapes=[pltpu.VMEM((2, tm, d), F32), pltpu.SemaphoreType.DMA((2,))],
    )
    return pl.pallas_call(
        functools.partial(_moe_expert_kernel, tm=tm),
        grid_spec=grid_spec,
        out_shape=jax.ShapeDtypeStruct((n_tiles * tm, d), F32),
        compiler_params=_cparams(("arbitrary",)),
        name="moe_experts_sparse",
    )(tile_expert, tok_tiles, tok_tiles, w_rows, wgu, bgu.reshape(ne, 1, f2), wd, bd.reshape(ne, 1, d), h2)


def _moe_combine_kernel(pos_ref, pos_next_ref, x_ref, g2_ref, y_hbm, o_ref, ybuf, sem, *, tm):
    i = pl.program_id(0)
    slot = i % 2
    n_rows = TOP_K * tm

    @pl.when(i == 0)
    def _():
        _row_gather_start(y_hbm, pos_ref, ybuf.at[0], sem.at[0], n_rows)

    _row_gather_start(y_hbm, pos_next_ref, ybuf.at[1 - slot], sem.at[1 - slot], n_rows)
    _row_gather_wait(y_hbm, ybuf.at[slot], sem.at[slot])
    acc = ybuf[slot, 0:tm, :]
    for k in range(1, TOP_K):
        acc = acc + ybuf[slot, k * tm:(k + 1) * tm, :]
    o_ref[...] = x_ref[...] + g2_ref[0] * acc

    @pl.when(i == pl.num_programs(0) - 1)
    def _():
        _row_gather_wait(y_hbm, ybuf.at[1 - slot], sem.at[1 - slot])


def _moe_combine(y_rows, pos_tiles, x2, g2, rows_per_batch, tm):
    n, d = x2.shape
    nb = g2.shape[0]
    tiles_per_batch = rows_per_batch // tm
    smem_rows = lambda off: pl.BlockSpec((1, 1, TOP_K * tm), lambda i: (i + off, 0, 0), memory_space=pltpu.SMEM)
    return pl.pallas_call(
        functools.partial(_moe_combine_kernel, tm=tm),
        grid=(n // tm,),
        in_specs=[smem_rows(0), smem_rows(1),
                  pl.BlockSpec((tm, d), lambda i: (i, 0)),
                  pl.BlockSpec((1, 1, d), lambda i: (i // tiles_per_batch, 0, 0)),
                  pl.BlockSpec(memory_space=pl.ANY)],
        out_specs=pl.BlockSpec((tm, d), lambda i: (i, 0)),
        out_shape=jax.ShapeDtypeStruct((n, d), F32),
        scratch_shapes=[pltpu.VMEM((2, TOP_K * tm, d), F32), pltpu.SemaphoreType.DMA((2,))],
        compiler_params=_cparams(("arbitrary",)),
        name="moe_combine",
    )(pos_tiles, pos_tiles, x2, g2.reshape(nb, 1, d), y_rows)


def _moe_dispatch_plan(idx4, w4, ne, tm):
    n = idx4.shape[0]
    n_pairs = n * TOP_K
    n_tiles = n_pairs // tm + ne
    eid = idx4.reshape(-1)
    pair_ids = jnp.arange(n_pairs, dtype=jnp.int32)
    order = jnp.sort(eid * n_pairs + pair_ids) % n_pairs
    counts = jnp.sum((eid[:, None] == jnp.arange(ne, dtype=jnp.int32)[None, :]).astype(jnp.int32), axis=0)
    tiles_e = (counts + tm - 1) // tm
    tile_end = jnp.cumsum(tiles_e)
    tile_off = tile_end - tiles_e
    cnt_off = jnp.cumsum(counts) - counts
    t_idx = jnp.arange(n_tiles, dtype=jnp.int32)
    tile_e = jnp.minimum(jnp.sum((t_idx[:, None] >= tile_end[None, :]).astype(jnp.int32), axis=1), ne - 1)
    first = (t_idx - tile_off[tile_e]) * tm
    n_valid = jnp.where(t_idx < tile_end[-1], jnp.clip(counts[tile_e] - first, 0, tm), 0)
    r = jnp.arange(n_tiles * tm, dtype=jnp.int32)
    tr, lr = r // tm, r % tm
    valid = lr < n_valid[tr]
    pair = order[jnp.clip(cnt_off[tile_e[tr]] + first[tr] + lr, 0, n_pairs - 1)]
    tok_rows = jnp.where(valid, pair // TOP_K, 0)
    w_rows = jnp.where(valid, w4.reshape(-1)[pair], 0.0)
    rank = jnp.zeros((n_pairs,), jnp.int32).at[order].set(pair_ids) - cnt_off[eid]
    pos = (tile_off[eid] * tm + rank).reshape(n, TOP_K)
    return tile_e, tok_rows, w_rows, pos


def _dsa_layer(h, x, g1, w_in, kv_gain, w_uk, w_uv, w_out, rel_bias, bias_tiles):
    b, s, d = h.shape
    n_heads = d // HEAD_DIM
    rank = kv_gain.shape[0]
    idx_heads = n_heads // 2
    n_sel = min(TOPK_MAX, s // 4)
    o1 = n_heads * HEAD_DIM
    h2 = h.reshape(b * s, d)
    q = _matmul(h2, w_in[:, :o1].astype(BF16), BF16).reshape(b, s, o1)
    w_rest = _pad_cols(w_in[:, o1:]).astype(BF16)
    rest = _matmul(h2, w_rest, F32, tn=w_rest.shape[1]).reshape(b, s, -1)
    assert rank % LANES == 0 and (idx_heads * IDX_DIM) == rank
    ckv = _norm(rest, kv_gain, BF16)
    w_k = jnp.transpose(w_uk, (2, 0, 1)).reshape(rank, o1)
    w_v = jnp.transpose(w_uv, (1, 0, 2)).reshape(rank, o1)
    w_kv = jnp.concatenate([w_k, w_v], axis=1).astype(BF16)
    kv = _matmul(ckv.reshape(b * s, rank), w_kv, BF16).reshape(b, s, 2 * o1)
    t = bias_tiles.shape[-1]
    w_off = 2 * rank + IDX_DIM
    w_t = jnp.transpose(rest[:, :, w_off:w_off + idx_heads], (0, 2, 1))
    mask_t = _indexer_mask(rest, w_t, t, idx_heads, n_sel, qidx_block=1, kx_block=2 * rank // LANES)
    o = _dsa_attention(q, kv, mask_t, bias_tiles, rel_bias, n_heads)
    return _matmul_residual(o.reshape(b * s, o1), w_out.astype(BF16), x.reshape(b * s, d), g1, s)


def _fox_layer(h, x, g1, w_in, f_bias, w_out):
    b, s, d = h.shape
    n_heads = f_bias.shape[0]
    hd3 = 3 * n_heads * HEAD_DIM
    h2 = h.reshape(b * s, d)
    qkv = _matmul(h2, w_in[:, :hd3].astype(BF16), BF16).reshape(b, s, hd3)
    w_f = _pad_cols(w_in[:, hd3:]).astype(BF16)
    fg = _matmul(h2, w_f, F32, tn=w_f.shape[1]).reshape(b, s, -1)
    cum = _logsig_cumsum(fg, f_bias)
    o = _fox_attention(qkv, cum, n_heads)
    return _matmul_residual(o.reshape(b * s, -1), w_out.astype(BF16), x.reshape(b * s, d), g1, s)


def _diff_layer(h, x, g1, w_in, lam_p, sub_gain, w_out, rel_bias, bias_tiles, lam_init):
    b, s, d = h.shape
    n_heads = w_in.shape[1] // (6 * HEAD_DIM)
    h2 = h.reshape(b * s, d)
    qkv = _matmul(h2, w_in.astype(BF16), BF16).reshape(b, s, -1)
    o = _diff_attention(qkv, bias_tiles, rel_bias, lam_p, sub_gain, n_heads, lam_init)
    return _matmul_residual(o.reshape(b * s, -1), w_out.astype(BF16), x.reshape(b * s, d), g1, s)


def _moe_layer(h, x2, g2, w_router, b_router, wgu, b_gu, w_dn, b_dn):
    b, s, d = h.shape
    n = b * s
    ne = wgu.shape[0]
    h2 = h.reshape(n, d)
    idx, wts = _router(h2, w_router, b_router)
    tm = min(MOE_ROW_TILE, n * TOP_K)
    tmc = _row_tile(s, MOE_TOKEN_TILE)
    tile_e, tok_rows, w_rows, pos = _moe_dispatch_plan(idx[:, :TOP_K], wts[:, :TOP_K], ne, tm)
    tok_tiles = jnp.pad(tok_rows.reshape(-1, 1, tm), ((0, 1), (0, 0), (0, 0)))
    w_rows = jnp.broadcast_to(w_rows[:, None], (w_rows.shape[0], LANES))
    bgu = jnp.concatenate([b_gu[:, 0::2], b_gu[:, 1::2]], axis=1)
    y_rows = _moe_experts_sparse(h2, tile_e, tok_tiles, w_rows, wgu, bgu, w_dn.astype(BF16), b_dn, tm)
    pos_tiles = jnp.transpose(pos.reshape(n // tmc, tmc, TOP_K), (0, 2, 1)).reshape(n // tmc, TOP_K * tmc)
    pos_tiles = jnp.pad(pos_tiles, ((0, 1), (0, 0)))[:, None, :]
    return _moe_combine(y_rows, pos_tiles, x2, g2, s, tmc)


def kernel(x, c, rel_bias, norm_mix, norm_ffn, w_mod, b_mod, dsa_w_in, dsa_kv_gain, dsa_w_uk, dsa_w_uv, dsa_w_out, fox_w_in, fox_forget_bias, fox_w_out, diff_w_in, diff_lambda, diff_subln_gain, diff_w_out, w_router, b_router, w_gate_up, b_gate_up, w_down, b_down, final_norm):
    b, s, d = x.shape
    depth = w_mod.shape[0]
    mod = _modulation(c, w_mod, b_mod)
    t_attn = _row_tile(s, 256)
    bias_tiles = _bias_tiles(rel_bias, t_attn)
    ne, ff2 = w_gate_up.shape[1], w_gate_up.shape[3]
    wgu_all = _split_even_odd_cols(w_gate_up.reshape(depth * ne, d, ff2)).reshape(depth, ne, d, ff2)
    ia = ib = ic = 0
    for i in range(depth):
        sh1, sc1, g1, sh2, sc2, g2 = (mod[i, :, k * d:(k + 1) * d] for k in range(6))
        h = _norm_mod(x, norm_mix[i], sc1, sh1)
        kind = i % N_MIXERS
        if kind == 0:
            x2 = _dsa_layer(h, x, g1, dsa_w_in[ia], dsa_kv_gain[ia], dsa_w_uk[ia], dsa_w_uv[ia],
                            dsa_w_out[ia], rel_bias, bias_tiles)
            ia += 1
        elif kind == 1:
            x2 = _fox_layer(h, x, g1, fox_w_in[ib], fox_forget_bias[ib], fox_w_out[ib])
            ib += 1
        else:
            lam_init = 0.8 - 0.6 * math.exp(-0.3 * i)
            x2 = _diff_layer(h, x, g1, diff_w_in[ic], diff_lambda[ic], diff_subln_gain[ic],
                             diff_w_out[ic], rel_bias, bias_tiles, lam_init)
            ic += 1
        x = x2.reshape(b, s, d)
        h = _norm_mod(x, norm_ffn[i], sc2, sh2, out_dtype=F32)
        x = _moe_layer(h, x2, g2, w_router[i], b_router[i], wgu_all[i], b_gate_up[i],
                       w_down[i], b_down[i]).reshape(b, s, d)
    return _norm(x, final_norm, x.dtype)
```

```python
import functools
import math

import numpy as np
import jax
import jax.numpy as jnp
from jax import lax
from jax.experimental import pallas as pl
from jax.experimental.pallas import tpu as pltpu

HEAD_DIM = 128
IDX_DIM = 64
TOPK_MAX = 256
TOP_K = 4
SWIGLU_LIMIT = 7.0
SWIGLU_ALPHA = 1.702
NUM_BUCKETS = 32
MAX_DISTANCE = 128
RMS_EPS = 1e-6
NEG_INF = -1e30
N_MIXERS = 3
LANES = 128
SUBLANES = 8
INT_MIN = -(2 ** 31)
LOG2E = 1.4426950408889634
QUERY_SCALE = HEAD_DIM ** -0.5 * LOG2E

F32 = jnp.float32
BF16 = jnp.bfloat16

VMEM_LIMIT = 56 * 1024 * 1024
HEADS_PER_STEP = 4
MOE_ROW_TILE = 256
MOE_TOKEN_TILE = 128


def _cparams(sem):
    return pltpu.CompilerParams(dimension_semantics=sem, vmem_limit_bytes=VMEM_LIMIT)


def _sigmoid(x):
    return 1.0 / (1.0 + jnp.exp(-x))


def _row_tile(s, want):
    t = min(s, want)
    assert s % t == 0
    return t


def _pad_cols(w, mult=LANES):
    n = w.shape[-1]
    pad = (-n) % mult
    if pad:
        w = jnp.pad(w, [(0, 0)] * (w.ndim - 1) + [(0, pad)])
    return w


def _mod_kernel(c_ref, w_ref, b_ref, o_ref):
    c = c_ref[...]
    cs = (c * _sigmoid(c)).astype(BF16)
    o_ref[0] = jnp.dot(cs, w_ref[0].astype(BF16), preferred_element_type=F32) + b_ref[0]


def _modulation(c, w_mod, b_mod):
    depth, d, n = w_mod.shape
    b = c.shape[0]
    tn = min(n, 1024)
    return pl.pallas_call(
        _mod_kernel,
        grid=(depth, n // tn),
        in_specs=[pl.BlockSpec((b, d), lambda i, j: (0, 0)),
                  pl.BlockSpec((1, d, tn), lambda i, j: (i, 0, j)),
                  pl.BlockSpec((1, 1, tn), lambda i, j: (i, 0, j))],
        out_specs=pl.BlockSpec((1, b, tn), lambda i, j: (i, 0, j)),
        out_shape=jax.ShapeDtypeStruct((depth, b, n), F32),
        compiler_params=_cparams(("parallel", "parallel")),
        name="adaln_modulation",
    )(c, w_mod, b_mod.reshape(depth, 1, n))


def _norm_mod_kernel(x_ref, g_ref, sc_ref, sh_ref, o_ref):
    x = x_ref[0]
    ms = jnp.mean(x * x, axis=-1, keepdims=True)
    y = x * lax.rsqrt(ms + RMS_EPS) * g_ref[...]
    o_ref[0] = (y * (1.0 + sc_ref[0]) + sh_ref[0]).astype(o_ref.dtype)


def _norm_kernel(x_ref, g_ref, o_ref):
    x = x_ref[0]
    ms = jnp.mean(x * x, axis=-1, keepdims=True)
    o_ref[0] = (x * lax.rsqrt(ms + RMS_EPS) * g_ref[...]).astype(o_ref.dtype)


def _norm_mod(x, g, sc, sh, out_dtype=BF16):
    b, s, d = x.shape
    ts = _row_tile(s, 512)
    vec = pl.BlockSpec((1, 1, d), lambda i, j: (i, 0, 0))
    return pl.pallas_call(
        _norm_mod_kernel,
        grid=(b, s // ts),
        in_specs=[pl.BlockSpec((1, ts, d), lambda i, j: (i, j, 0)),
                  pl.BlockSpec((1, d), lambda i, j: (0, 0)), vec, vec],
        out_specs=pl.BlockSpec((1, ts, d), lambda i, j: (i, j, 0)),
        out_shape=jax.ShapeDtypeStruct((b, s, d), out_dtype),
        compiler_params=_cparams(("parallel", "parallel")),
        name="rmsnorm_adaln",
    )(x, g.reshape(1, d), sc.reshape(b, 1, d), sh.reshape(b, 1, d))


def _norm(x, g, out_dtype, col_block=0):
    b, s, _ = x.shape
    d = g.shape[-1]
    ts = _row_tile(s, 512)
    return pl.pallas_call(
        _norm_kernel,
        grid=(b, s // ts),
        in_specs=[pl.BlockSpec((1, ts, d), lambda i, j: (i, j, col_block)),
                  pl.BlockSpec((1, d), lambda i, j: (0, 0))],
        out_specs=pl.BlockSpec((1, ts, d), lambda i, j: (i, j, 0)),
        out_shape=jax.ShapeDtypeStruct((b, s, d), out_dtype),
        compiler_params=_cparams(("parallel", "parallel")),
        name="rmsnorm",
    )(x, g.reshape(1, d))


def _mm_kernel(a_ref, b_ref, o_ref):
    o_ref[...] = jnp.dot(a_ref[...], b_ref[...], preferred_element_type=F32).astype(o_ref.dtype)


def _mm_res_kernel(a_ref, b_ref, x_ref, g_ref, o_ref):
    y = jnp.dot(a_ref[...], b_ref[...], preferred_element_type=F32)
    o_ref[...] = x_ref[...] + g_ref[0] * y


def _mm_scaled_kernel(a_ref, b_ref, s_ref, o_ref):
    y = jnp.dot(a_ref[...], b_ref[...], preferred_element_type=F32)
    o_ref[...] = (y * s_ref[...]).astype(o_ref.dtype)


def _matmul(a, w, out_dtype, tm=1024, tn=512, col_scale=None):
    m, k = a.shape
    n = w.shape[1]
    tm = _row_tile(m, tm)
    tn = _row_tile(n, tn)
    in_specs = [pl.BlockSpec((tm, k), lambda i, j: (i, 0)),
                pl.BlockSpec((k, tn), lambda i, j: (0, j))]
    args = (a, w)
    if col_scale is not None:
        in_specs.append(pl.BlockSpec((1, tn), lambda i, j: (0, j)))
        args += (col_scale.reshape(1, n),)
    return pl.pallas_call(
        _mm_kernel if col_scale is None else _mm_scaled_kernel,
        grid=(m // tm, n // tn),
        in_specs=in_specs,
        out_specs=pl.BlockSpec((tm, tn), lambda i, j: (i, j)),
        out_shape=jax.ShapeDtypeStruct((m, n), out_dtype),
        compiler_params=_cparams(("parallel", "parallel")),
        name="matmul",
    )(*args)


def _query_col_scale(n_query_cols, n_cols):
    return jnp.concatenate([jnp.full((n_query_cols,), QUERY_SCALE, F32), jnp.ones((n_cols - n_query_cols,), F32)])


def _matmul_residual(a, w, x, gate, rows_per_batch, tm=1024, tn=512):
    m, k = a.shape
    n = w.shape[1]
    tm = _row_tile(rows_per_batch, tm)
    tn = _row_tile(n, tn)
    tiles_per_batch = rows_per_batch // tm
    nb = gate.shape[0]
    return pl.pallas_call(
        _mm_res_kernel,
        grid=(m // tm, n // tn),
        in_specs=[pl.BlockSpec((tm, k), lambda i, j: (i, 0)),
                  pl.BlockSpec((k, tn), lambda i, j: (0, j)),
                  pl.BlockSpec((tm, tn), lambda i, j: (i, j)),
                  pl.BlockSpec((1, 1, tn), lambda i, j: (i // tiles_per_batch, 0, j))],
        out_specs=pl.BlockSpec((tm, tn), lambda i, j: (i, j)),
        out_shape=jax.ShapeDtypeStruct((m, n), F32),
        compiler_params=_cparams(("parallel", "parallel")),
        name="matmul_residual",
    )(a, w, x, gate.reshape(nb, 1, n))


def _softmax_step_t(u, row_const, vt, carry, acc_ref):
    m_old, l_old = carry
    m_new = jnp.maximum(m_old, jnp.max(u, axis=0, keepdims=True) + row_const)
    alpha = jnp.exp2(m_old - m_new)
    p = jnp.exp2(u + (row_const - m_new))
    l_new = alpha * l_old + jnp.sum(p, axis=0, keepdims=True)
    acc_ref[...] = alpha * acc_ref[...] + jnp.dot(vt, p.astype(vt.dtype), preferred_element_type=F32)
    return m_new, l_new


def _softmax_init(t):
    return jnp.full((1, t), NEG_INF, F32), jnp.zeros((1, t), F32)


def _kq(k, q):
    return lax.dot_general(k, q, (((1,), (1,)), ((), ())), preferred_element_type=F32)


def _causal_tile_mask_t(t):
    r = lax.broadcasted_iota(jnp.int32, (t, t), 0)
    c = lax.broadcasted_iota(jnp.int32, (t, t), 1)
    return r <= c


def _t5_bucket_np(dist):
    n = np.maximum(dist, 0)
    max_exact = NUM_BUCKETS // 2
    nf = np.maximum(n, 1).astype(np.float32)
    large = max_exact + (np.log(nf / np.float32(max_exact)) / np.float32(math.log(MAX_DISTANCE / max_exact))
                         * np.float32(NUM_BUCKETS - max_exact)).astype(np.int32)
    large = np.minimum(large, NUM_BUCKETS - 1)
    return np.where(n < max_exact, n, large).astype(np.int32)


def _bias_tile_kernel(bk_ref, rel_ref, o_ref):
    m = pl.program_id(0)
    bk = bk_ref[...]
    acc = jnp.zeros(bk.shape, F32)
    for b in range(NUM_BUCKETS):
        acc = jnp.where(bk == b, rel_ref[b, m], acc)
    o_ref[0] = acc * LOG2E


def _bias_tiles(rel_bias, t):
    assert t >= MAX_DISTANCE
    n_maps = rel_bias.shape[1]
    kk = np.arange(t)[:, None]
    qq = np.arange(t)[None, :]
    buckets = np.stack([_t5_bucket_np(qq - kk), _t5_bucket_np(t + qq - kk)])
    return pl.pallas_call(
        _bias_tile_kernel,
        grid=(n_maps,),
        in_specs=[pl.BlockSpec((2, t, t), lambda m: (0, 0, 0)),
                  pl.BlockSpec(memory_space=pltpu.SMEM)],
        out_specs=pl.BlockSpec((1, 2, t, t), lambda m: (m, 0, 0, 0)),
        out_shape=jax.ShapeDtypeStruct((n_maps, 2, t, t), F32),
        compiler_params=_cparams(("parallel",)),
        name="t5_bias_tiles",
    )(jnp.asarray(buckets), rel_bias)


def _key_tile_sweep(qi, tile, carry):
    n_far = jnp.maximum(qi - 1, 0)
    carry = lax.fori_loop(0, n_far, lambda kj, c: tile(kj, c, "far"), carry)
    carry = lax.fori_loop(n_far, qi, lambda kj, c: tile(kj, c, "near"), carry)
    return tile(qi, carry, "diag")


def _logsig_cumsum_kernel(f_ref, fb_ref, o_ref, carry_ref):
    @pl.when(pl.program_id(1) == 0)
    def _():
        carry_ref[...] = jnp.zeros_like(carry_ref)

    z = f_ref[0] + fb_ref[...]
    logf = -(jnp.maximum(-z, 0.0) + jnp.log(1.0 + jnp.exp(-jnp.abs(z))))
    t = z.shape[0]
    r = lax.broadcasted_iota(jnp.int32, (t, t), 0)
    c = lax.broadcasted_iota(jnp.int32, (t, t), 1)
    tri = jnp.where(r >= c, 1.0, 0.0).astype(BF16)
    hi = logf.astype(BF16)
    r1 = logf - hi.astype(F32)
    mid = r1.astype(BF16)
    lo = (r1 - mid.astype(F32)).astype(BF16)
    cum = (jnp.dot(tri, hi, preferred_element_type=F32)
           + jnp.dot(tri, mid, preferred_element_type=F32)
           + jnp.dot(tri, lo, preferred_element_type=F32)) + carry_ref[...]
    o_ref[0] = cum * LOG2E
    carry_ref[...] = cum[t - 1:t, :]


def _logsig_cumsum(fg, f_bias):
    b, s, w = fg.shape
    tc = _row_tile(s, 512)
    fb = jnp.pad(f_bias, (0, w - f_bias.shape[0])).reshape(1, w)
    return pl.pallas_call(
        _logsig_cumsum_kernel,
        grid=(b, s // tc),
        in_specs=[pl.BlockSpec((1, tc, w), lambda i, j: (i, j, 0)),
                  pl.BlockSpec((1, w), lambda i, j: (0, 0))],
        out_specs=pl.BlockSpec((1, tc, w), lambda i, j: (i, j, 0)),
        out_shape=jax.ShapeDtypeStruct((b, s, w), F32),
        scratch_shapes=[pltpu.VMEM((1, w), F32)],
        compiler_params=_cparams(("parallel", "arbitrary")),
        name="fox_logsig_cumsum",
    )(fg, fb)


def _fox_kernel(q_ref, k_ref, vt_ref, cq_ref, ck_ref, o_ref, acc_ref, *, t, group):
    qi = pl.program_id(2)
    acc_ref[...] = jnp.zeros_like(acc_ref)
    hd = HEAD_DIM
    q = q_ref[0]

    def tile(kj, carry, diag):
        k0 = pl.multiple_of(kj * t, t)
        k = k_ref[0, pl.ds(k0, t), :]
        sts = [_kq(k[:, g * hd:(g + 1) * hd], q[:, g * hd:(g + 1) * hd]) for g in range(group)]
        out = []
        for g in range(group):
            ck = jnp.tile(ck_ref[0, g, pl.ds(k0, t), :], (1, t // LANES))
            u = sts[g] - ck
            if diag:
                u = jnp.where(_causal_tile_mask_t(t), u, NEG_INF)
            out.append(_softmax_step_t(u, cq_ref[0, g], vt_ref[0, g * hd:(g + 1) * hd, pl.ds(k0, t)],
                                       carry[g], acc_ref.at[g]))
        return tuple(out)

    carry = lax.fori_loop(0, qi, lambda kj, c: tile(kj, c, False), (_softmax_init(t),) * group)
    carry = tile(qi, carry, True)
    o_ref[0] = jnp.concatenate([(acc_ref[g] / carry[g][1]).T for g in range(group)], axis=1).astype(o_ref.dtype)


def _fox_attention(qkv, cum, n_heads, group=HEADS_PER_STEP):
    b, s, _ = qkv.shape
    t = _row_tile(s, 256)
    hd = HEAD_DIM
    group = min(group, n_heads)
    w = group * hd
    ng = n_heads // group
    cum_t = jnp.transpose(cum[:, :, :n_heads], (0, 2, 1))
    cum_q = cum_t[:, :, None, :]
    cum_k = jnp.broadcast_to(cum_t[..., None], (b, n_heads, s, LANES))
    vt = jnp.transpose(qkv[:, :, 2 * n_heads * hd:], (0, 2, 1))
    return pl.pallas_call(
        functools.partial(_fox_kernel, t=t, group=group),
        grid=(b, ng, s // t),
        in_specs=[pl.BlockSpec((1, t, w), lambda i, h, j: (i, j, h)),
                  pl.BlockSpec((1, s, w), lambda i, h, j: (i, 0, ng + h)),
                  pl.BlockSpec((1, w, s), lambda i, h, j: (i, h, 0)),
                  pl.BlockSpec((1, group, 1, t), lambda i, h, j: (i, h, 0, j)),
                  pl.BlockSpec((1, group, s, LANES), lambda i, h, j: (i, h, 0, 0))],
        out_specs=pl.BlockSpec((1, t, w), lambda i, h, j: (i, j, h)),
        out_shape=jax.ShapeDtypeStruct((b, s, n_heads * hd), BF16),
        scratch_shapes=[pltpu.VMEM((group, hd, t), F32)],
        compiler_params=_cparams(("parallel", "parallel", "arbitrary")),
        name="fox_attention",
    )(qkv, qkv, vt, cum_q, cum_k)


def _diff_kernel(q_ref, k_ref, vt_ref, bias_ref, rel_ref, lam_ref, gain_ref, o_ref, acc_ref,
                 *, t, lam_init, group):
    hg = pl.program_id(1)
    qi = pl.program_id(2)
    acc_ref[...] = jnp.zeros_like(acc_ref)
    q = q_ref[0]
    hd = HEAD_DIM
    n_chain = 2 * group

    def tile(kj, carry, kind):
        k0 = pl.multiple_of(kj * t, t)
        k = k_ref[0, pl.ds(k0, t), :]
        sts = [_kq(k[:, c * hd:(c + 1) * hd], q[:, c * hd:(c + 1) * hd]) for c in range(n_chain)]
        out = []
        for c in range(n_chain):
            if kind == "far":
                u, const = sts[c], rel_ref[NUM_BUCKETS - 1, n_chain * hg + c] * LOG2E
            elif kind == "near":
                u, const = sts[c] + bias_ref[c, 1], 0.0
            else:
                u, const = jnp.where(_causal_tile_mask_t(t), sts[c] + bias_ref[c, 0], NEG_INF), 0.0
            vt = vt_ref[0, (c // 2) * 2 * hd:(c // 2 + 1) * 2 * hd, pl.ds(k0, t)]
            out.append(_softmax_step_t(u, const, vt, carry[c], acc_ref.at[c]))
        return tuple(out)

    carry = _key_tile_sweep(qi, tile, (_softmax_init(t),) * n_chain)

    lp = lam_ref[...]
    lam = (jnp.exp(jnp.sum(lp[0:1] * lp[1:2], axis=-1, keepdims=True))
           - jnp.exp(jnp.sum(lp[2:3] * lp[3:4], axis=-1, keepdims=True)) + lam_init)
    outs = []
    for g in range(group):
        o = (acc_ref[2 * g] / carry[2 * g][1] - lam * (acc_ref[2 * g + 1] / carry[2 * g + 1][1])).T
        ms = jnp.mean(o * o, axis=-1, keepdims=True)
        outs.append(o * lax.rsqrt(ms + RMS_EPS) * gain_ref[...] * (1.0 - lam_init))
    o_ref[0] = jnp.concatenate(outs, axis=1).astype(o_ref.dtype)


def _diff_attention(qkv, bias_tiles, rel_bias, lam_p, sub_gain, n_heads, lam_init, group=HEADS_PER_STEP // 2):
    b, s, _ = qkv.shape
    t = bias_tiles.shape[-1]
    group = min(group, n_heads)
    hw = 2 * HEAD_DIM
    w = group * hw
    ng = n_heads // group
    vt = jnp.transpose(qkv[:, :, 2 * n_heads * hw:], (0, 2, 1))
    return pl.pallas_call(
        functools.partial(_diff_kernel, t=t, lam_init=lam_init, group=group),
        grid=(b, ng, s // t),
        in_specs=[pl.BlockSpec((1, t, w), lambda i, h, j: (i, j, h)),
                  pl.BlockSpec((1, s, w), lambda i, h, j: (i, 0, ng + h)),
                  pl.BlockSpec((1, w, s), lambda i, h, j: (i, h, 0)),
                  pl.BlockSpec((2 * group, 2, t, t), lambda i, h, j: (h, 0, 0, 0)),
                  pl.BlockSpec(memory_space=pltpu.SMEM),
                  pl.BlockSpec((4, HEAD_DIM), lambda i, h, j: (0, 0)),
                  pl.BlockSpec((1, hw), lambda i, h, j: (0, 0))],
        out_specs=pl.BlockSpec((1, t, w), lambda i, h, j: (i, j, h)),
        out_shape=jax.ShapeDtypeStruct((b, s, n_heads * hw), BF16),
        scratch_shapes=[pltpu.VMEM((2 * group, hw, t), F32)],
        compiler_params=_cparams(("parallel", "parallel", "arbitrary")),
        name="diff_attention",
    )(qkv, qkv, vt, bias_tiles, rel_bias, lam_p, sub_gain.reshape(1, hw))


def _float_key(x):
    bits = pltpu.bitcast(x, jnp.int32)
    return jnp.where(bits < 0, bits ^ jnp.int32(0x7FFFFFFF), bits)


def _indexer_kernel(q_ref, kx_ref, wt_ref, mask_ref, key_ref, *, t, s, n_heads, n_sel, kc):
    qi = pl.program_id(1)
    qpos = qi * t + lax.broadcasted_iota(jnp.int32, (1, t), 1)
    n_chunks = ((qi + 1) * t + kc - 1) // kc
    wt = wt_ref[0]
    qh = [q_ref[0][:, h * IDX_DIM:(h + 1) * IDX_DIM].astype(BF16) for h in range(n_heads)]

    def kpos_of(c):
        return c * kc + lax.broadcasted_iota(jnp.int32, (kc, 1), 0)

    def rows(c):
        return pl.ds(pl.multiple_of(c * kc, kc), kc)

    def score_chunk(c, carry):
        kk = kx_ref[0, rows(c), :][:, :IDX_DIM].astype(BF16)
        score = jnp.zeros((kc, t), F32)
        for h in range(n_heads):
            d = lax.dot_general(kk, qh[h], (((1,), (1,)), ((), ())), preferred_element_type=F32)
            score = score + wt[h:h + 1, :] * jnp.maximum(d, 0.0)
        score = jnp.where(kpos_of(c) <= qpos, score + 0.0, NEG_INF)
        key_ref[rows(c), :] = _float_key(score)
        return carry

    lax.fori_loop(0, n_chunks, score_chunk, 0)

    def count(pred):
        def body(c, acc):
            hit = jnp.where(pred(key_ref[rows(c), :], kpos_of(c)), 1.0, 0.0)
            return acc + jnp.sum(hit.reshape(kc // SUBLANES, SUBLANES, t), axis=0)
        acc = lax.fori_loop(0, n_chunks, body, jnp.zeros((SUBLANES, t), F32))
        return jnp.sum(acc, axis=0, keepdims=True)

    def value_bit(it, tau_u):
        cand_u = tau_u | jnp.left_shift(jnp.int32(1), 31 - it)
        cand = cand_u ^ jnp.int32(INT_MIN)
        return jnp.where(count(lambda key, kpos: key >= cand) >= n_sel, cand_u, tau_u)

    tau = lax.fori_loop(0, 32, value_bit, jnp.zeros((1, t), jnp.int32)) ^ jnp.int32(INT_MIN)

    need = n_sel - count(lambda key, kpos: key > tau)
    n_bits = int(math.log2(s))

    def index_bit(it, bound):
        cand = bound | jnp.left_shift(jnp.int32(1), n_bits - 1 - it)
        below = count(lambda key, kpos: (key == tau) & (kpos < cand))
        return jnp.where(below < need, cand, bound)

    tied = jnp.max(count(lambda key, kpos: key >= tau)) > n_sel
    bound = lax.cond(tied,
                     lambda: lax.fori_loop(0, n_bits, index_bit, jnp.zeros((1, t), jnp.int32)),
                     lambda: jnp.full((1, t), s - 1, jnp.int32))

    def write_chunk(c, carry):
        key = key_ref[rows(c), :]
        kpos = kpos_of(c)
        sel = ((key > tau) | ((key == tau) & (kpos <= bound))) & (kpos <= qpos)
        mask_ref[0, rows(c), :] = jnp.where(sel, 1, 0).astype(jnp.int8)
        return carry

    def zero_chunk(c, carry):
        mask_ref[0, rows(c), :] = jnp.zeros((kc, t), jnp.int8)
        return carry

    lax.fori_loop(0, n_chunks, write_chunk, 0)
    lax.fori_loop(n_chunks, s // kc, zero_chunk, 0)


def _indexer_mask(rest, w_t, t, n_heads, n_sel, qidx_block, kx_block):
    b, s, _ = rest.shape
    assert s & (s - 1) == 0
    kc = min(s, 512)
    assert kc >= n_sel and kc % t == 0
    wq = n_heads * IDX_DIM
    return pl.pallas_call(
        functools.partial(_indexer_kernel, t=t, s=s, n_heads=n_heads, n_sel=n_sel, kc=kc),
        grid=(b, s // t),
        in_specs=[pl.BlockSpec((1, t, wq), lambda i, j: (i, j, qidx_block)),
                  pl.BlockSpec((1, s, LANES), lambda i, j: (i, 0, kx_block)),
                  pl.BlockSpec((1, n_heads, t), lambda i, j: (i, 0, j))],
        out_specs=pl.BlockSpec((1, s, t), lambda i, j: (i, 0, j)),
        out_shape=jax.ShapeDtypeStruct((b, s, s), jnp.int8),
        scratch_shapes=[pltpu.VMEM((s, t), jnp.int32)],
        compiler_params=_cparams(("parallel", "parallel")),
        name="dsa_indexer_topk_mask",
    )(rest, rest, w_t)


def _dsa_kernel(q_ref, k_ref, vt_ref, bias_ref, mask_ref, rel_ref, o_ref, acc_ref, *, t, group):
    hg = pl.program_id(1)
    qi = pl.program_id(2)
    acc_ref[...] = jnp.zeros_like(acc_ref)
    q = q_ref[0]
    hd = HEAD_DIM

    def tile(kj, carry, kind):
        k0 = pl.multiple_of(kj * t, t)
        k = k_ref[0, pl.ds(k0, t), :]
        sts = [_kq(k[:, g * hd:(g + 1) * hd], q[:, g * hd:(g + 1) * hd]) for g in range(group)]
        keep = mask_ref[0, pl.ds(k0, t), :].astype(F32) > 0.5
        out = []
        for g in range(group):
            if kind == "far":
                u, const = sts[g], rel_ref[NUM_BUCKETS - 1, group * hg + g] * LOG2E
            elif kind == "near":
                u, const = sts[g] + bias_ref[g, 1], 0.0
            else:
                u, const = sts[g] + bias_ref[g, 0], 0.0
            out.append(_softmax_step_t(jnp.where(keep, u, NEG_INF), const,
                                       vt_ref[0, g * hd:(g + 1) * hd, pl.ds(k0, t)], carry[g], acc_ref.at[g]))
        return tuple(out)

    carry = _key_tile_sweep(qi, tile, (_softmax_init(t),) * group)
    o_ref[0] = jnp.concatenate([(acc_ref[g] / carry[g][1]).T for g in range(group)], axis=1).astype(o_ref.dtype)


def _dsa_attention(q, kv, mask_t, bias_tiles, rel_bias, n_heads, group=HEADS_PER_STEP):
    b, s, _ = q.shape
    t = bias_tiles.shape[-1]
    hd = HEAD_DIM
    group = min(group, n_heads)
    w = group * hd
    ng = n_heads // group
    vt = jnp.transpose(kv[:, :, n_heads * hd:], (0, 2, 1))
    return pl.pallas_call(
        functools.partial(_dsa_kernel, t=t, group=group),
        grid=(b, ng, s // t),
        in_specs=[pl.BlockSpec((1, t, w), lambda i, h, j: (i, j, h)),
                  pl.BlockSpec((1, s, w), lambda i, h, j: (i, 0, h)),
                  pl.BlockSpec((1, w, s), lambda i, h, j: (i, h, 0)),
                  pl.BlockSpec((group, 2, t, t), lambda i, h, j: (h, 0, 0, 0)),
                  pl.BlockSpec((1, s, t), lambda i, h, j: (i, 0, j)),
                  pl.BlockSpec(memory_space=pltpu.SMEM)],
        out_specs=pl.BlockSpec((1, t, w), lambda i, h, j: (i, j, h)),
        out_shape=jax.ShapeDtypeStruct((b, s, n_heads * hd), BF16),
        scratch_shapes=[pltpu.VMEM((group, hd, t), F32)],
        compiler_params=_cparams(("parallel", "parallel", "arbitrary")),
        name="dsa_attention",
    )(q, kv, vt, bias_tiles, mask_t, rel_bias)


def _router_kernel(h_ref, w_ref, b_ref, idx_ref, wt_ref):
    logits = jnp.dot(h_ref[...].astype(BF16), w_ref[...], preferred_element_type=F32) + b_ref[...]
    lane = lax.broadcasted_iota(jnp.int32, logits.shape, 1).astype(F32)
    top_v, top_i = [], []
    for _ in range(TOP_K):
        mx = jnp.max(logits, axis=-1, keepdims=True)
        first = jnp.min(jnp.where(logits == mx, lane, float(LANES)), axis=-1, keepdims=True)
        top_v.append(mx)
        top_i.append(first)
        logits = jnp.where(lane == first, -jnp.inf, logits)
    e = [jnp.exp(v - top_v[0]) for v in top_v]
    denom = e[0] + e[1] + e[2] + e[3]
    idx = jnp.zeros(logits.shape, F32)
    wts = jnp.zeros(logits.shape, F32)
    for k in range(TOP_K):
        idx = jnp.where(lane == float(k), top_i[k], idx)
        wts = jnp.where(lane == float(k), e[k] / denom, wts)
    idx_ref[...] = idx.astype(jnp.int32)
    wt_ref[...] = wts


def _router(h2, w_router, b_router):
    n, d = h2.shape
    ne = w_router.shape[1]
    tm = _row_tile(n, 1024)
    w = _pad_cols(w_router).astype(BF16)
    bias = jnp.concatenate([b_router, jnp.full((LANES - ne,), NEG_INF, F32)]).reshape(1, LANES)
    out = pl.BlockSpec((tm, LANES), lambda i: (i, 0))
    return pl.pallas_call(
        _router_kernel,
        grid=(n // tm,),
        in_specs=[pl.BlockSpec((tm, d), lambda i: (i, 0)),
                  pl.BlockSpec((d, LANES), lambda i: (0, 0)),
                  pl.BlockSpec((1, LANES), lambda i: (0, 0))],
        out_specs=[out, out],
        out_shape=[jax.ShapeDtypeStruct((n, LANES), jnp.int32), jax.ShapeDtypeStruct((n, LANES), F32)],
        compiler_params=_cparams(("parallel",)),
        name="moe_router_top4",
    )(h2, w, bias)


def _split_even_odd_kernel(w_ref, p_ref, o_ref):
    o_ref[0] = jnp.dot(w_ref[0].astype(BF16), p_ref[...], preferred_element_type=F32).astype(o_ref.dtype)


def _split_even_odd_cols(w):
    e, d, n = w.shape
    f = n // 2
    perm = np.zeros((n, n), np.float32)
    perm[2 * np.arange(f), np.arange(f)] = 1.0
    perm[2 * np.arange(f) + 1, f + np.arange(f)] = 1.0
    td = _row_tile(d, 512)
    return pl.pallas_call(
        _split_even_odd_kernel,
        grid=(e, d // td),
        in_specs=[pl.BlockSpec((1, td, n), lambda i, j: (i, j, 0)),
                  pl.BlockSpec((n, n), lambda i, j: (0, 0))],
        out_specs=pl.BlockSpec((1, td, n), lambda i, j: (i, j, 0)),
        out_shape=jax.ShapeDtypeStruct((e, d, n), BF16),
        compiler_params=_cparams(("parallel", "parallel")),
        name="gate_up_split_columns",
    )(w, jnp.asarray(perm, BF16))


def _expert_ffn(h, wgu, bgu, wd, bd):
    f = wgu.shape[1] // 2
    gu = jnp.dot(h, wgu, preferred_element_type=F32) + bgu
    g = jnp.minimum(gu[:, :f], SWIGLU_LIMIT)
    u = jnp.clip(gu[:, f:], -SWIGLU_LIMIT, SWIGLU_LIMIT)
    act = (u + 1.0) * (g * _sigmoid(g * SWIGLU_ALPHA))
    return jnp.dot(act.astype(BF16), wd, preferred_element_type=F32) + bd


def _row_gather_start(src_hbm, rows_ref, dst, sem, n_rows):
    for r in range(n_rows):
        copy = pltpu.make_async_copy(src_hbm.at[pl.ds(rows_ref[0, 0, r], 1)], dst.at[pl.ds(r, 1)], sem)
        copy.start(priority=r % 2)


def _row_gather_wait(src_hbm, dst, sem):
    pltpu.make_async_copy(src_hbm.at[pl.ds(0, dst.shape[0])], dst, sem).wait()


def _moe_expert_kernel(te_ref, tok_ref, tok_next_ref, wrow_ref, wgu_ref, bgu_ref, wd_ref, bd_ref, h_hbm,
                       o_ref, xbuf, sem, *, tm):
    i = pl.program_id(0)
    slot = i % 2

    @pl.when(i == 0)
    def _():
        _row_gather_start(h_hbm, tok_ref, xbuf.at[0], sem.at[0], tm)

    _row_gather_start(h_hbm, tok_next_ref, xbuf.at[1 - slot], sem.at[1 - slot], tm)
    _row_gather_wait(h_hbm, xbuf.at[slot], sem.at[slot])
    y = _expert_ffn(xbuf[slot].astype(BF16), wgu_ref[0], bgu_ref[0], wd_ref[0], bd_ref[0])
    o_ref[...] = y * jnp.tile(wrow_ref[...], (1, y.shape[1] // LANES))

    @pl.when(i == pl.num_programs(0) - 1)
    def _():
        _row_gather_wait(h_hbm, xbuf.at[1 - slot], sem.at[1 - slot])


def _moe_experts_sparse(h2, tile_expert, tok_tiles, w_rows, wgu, bgu, wd, bd, tm):
    n, d = h2.shape
    ne, _, f2 = wgu.shape
    n_tiles = tile_expert.shape[0]
    exp3 = lambda i, te: (te[i], 0, 0)
    smem_rows = lambda off: pl.BlockSpec((1, 1, tm), lambda i, te: (i + off, 0, 0), memory_space=pltpu.SMEM)
    grid_spec = pltpu.PrefetchScalarGridSpec(
        num_scalar_prefetch=1,
        grid=(n_tiles,),
        in_specs=[smem_rows(0), smem_rows(1),
                  pl.BlockSpec((tm, LANES), lambda i, te: (i, 0)),
                  pl.BlockSpec((1, d, f2), exp3), pl.BlockSpec((1, 1, f2), exp3),
                  pl.BlockSpec((1, f2 // 2, d), exp3), pl.BlockSpec((1, 1, d), exp3),
                  pl.BlockSpec(memory_space=pl.ANY)],
        out_specs=pl.BlockSpec((tm, d), lambda i, te: (i, 0)),
        scratch_shapes=[pltpu.VMEM((2, tm, d), F32), pltpu.SemaphoreType.DMA((2,))],
    )
    return pl.pallas_call(
        functools.partial(_moe_expert_kernel, tm=tm),
        grid_spec=grid_spec,
        out_shape=jax.ShapeDtypeStruct((n_tiles * tm, d), F32),
        compiler_params=_cparams(("arbitrary",)),
        name="moe_experts_sparse",
    )(tile_expert, tok_tiles, tok_tiles, w_rows, wgu, bgu.reshape(ne, 1, f2), wd, bd.reshape(ne, 1, d), h2)


def _moe_combine_kernel(pos_ref, pos_next_ref, x_ref, g2_ref, y_hbm, o_ref, ybuf, sem, *, tm):
    i = pl.program_id(0)
    slot = i % 2
    n_rows = TOP_K * tm

    @pl.when(i == 0)
    def _():
        _row_gather_start(y_hbm, pos_ref, ybuf.at[0], sem.at[0], n_rows)

    _row_gather_start(y_hbm, pos_next_ref, ybuf.at[1 - slot], sem.at[1 - slot], n_rows)
    _row_gather_wait(y_hbm, ybuf.at[slot], sem.at[slot])
    acc = ybuf[slot, 0:tm, :]
    for k in range(1, TOP_K):
        acc = acc + ybuf[slot, k * tm:(k + 1) * tm, :]
    o_ref[...] = x_ref[...] + g2_ref[0] * acc

    @pl.when(i == pl.num_programs(0) - 1)
    def _():
        _row_gather_wait(y_hbm, ybuf.at[1 - slot], sem.at[1 - slot])


def _moe_combine(y_rows, pos_tiles, x2, g2, rows_per_batch, tm):
    n, d = x2.shape
    nb = g2.shape[0]
    tiles_per_batch = rows_per_batch // tm
    smem_rows = lambda off: pl.BlockSpec((1, 1, TOP_K * tm), lambda i: (i + off, 0, 0), memory_space=pltpu.SMEM)
    return pl.pallas_call(
        functools.partial(_moe_combine_kernel, tm=tm),
        grid=(n // tm,),
        in_specs=[smem_rows(0), smem_rows(1),
                  pl.BlockSpec((tm, d), lambda i: (i, 0)),
                  pl.BlockSpec((1, 1, d), lambda i: (i // tiles_per_batch, 0, 0)),
                  pl.BlockSpec(memory_space=pl.ANY)],
        out_specs=pl.BlockSpec((tm, d), lambda i: (i, 0)),
        out_shape=jax.ShapeDtypeStruct((n, d), F32),
        scratch_shapes=[pltpu.VMEM((2, TOP_K * tm, d), F32), pltpu.SemaphoreType.DMA((2,))],
        compiler_params=_cparams(("arbitrary",)),
        name="moe_combine",
    )(pos_tiles, pos_tiles, x2, g2.reshape(nb, 1, d), y_rows)


def _moe_dispatch_plan(idx4, w4, ne, tm):
    n = idx4.shape[0]
    n_pairs = n * TOP_K
    n_rows = n_pairs + ne * tm
    big = 1 << (n_pairs + tm - 1).bit_length()
    unused = jnp.int32(ne * big)
    eid = idx4.reshape(-1)
    experts = jnp.arange(ne, dtype=jnp.int32)
    counts = jnp.sum((eid[:, None] == experts[None, :]).astype(jnp.int32), axis=0)
    n_pad = (-counts) % tm
    j = jnp.arange(tm, dtype=jnp.int32)
    pad_keys = jnp.where(j[None, :] < n_pad[:, None], experts[:, None] * big + n_pairs + j[None, :], unused)
    keys = jnp.concatenate([eid * big + jnp.arange(n_pairs, dtype=jnp.int32), pad_keys.reshape(-1)])
    wvals = jnp.concatenate([w4.reshape(-1), jnp.zeros((ne * tm,), F32)])
    skeys, w_rows = lax.sort((keys, wvals), num_keys=1)
    slot = skeys % big
    is_pair = (skeys < unused) & (slot < n_pairs)
    tok_rows = jnp.where(is_pair, slot // TOP_K, 0)
    tile_e = jnp.minimum(skeys[::tm] // big, ne - 1)
    _, row_of_pair = lax.sort((jnp.where(is_pair, slot, n_rows), jnp.arange(n_rows, dtype=jnp.int32)), num_keys=1)
    pos = row_of_pair[:n_pairs].reshape(n, TOP_K)
    return tile_e, tok_rows, w_rows, pos


def _dsa_layer(h, x, g1, w_in, kv_gain, w_uk, w_uv, w_out, rel_bias, bias_tiles):
    b, s, d = h.shape
    n_heads = d // HEAD_DIM
    rank = kv_gain.shape[0]
    idx_heads = n_heads // 2
    n_sel = min(TOPK_MAX, s // 4)
    o1 = n_heads * HEAD_DIM
    h2 = h.reshape(b * s, d)
    q = _matmul(h2, w_in[:, :o1].astype(BF16), BF16, col_scale=_query_col_scale(o1, o1)).reshape(b, s, o1)
    w_rest = _pad_cols(w_in[:, o1:]).astype(BF16)
    rest = _matmul(h2, w_rest, F32, tn=w_rest.shape[1]).reshape(b, s, -1)
    assert rank % LANES == 0 and (idx_heads * IDX_DIM) == rank
    ckv = _norm(rest, kv_gain, BF16)
    w_k = jnp.transpose(w_uk, (2, 0, 1)).reshape(rank, o1)
    w_v = jnp.transpose(w_uv, (1, 0, 2)).reshape(rank, o1)
    w_kv = jnp.concatenate([w_k, w_v], axis=1).astype(BF16)
    kv = _matmul(ckv.reshape(b * s, rank), w_kv, BF16).reshape(b, s, 2 * o1)
    t = bias_tiles.shape[-1]
    w_off = 2 * rank + IDX_DIM
    w_t = jnp.transpose(rest[:, :, w_off:w_off + idx_heads], (0, 2, 1))
    mask_t = _indexer_mask(rest, w_t, t, idx_heads, n_sel, qidx_block=1, kx_block=2 * rank // LANES)
    o = _dsa_attention(q, kv, mask_t, bias_tiles, rel_bias, n_heads)
    return _matmul_residual(o.reshape(b * s, o1), w_out.astype(BF16), x.reshape(b * s, d), g1, s)


def _fox_layer(h, x, g1, w_in, f_bias, w_out):
    b, s, d = h.shape
    n_heads = f_bias.shape[0]
    hd3 = 3 * n_heads * HEAD_DIM
    h2 = h.reshape(b * s, d)
    qkv = _matmul(h2, w_in[:, :hd3].astype(BF16), BF16,
                  col_scale=_query_col_scale(hd3 // 3, hd3)).reshape(b, s, hd3)
    w_f = _pad_cols(w_in[:, hd3:]).astype(BF16)
    fg = _matmul(h2, w_f, F32, tn=w_f.shape[1]).reshape(b, s, -1)
    cum = _logsig_cumsum(fg, f_bias)
    o = _fox_attention(qkv, cum, n_heads)
    return _matmul_residual(o.reshape(b * s, -1), w_out.astype(BF16), x.reshape(b * s, d), g1, s)


def _diff_layer(h, x, g1, w_in, lam_p, sub_gain, w_out, rel_bias, bias_tiles, lam_init):
    b, s, d = h.shape
    n_heads = w_in.shape[1] // (6 * HEAD_DIM)
    h2 = h.reshape(b * s, d)
    n_cols = w_in.shape[1]
    qkv = _matmul(h2, w_in.astype(BF16), BF16, col_scale=_query_col_scale(n_cols // 3, n_cols)).reshape(b, s, -1)
    o = _diff_attention(qkv, bias_tiles, rel_bias, lam_p, sub_gain, n_heads, lam_init)
    return _matmul_residual(o.reshape(b * s, -1), w_out.astype(BF16), x.reshape(b * s, d), g1, s)


def _moe_layer(h, x2, g2, w_router, b_router, wgu, b_gu, w_dn, b_dn):
    b, s, d = h.shape
    n = b * s
    ne = wgu.shape[0]
    h2 = h.reshape(n, d)
    idx, wts = _router(h2, w_router, b_router)
    tm = min(MOE_ROW_TILE, n * TOP_K)
    tmc = _row_tile(s, MOE_TOKEN_TILE)
    tile_e, tok_rows, w_rows, pos = _moe_dispatch_plan(idx[:, :TOP_K], wts[:, :TOP_K], ne, tm)
    tok_tiles = jnp.pad(tok_rows.reshape(-1, 1, tm), ((0, 1), (0, 0), (0, 0)))
    w_rows = jnp.broadcast_to(w_rows[:, None], (w_rows.shape[0], LANES))
    bgu = jnp.concatenate([b_gu[:, 0::2], b_gu[:, 1::2]], axis=1)
    y_rows = _moe_experts_sparse(h2, tile_e, tok_tiles, w_rows, wgu, bgu, w_dn.astype(BF16), b_dn, tm)
    pos_tiles = jnp.transpose(pos.reshape(n // tmc, tmc, TOP_K), (0, 2, 1)).reshape(n // tmc, TOP_K * tmc)
    pos_tiles = jnp.pad(pos_tiles, ((0, 1), (0, 0)))[:, None, :]
    return _moe_combine(y_rows, pos_tiles, x2, g2, s, tmc)


def kernel(x, c, rel_bias, norm_mix, norm_ffn, w_mod, b_mod, dsa_w_in, dsa_kv_gain, dsa_w_uk, dsa_w_uv, dsa_w_out, fox_w_in, fox_forget_bias, fox_w_out, diff_w_in, diff_lambda, diff_subln_gain, diff_w_out, w_router, b_router, w_gate_up, b_gate_up, w_down, b_down, final_norm):
    b, s, d = x.shape
    depth = w_mod.shape[0]
    mod = _modulation(c, w_mod, b_mod)
    t_attn = _row_tile(s, 256)
    bias_tiles = _bias_tiles(rel_bias, t_attn)
    ne, ff2 = w_gate_up.shape[1], w_gate_up.shape[3]
    wgu_all = _split_even_odd_cols(w_gate_up.reshape(depth * ne, d, ff2)).reshape(depth, ne, d, ff2)
    ia = ib = ic = 0
    for i in range(depth):
        sh1, sc1, g1, sh2, sc2, g2 = (mod[i, :, k * d:(k + 1) * d] for k in range(6))
        h = _norm_mod(x, norm_mix[i], sc1, sh1)
        kind = i % N_MIXERS
        if kind == 0:
            x2 = _dsa_layer(h, x, g1, dsa_w_in[ia], dsa_kv_gain[ia], dsa_w_uk[ia], dsa_w_uv[ia],
                            dsa_w_out[ia], rel_bias, bias_tiles)
            ia += 1
        elif kind == 1:
            x2 = _fox_layer(h, x, g1, fox_w_in[ib], fox_forget_bias[ib], fox_w_out[ib])
            ib += 1
        else:
            lam_init = 0.8 - 0.6 * math.exp(-0.3 * i)
            x2 = _diff_layer(h, x, g1, diff_w_in[ic], diff_lambda[ic], diff_subln_gain[ic],
                             diff_w_out[ic], rel_bias, bias_tiles, lam_init)
            ic += 1
        x = x2.reshape(b, s, d)
        h = _norm_mod(x, norm_ffn[i], sc2, sh2, out_dtype=F32)
        x = _moe_layer(h, x2, g2, w_router[i], b_router[i], wgu_all[i], b_gate_up[i],
                       w_down[i], b_down[i]).reshape(b, s, d)
    return _norm(x, final_norm, x.dtype)
```

```python
import functools
import math

import numpy as np
import jax
import jax.numpy as jnp
from jax import lax
from jax.experimental import pallas as pl
from jax.experimental.pallas import tpu as pltpu

HEAD_DIM = 128
IDX_DIM = 64
TOPK_MAX = 256
TOP_K = 4
SWIGLU_LIMIT = 7.0
SWIGLU_ALPHA = 1.702
NUM_BUCKETS = 32
MAX_DISTANCE = 128
RMS_EPS = 1e-6
NEG_INF = -1e30
N_MIXERS = 3
LANES = 128
COUNT_ROWS = 64
INT_MIN = -(2 ** 31)
LOG2E = 1.4426950408889634
QUERY_SCALE = HEAD_DIM ** -0.5 * LOG2E

F32 = jnp.float32
BF16 = jnp.bfloat16

VMEM_LIMIT = 56 * 1024 * 1024
HEADS_PER_STEP = 4
MOE_ROW_TILE = 256
MOE_TOKEN_TILE = 128


def _cparams(sem):
    return pltpu.CompilerParams(dimension_semantics=sem, vmem_limit_bytes=VMEM_LIMIT)


def _sigmoid(x):
    return 1.0 / (1.0 + jnp.exp(-x))


def _row_tile(s, want):
    t = min(s, want)
    assert s % t == 0
    return t


def _pad_cols(w, mult=LANES):
    n = w.shape[-1]
    pad = (-n) % mult
    if pad:
        w = jnp.pad(w, [(0, 0)] * (w.ndim - 1) + [(0, pad)])
    return w


def _mod_kernel(c_ref, w_ref, b_ref, o_ref):
    c = c_ref[...]
    cs = (c * _sigmoid(c)).astype(BF16)
    o_ref[0] = jnp.dot(cs, w_ref[0].astype(BF16), preferred_element_type=F32) + b_ref[0]


def _modulation(c, w_mod, b_mod):
    depth, d, n = w_mod.shape
    b = c.shape[0]
    tn = min(n, 1024)
    return pl.pallas_call(
        _mod_kernel,
        grid=(depth, n // tn),
        in_specs=[pl.BlockSpec((b, d), lambda i, j: (0, 0)),
                  pl.BlockSpec((1, d, tn), lambda i, j: (i, 0, j)),
                  pl.BlockSpec((1, 1, tn), lambda i, j: (i, 0, j))],
        out_specs=pl.BlockSpec((1, b, tn), lambda i, j: (i, 0, j)),
        out_shape=jax.ShapeDtypeStruct((depth, b, n), F32),
        compiler_params=_cparams(("parallel", "parallel")),
        name="adaln_modulation",
    )(c, w_mod, b_mod.reshape(depth, 1, n))


def _norm_mod_kernel(x_ref, g_ref, sc_ref, sh_ref, o_ref):
    x = x_ref[0]
    ms = jnp.mean(x * x, axis=-1, keepdims=True)
    y = x * lax.rsqrt(ms + RMS_EPS) * g_ref[...]
    o_ref[0] = (y * (1.0 + sc_ref[0]) + sh_ref[0]).astype(o_ref.dtype)


def _norm_kernel(x_ref, g_ref, o_ref):
    x = x_ref[0]
    ms = jnp.mean(x * x, axis=-1, keepdims=True)
    o_ref[0] = (x * lax.rsqrt(ms + RMS_EPS) * g_ref[...]).astype(o_ref.dtype)


def _norm_mod(x, g, sc, sh, out_dtype=BF16):
    b, s, d = x.shape
    ts = _row_tile(s, 512)
    vec = pl.BlockSpec((1, 1, d), lambda i, j: (i, 0, 0))
    return pl.pallas_call(
        _norm_mod_kernel,
        grid=(b, s // ts),
        in_specs=[pl.BlockSpec((1, ts, d), lambda i, j: (i, j, 0)),
                  pl.BlockSpec((1, d), lambda i, j: (0, 0)), vec, vec],
        out_specs=pl.BlockSpec((1, ts, d), lambda i, j: (i, j, 0)),
        out_shape=jax.ShapeDtypeStruct((b, s, d), out_dtype),
        compiler_params=_cparams(("parallel", "parallel")),
        name="rmsnorm_adaln",
    )(x, g.reshape(1, d), sc.reshape(b, 1, d), sh.reshape(b, 1, d))


def _norm(x, g, out_dtype, col_block=0):
    b, s, _ = x.shape
    d = g.shape[-1]
    ts = _row_tile(s, 512)
    return pl.pallas_call(
        _norm_kernel,
        grid=(b, s // ts),
        in_specs=[pl.BlockSpec((1, ts, d), lambda i, j: (i, j, col_block)),
                  pl.BlockSpec((1, d), lambda i, j: (0, 0))],
        out_specs=pl.BlockSpec((1, ts, d), lambda i, j: (i, j, 0)),
        out_shape=jax.ShapeDtypeStruct((b, s, d), out_dtype),
        compiler_params=_cparams(("parallel", "parallel")),
        name="rmsnorm",
    )(x, g.reshape(1, d))


def _mm_kernel(a_ref, b_ref, o_ref):
    o_ref[...] = jnp.dot(a_ref[...], b_ref[...], preferred_element_type=F32).astype(o_ref.dtype)


def _mm_res_kernel(a_ref, b_ref, x_ref, g_ref, o_ref):
    y = jnp.dot(a_ref[...], b_ref[...], preferred_element_type=F32)
    o_ref[...] = x_ref[...] + g_ref[0] * y


def _mm_scaled_kernel(a_ref, b_ref, s_ref, o_ref):
    y = jnp.dot(a_ref[...], b_ref[...], preferred_element_type=F32)
    o_ref[...] = (y * s_ref[...]).astype(o_ref.dtype)


def _matmul(a, w, out_dtype, tm=1024, tn=512, col_scale=None):
    m, k = a.shape
    n = w.shape[1]
    tm = _row_tile(m, tm)
    tn = _row_tile(n, tn)
    in_specs = [pl.BlockSpec((tm, k), lambda i, j: (i, 0)),
                pl.BlockSpec((k, tn), lambda i, j: (0, j))]
    args = (a, w)
    if col_scale is not None:
        in_specs.append(pl.BlockSpec((1, tn), lambda i, j: (0, j)))
        args += (col_scale.reshape(1, n),)
    return pl.pallas_call(
        _mm_kernel if col_scale is None else _mm_scaled_kernel,
        grid=(m // tm, n // tn),
        in_specs=in_specs,
        out_specs=pl.BlockSpec((tm, tn), lambda i, j: (i, j)),
        out_shape=jax.ShapeDtypeStruct((m, n), out_dtype),
        compiler_params=_cparams(("parallel", "parallel")),
        name="matmul",
    )(*args)


def _query_col_scale(n_query_cols, n_cols):
    return jnp.concatenate([jnp.full((n_query_cols,), QUERY_SCALE, F32), jnp.ones((n_cols - n_query_cols,), F32)])


def _matmul_residual(a, w, x, gate, rows_per_batch, tm=1024, tn=512):
    m, k = a.shape
    n = w.shape[1]
    tm = _row_tile(rows_per_batch, tm)
    tn = _row_tile(n, tn)
    tiles_per_batch = rows_per_batch // tm
    nb = gate.shape[0]
    return pl.pallas_call(
        _mm_res_kernel,
        grid=(m // tm, n // tn),
        in_specs=[pl.BlockSpec((tm, k), lambda i, j: (i, 0)),
                  pl.BlockSpec((k, tn), lambda i, j: (0, j)),
                  pl.BlockSpec((tm, tn), lambda i, j: (i, j)),
                  pl.BlockSpec((1, 1, tn), lambda i, j: (i // tiles_per_batch, 0, j))],
        out_specs=pl.BlockSpec((tm, tn), lambda i, j: (i, j)),
        out_shape=jax.ShapeDtypeStruct((m, n), F32),
        compiler_params=_cparams(("parallel", "parallel")),
        name="matmul_residual",
    )(a, w, x, gate.reshape(nb, 1, n))


def _kq(k, q):
    return lax.dot_general(k, q, (((1,), (1,)), ((), ())), preferred_element_type=F32)


def _causal_tile_mask_t(t):
    r = lax.broadcasted_iota(jnp.int32, (t, t), 0)
    c = lax.broadcasted_iota(jnp.int32, (t, t), 1)
    return r <= c


def _t5_bucket_np(dist):
    n = np.maximum(dist, 0)
    max_exact = NUM_BUCKETS // 2
    nf = np.maximum(n, 1).astype(np.float32)
    large = max_exact + (np.log(nf / np.float32(max_exact)) / np.float32(math.log(MAX_DISTANCE / max_exact))
                         * np.float32(NUM_BUCKETS - max_exact)).astype(np.int32)
    large = np.minimum(large, NUM_BUCKETS - 1)
    return np.where(n < max_exact, n, large).astype(np.int32)


def _bias_tile_kernel(bk_ref, rel_ref, o_ref):
    m = pl.program_id(0)
    bk = bk_ref[...]
    acc = jnp.zeros(bk.shape, F32)
    for b in range(NUM_BUCKETS):
        acc = jnp.where(bk == b, rel_ref[b, m], acc)
    o_ref[0] = acc * LOG2E


def _bias_tiles(rel_bias, t):
    assert t >= MAX_DISTANCE
    n_maps = rel_bias.shape[1]
    kk = np.arange(t)[:, None]
    qq = np.arange(t)[None, :]
    buckets = np.stack([_t5_bucket_np(qq - kk), _t5_bucket_np(t + qq - kk)])
    return pl.pallas_call(
        _bias_tile_kernel,
        grid=(n_maps,),
        in_specs=[pl.BlockSpec((2, t, t), lambda m: (0, 0, 0)),
                  pl.BlockSpec(memory_space=pltpu.SMEM)],
        out_specs=pl.BlockSpec((1, 2, t, t), lambda m: (m, 0, 0, 0)),
        out_shape=jax.ShapeDtypeStruct((n_maps, 2, t, t), F32),
        compiler_params=_cparams(("parallel",)),
        name="t5_bias_tiles",
    )(jnp.asarray(buckets), rel_bias)


def _pipelined_sweep(qi, n_chain, t, logits_fn, bias_fn, values_fn, raw_scr, p_scr, acc_ref, near_tile):
    acc_ref[...] = jnp.zeros_like(acc_ref)
    p_scr[...] = jnp.zeros_like(p_scr)
    first = logits_fn(0)
    for c in range(n_chain):
        raw_scr[c] = first[c]

    def flush_values(kj, state):
        for c in range(n_chain):
            acc_ref[c] = state[c][2] * acc_ref[c] + jnp.dot(values_fn(c, kj), p_scr[c], preferred_element_type=F32)

    def step(kj, state, kind, prefetch):
        slot = kj % 2
        flush_values(jnp.maximum(kj - 1, 0), state)
        new_state = []
        for c in range(n_chain):
            m_old, l_old, _ = state[c]
            u, row_const = bias_fn(c, kj, raw_scr[slot * n_chain + c], kind)
            m_new = jnp.maximum(m_old, jnp.max(u, axis=0, keepdims=True) + row_const)
            alpha = jnp.exp2(m_old - m_new)
            p = jnp.exp2(u + (row_const - m_new))
            p_scr[c] = p.astype(p_scr.dtype)
            new_state.append((m_new, alpha * l_old + jnp.sum(p, axis=0, keepdims=True), alpha))
        if prefetch:
            nxt = logits_fn(kj + 1)
            for c in range(n_chain):
                raw_scr[(1 - slot) * n_chain + c] = nxt[c]
        return tuple(new_state)

    init = (jnp.full((1, t), NEG_INF, F32), jnp.zeros((1, t), F32), jnp.ones((1, t), F32))
    state = (init,) * n_chain
    n_far = jnp.maximum(qi - 1, 0) if near_tile else qi
    state = lax.fori_loop(0, n_far, lambda kj, s: step(kj, s, "far", True), state)
    if near_tile:
        state = lax.fori_loop(n_far, qi, lambda kj, s: step(kj, s, "near", True), state)
    state = step(qi, state, "diag", False)
    flush_values(qi, state)
    return [s[1] for s in state]


def _sweep_scratch(n_chain, dv, t):
    return [pltpu.VMEM((n_chain, dv, t), F32), pltpu.VMEM((2 * n_chain, t, t), F32),
            pltpu.VMEM((n_chain, t, t), BF16)]


def _logsig_cumsum_kernel(f_ref, fb_ref, o_ref, carry_ref):
    @pl.when(pl.program_id(1) == 0)
    def _():
        carry_ref[...] = jnp.zeros_like(carry_ref)

    z = f_ref[0] + fb_ref[...]
    logf = -(jnp.maximum(-z, 0.0) + jnp.log(1.0 + jnp.exp(-jnp.abs(z))))
    t = z.shape[0]
    r = lax.broadcasted_iota(jnp.int32, (t, t), 0)
    c = lax.broadcasted_iota(jnp.int32, (t, t), 1)
    tri = jnp.where(r >= c, 1.0, 0.0).astype(BF16)
    hi = logf.astype(BF16)
    r1 = logf - hi.astype(F32)
    mid = r1.astype(BF16)
    lo = (r1 - mid.astype(F32)).astype(BF16)
    cum = (jnp.dot(tri, hi, preferred_element_type=F32)
           + jnp.dot(tri, mid, preferred_element_type=F32)
           + jnp.dot(tri, lo, preferred_element_type=F32)) + carry_ref[...]
    o_ref[0] = cum * LOG2E
    carry_ref[...] = cum[t - 1:t, :]


def _logsig_cumsum(fg, f_bias):
    b, s, w = fg.shape
    tc = _row_tile(s, 512)
    fb = jnp.pad(f_bias, (0, w - f_bias.shape[0])).reshape(1, w)
    return pl.pallas_call(
        _logsig_cumsum_kernel,
        grid=(b, s // tc),
        in_specs=[pl.BlockSpec((1, tc, w), lambda i, j: (i, j, 0)),
                  pl.BlockSpec((1, w), lambda i, j: (0, 0))],
        out_specs=pl.BlockSpec((1, tc, w), lambda i, j: (i, j, 0)),
        out_shape=jax.ShapeDtypeStruct((b, s, w), F32),
        scratch_shapes=[pltpu.VMEM((1, w), F32)],
        compiler_params=_cparams(("parallel", "arbitrary")),
        name="fox_logsig_cumsum",
    )(fg, fb)


def _fox_kernel(q_ref, k_ref, vt_ref, cq_ref, ck_ref, o_ref, acc_ref, raw_scr, p_scr, *, t, group):
    qi = pl.program_id(2)
    hd = HEAD_DIM
    q = q_ref[0]

    def key_rows(kj):
        return pl.ds(pl.multiple_of(kj * t, t), t)

    def logits_fn(kj):
        k = k_ref[0, key_rows(kj), :]
        return [_kq(k[:, g * hd:(g + 1) * hd], q[:, g * hd:(g + 1) * hd]) for g in range(group)]

    def bias_fn(g, kj, raw, kind):
        ck = jnp.tile(ck_ref[0, g, key_rows(kj), :], (1, t // LANES))
        u = raw - ck
        if kind == "diag":
            u = jnp.where(_causal_tile_mask_t(t), u, NEG_INF)
        return u, cq_ref[0, g]

    def values_fn(g, kj):
        return vt_ref[0, g * hd:(g + 1) * hd, key_rows(kj)]

    denom = _pipelined_sweep(qi, group, t, logits_fn, bias_fn, values_fn, raw_scr, p_scr, acc_ref, near_tile=False)
    o_ref[0] = jnp.concatenate([(acc_ref[g] / denom[g]).T for g in range(group)], axis=1).astype(o_ref.dtype)


def _fox_attention(qkv, cum, n_heads, group=HEADS_PER_STEP):
    b, s, _ = qkv.shape
    t = _row_tile(s, 256)
    hd = HEAD_DIM
    group = min(group, n_heads)
    w = group * hd
    ng = n_heads // group
    cum_t = jnp.transpose(cum[:, :, :n_heads], (0, 2, 1))
    cum_q = cum_t[:, :, None, :]
    cum_k = jnp.broadcast_to(cum_t[..., None], (b, n_heads, s, LANES))
    vt = jnp.transpose(qkv[:, :, 2 * n_heads * hd:], (0, 2, 1))
    return pl.pallas_call(
        functools.partial(_fox_kernel, t=t, group=group),
        grid=(b, ng, s // t),
        in_specs=[pl.BlockSpec((1, t, w), lambda i, h, j: (i, j, h)),
                  pl.BlockSpec((1, s, w), lambda i, h, j: (i, 0, ng + h)),
                  pl.BlockSpec((1, w, s), lambda i, h, j: (i, h, 0)),
                  pl.BlockSpec((1, group, 1, t), lambda i, h, j: (i, h, 0, j)),
                  pl.BlockSpec((1, group, s, LANES), lambda i, h, j: (i, h, 0, 0))],
        out_specs=pl.BlockSpec((1, t, w), lambda i, h, j: (i, j, h)),
        out_shape=jax.ShapeDtypeStruct((b, s, n_heads * hd), BF16),
        scratch_shapes=_sweep_scratch(group, hd, t),
        compiler_params=_cparams(("parallel", "parallel", "arbitrary")),
        name="fox_attention",
    )(qkv, qkv, vt, cum_q, cum_k)


def _diff_kernel(q_ref, k_ref, vt_ref, bias_ref, rel_ref, lam_ref, gain_ref, o_ref, acc_ref, raw_scr, p_scr,
                 *, t, lam_init, group):
    hg = pl.program_id(1)
    qi = pl.program_id(2)
    q = q_ref[0]
    hd = HEAD_DIM
    n_chain = 2 * group

    def key_rows(kj):
        return pl.ds(pl.multiple_of(kj * t, t), t)

    def logits_fn(kj):
        k = k_ref[0, key_rows(kj), :]
        return [_kq(k[:, c * hd:(c + 1) * hd], q[:, c * hd:(c + 1) * hd]) for c in range(n_chain)]

    def bias_fn(c, kj, raw, kind):
        if kind == "far":
            return raw, rel_ref[NUM_BUCKETS - 1, n_chain * hg + c] * LOG2E
        if kind == "near":
            return raw + bias_ref[c, 1], 0.0
        return jnp.where(_causal_tile_mask_t(t), raw + bias_ref[c, 0], NEG_INF), 0.0

    def values_fn(c, kj):
        return vt_ref[0, (c // 2) * 2 * hd:(c // 2 + 1) * 2 * hd, key_rows(kj)]

    denom = _pipelined_sweep(qi, n_chain, t, logits_fn, bias_fn, values_fn, raw_scr, p_scr, acc_ref, near_tile=True)

    lp = lam_ref[...]
    lam = (jnp.exp(jnp.sum(lp[0:1] * lp[1:2], axis=-1, keepdims=True))
           - jnp.exp(jnp.sum(lp[2:3] * lp[3:4], axis=-1, keepdims=True)) + lam_init)
    outs = []
    for g in range(group):
        o = (acc_ref[2 * g] / denom[2 * g] - lam * (acc_ref[2 * g + 1] / denom[2 * g + 1])).T
        ms = jnp.mean(o * o, axis=-1, keepdims=True)
        outs.append(o * lax.rsqrt(ms + RMS_EPS) * gain_ref[...] * (1.0 - lam_init))
    o_ref[0] = jnp.concatenate(outs, axis=1).astype(o_ref.dtype)


def _diff_attention(qkv, bias_tiles, rel_bias, lam_p, sub_gain, n_heads, lam_init, group=HEADS_PER_STEP // 2):
    b, s, _ = qkv.shape
    t = bias_tiles.shape[-1]
    group = min(group, n_heads)
    hw = 2 * HEAD_DIM
    w = group * hw
    ng = n_heads // group
    vt = jnp.transpose(qkv[:, :, 2 * n_heads * hw:], (0, 2, 1))
    return pl.pallas_call(
        functools.partial(_diff_kernel, t=t, lam_init=lam_init, group=group),
        grid=(b, ng, s // t),
        in_specs=[pl.BlockSpec((1, t, w), lambda i, h, j: (i, j, h)),
                  pl.BlockSpec((1, s, w), lambda i, h, j: (i, 0, ng + h)),
                  pl.BlockSpec((1, w, s), lambda i, h, j: (i, h, 0)),
                  pl.BlockSpec((2 * group, 2, t, t), lambda i, h, j: (h, 0, 0, 0)),
                  pl.BlockSpec(memory_space=pltpu.SMEM),
                  pl.BlockSpec((4, HEAD_DIM), lambda i, h, j: (0, 0)),
                  pl.BlockSpec((1, hw), lambda i, h, j: (0, 0))],
        out_specs=pl.BlockSpec((1, t, w), lambda i, h, j: (i, j, h)),
        out_shape=jax.ShapeDtypeStruct((b, s, n_heads * hw), BF16),
        scratch_shapes=_sweep_scratch(2 * group, hw, t),
        compiler_params=_cparams(("parallel", "parallel", "arbitrary")),
        name="diff_attention",
    )(qkv, qkv, vt, bias_tiles, rel_bias, lam_p, sub_gain.reshape(1, hw))


def _float_key(x):
    bits = pltpu.bitcast(x, jnp.int32)
    return jnp.where(bits < 0, bits ^ jnp.int32(0x7FFFFFFF), bits)


def _indexer_kernel(q_ref, kx_ref, wt_ref, mask_ref, key_ref, *, t, s, n_heads, n_sel, kc):
    qi = pl.program_id(1)
    qpos = qi * t + lax.broadcasted_iota(jnp.int32, (1, t), 1)
    n_chunks = ((qi + 1) * t + kc - 1) // kc
    wt = wt_ref[0]
    qh = [q_ref[0][:, h * IDX_DIM:(h + 1) * IDX_DIM].astype(BF16) for h in range(n_heads)]

    def kpos_of(c):
        return c * kc + lax.broadcasted_iota(jnp.int32, (kc, 1), 0)

    def rows(c):
        return pl.ds(pl.multiple_of(c * kc, kc), kc)

    def score_chunk(c, carry):
        kk = kx_ref[0, rows(c), :][:, :IDX_DIM].astype(BF16)
        score = jnp.zeros((kc, t), F32)
        for h in range(n_heads):
            d = lax.dot_general(kk, qh[h], (((1,), (1,)), ((), ())), preferred_element_type=F32)
            score = score + wt[h:h + 1, :] * jnp.maximum(d, 0.0)
        score = jnp.where(kpos_of(c) <= qpos, score + 0.0, NEG_INF)
        key_ref[rows(c), :] = _float_key(score)
        return carry

    lax.fori_loop(0, n_chunks, score_chunk, 0)

    def count(pred):
        def body(c, acc):
            hit = jnp.where(pred(key_ref[rows(c), :], kpos_of(c)), 1.0, 0.0)
            return acc + jnp.sum(hit.reshape(kc // COUNT_ROWS, COUNT_ROWS, t), axis=0)
        acc = lax.fori_loop(0, n_chunks, body, jnp.zeros((COUNT_ROWS, t), F32))
        return jnp.sum(acc, axis=0, keepdims=True)

    def value_bit(it, tau_u):
        cand_u = tau_u | jnp.left_shift(jnp.int32(1), 31 - it)
        cand = cand_u ^ jnp.int32(INT_MIN)
        return jnp.where(count(lambda key, kpos: key >= cand) >= n_sel, cand_u, tau_u)

    tau = lax.fori_loop(0, 32, value_bit, jnp.zeros((1, t), jnp.int32)) ^ jnp.int32(INT_MIN)

    need = n_sel - count(lambda key, kpos: key > tau)
    n_bits = int(math.log2(s))

    def index_bit(it, bound):
        cand = bound | jnp.left_shift(jnp.int32(1), n_bits - 1 - it)
        below = count(lambda key, kpos: (key == tau) & (kpos < cand))
        return jnp.where(below < need, cand, bound)

    tied = jnp.max(count(lambda key, kpos: key >= tau)) > n_sel
    bound = lax.cond(tied,
                     lambda: lax.fori_loop(0, n_bits, index_bit, jnp.zeros((1, t), jnp.int32)),
                     lambda: jnp.full((1, t), s - 1, jnp.int32))

    def write_chunk(c, carry):
        key = key_ref[rows(c), :]
        kpos = kpos_of(c)
        sel = ((key > tau) | ((key == tau) & (kpos <= bound))) & (kpos <= qpos)
        mask_ref[0, rows(c), :] = jnp.where(sel, 1, 0).astype(jnp.int8)
        return carry

    def zero_chunk(c, carry):
        mask_ref[0, rows(c), :] = jnp.zeros((kc, t), jnp.int8)
        return carry

    lax.fori_loop(0, n_chunks, write_chunk, 0)
    lax.fori_loop(n_chunks, s // kc, zero_chunk, 0)


def _indexer_mask(rest, w_t, t, n_heads, n_sel, qidx_block, kx_block):
    b, s, _ = rest.shape
    assert s & (s - 1) == 0
    kc = min(s, 512)
    assert kc >= n_sel and kc % t == 0
    wq = n_heads * IDX_DIM
    return pl.pallas_call(
        functools.partial(_indexer_kernel, t=t, s=s, n_heads=n_heads, n_sel=n_sel, kc=kc),
        grid=(b, s // t),
        in_specs=[pl.BlockSpec((1, t, wq), lambda i, j: (i, j, qidx_block)),
                  pl.BlockSpec((1, s, LANES), lambda i, j: (i, 0, kx_block)),
                  pl.BlockSpec((1, n_heads, t), lambda i, j: (i, 0, j))],
        out_specs=pl.BlockSpec((1, s, t), lambda i, j: (i, 0, j)),
        out_shape=jax.ShapeDtypeStruct((b, s, s), jnp.int8),
        scratch_shapes=[pltpu.VMEM((s, t), jnp.int32)],
        compiler_params=_cparams(("parallel", "parallel")),
        name="dsa_indexer_topk_mask",
    )(rest, rest, w_t)


def _dsa_kernel(q_ref, k_ref, vt_ref, bias_ref, mask_ref, rel_ref, o_ref, acc_ref, raw_scr, p_scr, *, t, group):
    hg = pl.program_id(1)
    qi = pl.program_id(2)
    q = q_ref[0]
    hd = HEAD_DIM

    def key_rows(kj):
        return pl.ds(pl.multiple_of(kj * t, t), t)

    def logits_fn(kj):
        k = k_ref[0, key_rows(kj), :]
        return [_kq(k[:, g * hd:(g + 1) * hd], q[:, g * hd:(g + 1) * hd]) for g in range(group)]

    def bias_fn(g, kj, raw, kind):
        keep = mask_ref[0, key_rows(kj), :].astype(F32) > 0.5
        if kind == "far":
            return jnp.where(keep, raw, NEG_INF), rel_ref[NUM_BUCKETS - 1, group * hg + g] * LOG2E
        return jnp.where(keep, raw + bias_ref[g, 1 if kind == "near" else 0], NEG_INF), 0.0

    def values_fn(g, kj):
        return vt_ref[0, g * hd:(g + 1) * hd, key_rows(kj)]

    denom = _pipelined_sweep(qi, group, t, logits_fn, bias_fn, values_fn, raw_scr, p_scr, acc_ref, near_tile=True)
    o_ref[0] = jnp.concatenate([(acc_ref[g] / denom[g]).T for g in range(group)], axis=1).astype(o_ref.dtype)


def _dsa_attention(q, kv, mask_t, bias_tiles, rel_bias, n_heads, group=HEADS_PER_STEP):
    b, s, _ = q.shape
    t = bias_tiles.shape[-1]
    hd = HEAD_DIM
    group = min(group, n_heads)
    w = group * hd
    ng = n_heads // group
    vt = jnp.transpose(kv[:, :, n_heads * hd:], (0, 2, 1))
    return pl.pallas_call(
        functools.partial(_dsa_kernel, t=t, group=group),
        grid=(b, ng, s // t),
        in_specs=[pl.BlockSpec((1, t, w), lambda i, h, j: (i, j, h)),
                  pl.BlockSpec((1, s, w), lambda i, h, j: (i, 0, h)),
                  pl.BlockSpec((1, w, s), lambda i, h, j: (i, h, 0)),
                  pl.BlockSpec((group, 2, t, t), lambda i, h, j: (h, 0, 0, 0)),
                  pl.BlockSpec((1, s, t), lambda i, h, j: (i, 0, j)),
                  pl.BlockSpec(memory_space=pltpu.SMEM)],
        out_specs=pl.BlockSpec((1, t, w), lambda i, h, j: (i, j, h)),
        out_shape=jax.ShapeDtypeStruct((b, s, n_heads * hd), BF16),
        scratch_shapes=_sweep_scratch(group, hd, t),
        compiler_params=_cparams(("parallel", "parallel", "arbitrary")),
        name="dsa_attention",
    )(q, kv, vt, bias_tiles, mask_t, rel_bias)


def _router_kernel(h_ref, w_ref, b_ref, idx_ref, wt_ref):
    logits = jnp.dot(h_ref[...].astype(BF16), w_ref[...], preferred_element_type=F32) + b_ref[...]
    lane = lax.broadcasted_iota(jnp.int32, logits.shape, 1).astype(F32)
    top_v, top_i = [], []
    for _ in range(TOP_K):
        mx = jnp.max(logits, axis=-1, keepdims=True)
        first = jnp.min(jnp.where(logits == mx, lane, float(LANES)), axis=-1, keepdims=True)
        top_v.append(mx)
        top_i.append(first)
        logits = jnp.where(lane == first, -jnp.inf, logits)
    e = [jnp.exp(v - top_v[0]) for v in top_v]
    denom = e[0] + e[1] + e[2] + e[3]
    idx = jnp.zeros(logits.shape, F32)
    wts = jnp.zeros(logits.shape, F32)
    for k in range(TOP_K):
        idx = jnp.where(lane == float(k), top_i[k], idx)
        wts = jnp.where(lane == float(k), e[k] / denom, wts)
    idx_ref[...] = idx.astype(jnp.int32)
    wt_ref[...] = wts


def _router(h2, w_router, b_router):
    n, d = h2.shape
    ne = w_router.shape[1]
    tm = _row_tile(n, 1024)
    w = _pad_cols(w_router).astype(BF16)
    bias = jnp.concatenate([b_router, jnp.full((LANES - ne,), NEG_INF, F32)]).reshape(1, LANES)
    out = pl.BlockSpec((tm, LANES), lambda i: (i, 0))
    return pl.pallas_call(
        _router_kernel,
        grid=(n // tm,),
        in_specs=[pl.BlockSpec((tm, d), lambda i: (i, 0)),
                  pl.BlockSpec((d, LANES), lambda i: (0, 0)),
                  pl.BlockSpec((1, LANES), lambda i: (0, 0))],
        out_specs=[out, out],
        out_shape=[jax.ShapeDtypeStruct((n, LANES), jnp.int32), jax.ShapeDtypeStruct((n, LANES), F32)],
        compiler_params=_cparams(("parallel",)),
        name="moe_router_top4",
    )(h2, w, bias)


def _split_even_odd_kernel(w_ref, p_ref, o_ref):
    o_ref[0] = jnp.dot(w_ref[0].astype(BF16), p_ref[...], preferred_element_type=F32).astype(o_ref.dtype)


def _split_even_odd_cols(w):
    e, d, n = w.shape
    f = n // 2
    perm = np.zeros((n, n), np.float32)
    perm[2 * np.arange(f), np.arange(f)] = 1.0
    perm[2 * np.arange(f) + 1, f + np.arange(f)] = 1.0
    td = _row_tile(d, 512)
    return pl.pallas_call(
        _split_even_odd_kernel,
        grid=(e, d // td),
        in_specs=[pl.BlockSpec((1, td, n), lambda i, j: (i, j, 0)),
                  pl.BlockSpec((n, n), lambda i, j: (0, 0))],
        out_specs=pl.BlockSpec((1, td, n), lambda i, j: (i, j, 0)),
        out_shape=jax.ShapeDtypeStruct((e, d, n), BF16),
        compiler_params=_cparams(("parallel", "parallel")),
        name="gate_up_split_columns",
    )(w, jnp.asarray(perm, BF16))


def _expert_ffn(h, wgu, bgu, wd, bd):
    f = wgu.shape[1] // 2
    gu = jnp.dot(h, wgu, preferred_element_type=F32) + bgu
    g = jnp.minimum(gu[:, :f], SWIGLU_LIMIT)
    u = jnp.clip(gu[:, f:], -SWIGLU_LIMIT, SWIGLU_LIMIT)
    act = (u + 1.0) * (g * _sigmoid(g * SWIGLU_ALPHA))
    return jnp.dot(act.astype(BF16), wd, preferred_element_type=F32) + bd


def _row_gather_start(src_hbm, rows_ref, dst, sem, n_rows):
    for r in range(n_rows):
        copy = pltpu.make_async_copy(src_hbm.at[pl.ds(rows_ref[0, 0, r], 1)], dst.at[pl.ds(r, 1)], sem)
        copy.start(priority=r % 2)


def _row_gather_wait(src_hbm, dst, sem):
    pltpu.make_async_copy(src_hbm.at[pl.ds(0, dst.shape[0])], dst, sem).wait()


def _moe_expert_kernel(te_ref, tok_ref, tok_next_ref, wrow_ref, wgu_ref, bgu_ref, wd_ref, bd_ref, h_hbm,
                       o_ref, xbuf, sem, *, tm):
    i = pl.program_id(0)
    slot = i % 2

    @pl.when(i == 0)
    def _():
        _row_gather_start(h_hbm, tok_ref, xbuf.at[0], sem.at[0], tm)

    _row_gather_start(h_hbm, tok_next_ref, xbuf.at[1 - slot], sem.at[1 - slot], tm)
    _row_gather_wait(h_hbm, xbuf.at[slot], sem.at[slot])
    y = _expert_ffn(xbuf[slot].astype(BF16), wgu_ref[0], bgu_ref[0], wd_ref[0], bd_ref[0])
    o_ref[...] = y * jnp.tile(wrow_ref[...], (1, y.shape[1] // LANES))

    @pl.when(i == pl.num_programs(0) - 1)
    def _():
        _row_gather_wait(h_hbm, xbuf.at[1 - slot], sem.at[1 - slot])


def _moe_experts_sparse(h2, tile_expert, tok_tiles, w_rows, wgu, bgu, wd, bd, tm):
    n, d = h2.shape
    ne, _, f2 = wgu.shape
    n_tiles = tile_expert.shape[0]
    exp3 = lambda i, te: (te[i], 0, 0)
    smem_rows = lambda off: pl.BlockSpec((1, 1, tm), lambda i, te: (i + off, 0, 0), memory_space=pltpu.SMEM)
    grid_spec = pltpu.PrefetchScalarGridSpec(
        num_scalar_prefetch=1,
        grid=(n_tiles,),
        in_specs=[smem_rows(0), smem_rows(1),
                  pl.BlockSpec((tm, LANES), lambda i, te: (i, 0)),
                  pl.BlockSpec((1, d, f2), exp3), pl.BlockSpec((1, 1, f2), exp3),
                  pl.BlockSpec((1, f2 // 2, d), exp3), pl.BlockSpec((1, 1, d), exp3),
                  pl.BlockSpec(memory_space=pl.ANY)],
        out_specs=pl.BlockSpec((tm, d), lambda i, te: (i, 0)),
        scratch_shapes=[pltpu.VMEM((2, tm, d), F32), pltpu.SemaphoreType.DMA((2,))],
    )
    return pl.pallas_call(
        functools.partial(_moe_expert_kernel, tm=tm),
        grid_spec=grid_spec,
        out_shape=jax.ShapeDtypeStruct((n_tiles * tm, d), F32),
        compiler_params=_cparams(("arbitrary",)),
        name="moe_experts_sparse",
    )(tile_expert, tok_tiles, tok_tiles, w_rows, wgu, bgu.reshape(ne, 1, f2), wd, bd.reshape(ne, 1, d), h2)


def _moe_combine_kernel(pos_ref, pos_next_ref, x_ref, g2_ref, y_hbm, o_ref, ybuf, sem, *, tm):
    i = pl.program_id(0)
    slot = i % 2
    n_rows = TOP_K * tm

    @pl.when(i == 0)
    def _():
        _row_gather_start(y_hbm, pos_ref, ybuf.at[0], sem.at[0], n_rows)

    _row_gather_start(y_hbm, pos_next_ref, ybuf.at[1 - slot], sem.at[1 - slot], n_rows)
    _row_gather_wait(y_hbm, ybuf.at[slot], sem.at[slot])
    acc = ybuf[slot, 0:tm, :]
    for k in range(1, TOP_K):
        acc = acc + ybuf[slot, k * tm:(k + 1) * tm, :]
    o_ref[...] = x_ref[...] + g2_ref[0] * acc

    @pl.when(i == pl.num_programs(0) - 1)
    def _():
        _row_gather_wait(y_hbm, ybuf.at[1 - slot], sem.at[1 - slot])


def _moe_combine(y_rows, pos_tiles, x2, g2, rows_per_batch, tm):
    n, d = x2.shape
    nb = g2.shape[0]
    tiles_per_batch = rows_per_batch // tm
    smem_rows = lambda off: pl.BlockSpec((1, 1, TOP_K * tm), lambda i: (i + off, 0, 0), memory_space=pltpu.SMEM)
    return pl.pallas_call(
        functools.partial(_moe_combine_kernel, tm=tm),
        grid=(n // tm,),
        in_specs=[smem_rows(0), smem_rows(1),
                  pl.BlockSpec((tm, d), lambda i: (i, 0)),
                  pl.BlockSpec((1, 1, d), lambda i: (i // tiles_per_batch, 0, 0)),
                  pl.BlockSpec(memory_space=pl.ANY)],
        out_specs=pl.BlockSpec((tm, d), lambda i: (i, 0)),
        out_shape=jax.ShapeDtypeStruct((n, d), F32),
        scratch_shapes=[pltpu.VMEM((2, TOP_K * tm, d), F32), pltpu.SemaphoreType.DMA((2,))],
        compiler_params=_cparams(("arbitrary",)),
        name="moe_combine",
    )(pos_tiles, pos_tiles, x2, g2.reshape(nb, 1, d), y_rows)


def _moe_dispatch_plan(idx4, w4, ne, tm):
    n = idx4.shape[0]
    n_pairs = n * TOP_K
    n_rows = n_pairs + ne * tm
    big = 1 << (n_pairs + tm - 1).bit_length()
    unused = jnp.int32(ne * big)
    eid = idx4.reshape(-1)
    experts = jnp.arange(ne, dtype=jnp.int32)
    counts = jnp.sum((eid[:, None] == experts[None, :]).astype(jnp.int32), axis=0)
    n_pad = (-counts) % tm
    j = jnp.arange(tm, dtype=jnp.int32)
    pad_keys = jnp.where(j[None, :] < n_pad[:, None], experts[:, None] * big + n_pairs + j[None, :], unused)
    keys = jnp.concatenate([eid * big + jnp.arange(n_pairs, dtype=jnp.int32), pad_keys.reshape(-1)])
    wvals = jnp.concatenate([w4.reshape(-1), jnp.zeros((ne * tm,), F32)])
    skeys, w_rows = lax.sort((keys, wvals), num_keys=1)
    slot = skeys % big
    is_pair = (skeys < unused) & (slot < n_pairs)
    tok_rows = jnp.where(is_pair, slot // TOP_K, 0)
    tile_e = jnp.minimum(skeys[::tm] // big, ne - 1)
    _, row_of_pair = lax.sort((jnp.where(is_pair, slot, n_rows), jnp.arange(n_rows, dtype=jnp.int32)), num_keys=1)
    pos = row_of_pair[:n_pairs].reshape(n, TOP_K)
    return tile_e, tok_rows, w_rows, pos


def _dsa_layer(h, x, g1, w_in, kv_gain, w_uk, w_uv, w_out, rel_bias, bias_tiles):
    b, s, d = h.shape
    n_heads = d // HEAD_DIM
    rank = kv_gain.shape[0]
    idx_heads = n_heads // 2
    n_sel = min(TOPK_MAX, s // 4)
    o1 = n_heads * HEAD_DIM
    h2 = h.reshape(b * s, d)
    q = _matmul(h2, w_in[:, :o1].astype(BF16), BF16, col_scale=_query_col_scale(o1, o1)).reshape(b, s, o1)
    w_rest = _pad_cols(w_in[:, o1:]).astype(BF16)
    rest = _matmul(h2, w_rest, F32, tn=w_rest.shape[1]).reshape(b, s, -1)
    assert rank % LANES == 0 and (idx_heads * IDX_DIM) == rank
    ckv = _norm(rest, kv_gain, BF16)
    w_k = jnp.transpose(w_uk, (2, 0, 1)).reshape(rank, o1)
    w_v = jnp.transpose(w_uv, (1, 0, 2)).reshape(rank, o1)
    w_kv = jnp.concatenate([w_k, w_v], axis=1).astype(BF16)
    kv = _matmul(ckv.reshape(b * s, rank), w_kv, BF16).reshape(b, s, 2 * o1)
    t = bias_tiles.shape[-1]
    w_off = 2 * rank + IDX_DIM
    w_t = jnp.transpose(rest[:, :, w_off:w_off + idx_heads], (0, 2, 1))
    mask_t = _indexer_mask(rest, w_t, t, idx_heads, n_sel, qidx_block=1, kx_block=2 * rank // LANES)
    o = _dsa_attention(q, kv, mask_t, bias_tiles, rel_bias, n_heads)
    return _matmul_residual(o.reshape(b * s, o1), w_out.astype(BF16), x.reshape(b * s, d), g1, s)


def _fox_layer(h, x, g1, w_in, f_bias, w_out):
    b, s, d = h.shape
    n_heads = f_bias.shape[0]
    hd3 = 3 * n_heads * HEAD_DIM
    h2 = h.reshape(b * s, d)
    qkv = _matmul(h2, w_in[:, :hd3].astype(BF16), BF16,
                  col_scale=_query_col_scale(hd3 // 3, hd3)).reshape(b, s, hd3)
    w_f = _pad_cols(w_in[:, hd3:]).astype(BF16)
    fg = _matmul(h2, w_f, F32, tn=w_f.shape[1]).reshape(b, s, -1)
    cum = _logsig_cumsum(fg, f_bias)
    o = _fox_attention(qkv, cum, n_heads)
    return _matmul_residual(o.reshape(b * s, -1), w_out.astype(BF16), x.reshape(b * s, d), g1, s)


def _diff_layer(h, x, g1, w_in, lam_p, sub_gain, w_out, rel_bias, bias_tiles, lam_init):
    b, s, d = h.shape
    n_heads = w_in.shape[1] // (6 * HEAD_DIM)
    h2 = h.reshape(b * s, d)
    n_cols = w_in.shape[1]
    qkv = _matmul(h2, w_in.astype(BF16), BF16, col_scale=_query_col_scale(n_cols // 3, n_cols)).reshape(b, s, -1)
    o = _diff_attention(qkv, bias_tiles, rel_bias, lam_p, sub_gain, n_heads, lam_init)
    return _matmul_residual(o.reshape(b * s, -1), w_out.astype(BF16), x.reshape(b * s, d), g1, s)


def _moe_layer(h, x2, g2, w_router, b_router, wgu, b_gu, w_dn, b_dn):
    b, s, d = h.shape
    n = b * s
    ne = wgu.shape[0]
    h2 = h.reshape(n, d)
    idx, wts = _router(h2, w_router, b_router)
    tm = min(MOE_ROW_TILE, n * TOP_K)
    tmc = _row_tile(s, MOE_TOKEN_TILE)
    tile_e, tok_rows, w_rows, pos = _moe_dispatch_plan(idx[:, :TOP_K], wts[:, :TOP_K], ne, tm)
    tok_tiles = jnp.pad(tok_rows.reshape(-1, 1, tm), ((0, 1), (0, 0), (0, 0)))
    w_rows = jnp.broadcast_to(w_rows[:, None], (w_rows.shape[0], LANES))
    bgu = jnp.concatenate([b_gu[:, 0::2], b_gu[:, 1::2]], axis=1)
    y_rows = _moe_experts_sparse(h2, tile_e, tok_tiles, w_rows, wgu, bgu, w_dn.astype(BF16), b_dn, tm)
    pos_tiles = jnp.transpose(pos.reshape(n // tmc, tmc, TOP_K), (0, 2, 1)).reshape(n // tmc, TOP_K * tmc)
    pos_tiles = jnp.pad(pos_tiles, ((0, 1), (0, 0)))[:, None, :]
    return _moe_combine(y_rows, pos_tiles, x2, g2, s, tmc)


def kernel(x, c, rel_bias, norm_mix, norm_ffn, w_mod, b_mod, dsa_w_in, dsa_kv_gain, dsa_w_uk, dsa_w_uv, dsa_w_out, fox_w_in, fox_forget_bias, fox_w_out, diff_w_in, diff_lambda, diff_subln_gain, diff_w_out, w_router, b_router, w_gate_up, b_gate_up, w_down, b_down, final_norm):
    b, s, d = x.shape
    depth = w_mod.shape[0]
    mod = _modulation(c, w_mod, b_mod)
    t_attn = _row_tile(s, 256)
    bias_tiles = _bias_tiles(rel_bias, t_attn)
    ne, ff2 = w_gate_up.shape[1], w_gate_up.shape[3]
    wgu_all = _split_even_odd_cols(w_gate_up.reshape(depth * ne, d, ff2)).reshape(depth, ne, d, ff2)
    ia = ib = ic = 0
    for i in range(depth):
        sh1, sc1, g1, sh2, sc2, g2 = (mod[i, :, k * d:(k + 1) * d] for k in range(6))
        h = _norm_mod(x, norm_mix[i], sc1, sh1)
        kind = i % N_MIXERS
        if kind == 0:
            x2 = _dsa_layer(h, x, g1, dsa_w_in[ia], dsa_kv_gain[ia], dsa_w_uk[ia], dsa_w_uv[ia],
                            dsa_w_out[ia], rel_bias, bias_tiles)
            ia += 1
        elif kind == 1:
            x2 = _fox_layer(h, x, g1, fox_w_in[ib], fox_forget_bias[ib], fox_w_out[ib])
            ib += 1
        else:
            lam_init = 0.8 - 0.6 * math.exp(-0.3 * i)
            x2 = _diff_layer(h, x, g1, diff_w_in[ic], diff_lambda[ic], diff_subln_gain[ic],
                             diff_w_out[ic], rel_bias, bias_tiles, lam_init)
            ic += 1
        x = x2.reshape(b, s, d)
        h = _norm_mod(x, norm_ffn[i], sc2, sh2, out_dtype=F32)
        x = _moe_layer(h, x2, g2, w_router[i], b_router[i], wgu_all[i], b_gate_up[i],
                       w_down[i], b_down[i]).reshape(b, s, d)
    return _norm(x, final_norm, x.dtype)
```

```python
import functools
import math

import numpy as np
import jax
import jax.numpy as jnp
from jax import lax
from jax.experimental import pallas as pl
from jax.experimental.pallas import tpu as pltpu

HEAD_DIM = 128
IDX_DIM = 64
TOPK_MAX = 256
TOP_K = 4
SWIGLU_LIMIT = 7.0
SWIGLU_ALPHA = 1.702
NUM_BUCKETS = 32
MAX_DISTANCE = 128
RMS_EPS = 1e-6
NEG_INF = -1e30
N_MIXERS = 3
LANES = 128
COUNT_ROWS = 64
INT_MIN = -(2 ** 31)
LOG2E = 1.4426950408889634
QUERY_SCALE = HEAD_DIM ** -0.5 * LOG2E

F32 = jnp.float32
BF16 = jnp.bfloat16

VMEM_LIMIT = 56 * 1024 * 1024
HEADS_PER_STEP = 4
MOE_ROW_TILE = 256
MOE_TOKEN_TILE = 128


def _cparams(sem):
    return pltpu.CompilerParams(dimension_semantics=sem, vmem_limit_bytes=VMEM_LIMIT)


def _sigmoid(x):
    return 1.0 / (1.0 + jnp.exp(-x))


def _row_tile(s, want):
    t = min(s, want)
    assert s % t == 0
    return t


def _pad_cols(w, mult=LANES):
    n = w.shape[-1]
    pad = (-n) % mult
    if pad:
        w = jnp.pad(w, [(0, 0)] * (w.ndim - 1) + [(0, pad)])
    return w


def _mod_kernel(c_ref, w_ref, b_ref, o_ref):
    c = c_ref[...]
    cs = (c * _sigmoid(c)).astype(BF16)
    o_ref[0] = jnp.dot(cs, w_ref[0].astype(BF16), preferred_element_type=F32) + b_ref[0]


def _modulation(c, w_mod, b_mod):
    depth, d, n = w_mod.shape
    b = c.shape[0]
    tn = min(n, 1024)
    return pl.pallas_call(
        _mod_kernel,
        grid=(depth, n // tn),
        in_specs=[pl.BlockSpec((b, d), lambda i, j: (0, 0)),
                  pl.BlockSpec((1, d, tn), lambda i, j: (i, 0, j)),
                  pl.BlockSpec((1, 1, tn), lambda i, j: (i, 0, j))],
        out_specs=pl.BlockSpec((1, b, tn), lambda i, j: (i, 0, j)),
        out_shape=jax.ShapeDtypeStruct((depth, b, n), F32),
        compiler_params=_cparams(("parallel", "parallel")),
        name="adaln_modulation",
    )(c, w_mod, b_mod.reshape(depth, 1, n))


def _norm_mod_kernel(x_ref, g_ref, sc_ref, sh_ref, o_ref):
    x = x_ref[0]
    ms = jnp.mean(x * x, axis=-1, keepdims=True)
    y = x * lax.rsqrt(ms + RMS_EPS) * g_ref[...]
    o_ref[0] = (y * (1.0 + sc_ref[0]) + sh_ref[0]).astype(o_ref.dtype)


def _norm_kernel(x_ref, g_ref, o_ref):
    x = x_ref[0]
    ms = jnp.mean(x * x, axis=-1, keepdims=True)
    o_ref[0] = (x * lax.rsqrt(ms + RMS_EPS) * g_ref[...]).astype(o_ref.dtype)


def _norm_mod(x, g, sc, sh, out_dtype=BF16):
    b, s, d = x.shape
    ts = _row_tile(s, 512)
    vec = pl.BlockSpec((1, 1, d), lambda i, j: (i, 0, 0))
    return pl.pallas_call(
        _norm_mod_kernel,
        grid=(b, s // ts),
        in_specs=[pl.BlockSpec((1, ts, d), lambda i, j: (i, j, 0)),
                  pl.BlockSpec((1, d), lambda i, j: (0, 0)), vec, vec],
        out_specs=pl.BlockSpec((1, ts, d), lambda i, j: (i, j, 0)),
        out_shape=jax.ShapeDtypeStruct((b, s, d), out_dtype),
        compiler_params=_cparams(("parallel", "parallel")),
        name="rmsnorm_adaln",
    )(x, g.reshape(1, d), sc.reshape(b, 1, d), sh.reshape(b, 1, d))


def _norm(x, g, out_dtype, col_block=0):
    b, s, _ = x.shape
    d = g.shape[-1]
    ts = _row_tile(s, 512)
    return pl.pallas_call(
        _norm_kernel,
        grid=(b, s // ts),
        in_specs=[pl.BlockSpec((1, ts, d), lambda i, j: (i, j, col_block)),
                  pl.BlockSpec((1, d), lambda i, j: (0, 0))],
        out_specs=pl.BlockSpec((1, ts, d), lambda i, j: (i, j, 0)),
        out_shape=jax.ShapeDtypeStruct((b, s, d), out_dtype),
        compiler_params=_cparams(("parallel", "parallel")),
        name="rmsnorm",
    )(x, g.reshape(1, d))


def _mm_kernel(a_ref, b_ref, o_ref):
    o_ref[...] = jnp.dot(a_ref[...], b_ref[...], preferred_element_type=F32).astype(o_ref.dtype)


def _mm_res_kernel(a_ref, b_ref, x_ref, g_ref, o_ref):
    y = jnp.dot(a_ref[...], b_ref[...], preferred_element_type=F32)
    o_ref[...] = x_ref[...] + g_ref[0] * y


def _mm_scaled_kernel(a_ref, b_ref, s_ref, o_ref):
    y = jnp.dot(a_ref[...], b_ref[...], preferred_element_type=F32)
    o_ref[...] = (y * s_ref[...]).astype(o_ref.dtype)


def _matmul(a, w, out_dtype, tm=1024, tn=512, col_scale=None):
    m, k = a.shape
    n = w.shape[1]
    tm = _row_tile(m, tm)
    tn = _row_tile(n, tn)
    in_specs = [pl.BlockSpec((tm, k), lambda i, j: (i, 0)),
                pl.BlockSpec((k, tn), lambda i, j: (0, j))]
    args = (a, w)
    if col_scale is not None:
        in_specs.append(pl.BlockSpec((1, tn), lambda i, j: (0, j)))
        args += (col_scale.reshape(1, n),)
    return pl.pallas_call(
        _mm_kernel if col_scale is None else _mm_scaled_kernel,
        grid=(m // tm, n // tn),
        in_specs=in_specs,
        out_specs=pl.BlockSpec((tm, tn), lambda i, j: (i, j)),
        out_shape=jax.ShapeDtypeStruct((m, n), out_dtype),
        compiler_params=_cparams(("parallel", "parallel")),
        name="matmul",
    )(*args)


def _mm_t_kernel(w_ref, a_ref, o_ref):
    o_ref[0] = lax.dot_general(w_ref[...], a_ref[...], (((1,), (1,)), ((), ())),
                               preferred_element_type=F32).astype(o_ref.dtype)


def _matmul_transposed(a, w_t, out_dtype, rows_per_batch, tm=1024, tn=512):
    m, k = a.shape
    n = w_t.shape[0]
    tm = _row_tile(rows_per_batch, tm)
    tn = _row_tile(n, tn)
    tiles_per_batch = rows_per_batch // tm
    return pl.pallas_call(
        _mm_t_kernel,
        grid=(m // tm, n // tn),
        in_specs=[pl.BlockSpec((tn, k), lambda i, j: (j, 0)),
                  pl.BlockSpec((tm, k), lambda i, j: (i, 0))],
        out_specs=pl.BlockSpec((1, tn, tm), lambda i, j: (i // tiles_per_batch, j, i % tiles_per_batch)),
        out_shape=jax.ShapeDtypeStruct((m // rows_per_batch, n, rows_per_batch), out_dtype),
        compiler_params=_cparams(("parallel", "parallel")),
        name="matmul_transposed",
    )(w_t, a)


def _query_col_scale(n_query_cols, n_cols):
    return jnp.concatenate([jnp.full((n_query_cols,), QUERY_SCALE, F32), jnp.ones((n_cols - n_query_cols,), F32)])


def _matmul_residual(a, w, x, gate, rows_per_batch, tm=1024, tn=512):
    m, k = a.shape
    n = w.shape[1]
    tm = _row_tile(rows_per_batch, tm)
    tn = _row_tile(n, tn)
    tiles_per_batch = rows_per_batch // tm
    nb = gate.shape[0]
    return pl.pallas_call(
        _mm_res_kernel,
        grid=(m // tm, n // tn),
        in_specs=[pl.BlockSpec((tm, k), lambda i, j: (i, 0)),
                  pl.BlockSpec((k, tn), lambda i, j: (0, j)),
                  pl.BlockSpec((tm, tn), lambda i, j: (i, j)),
                  pl.BlockSpec((1, 1, tn), lambda i, j: (i // tiles_per_batch, 0, j))],
        out_specs=pl.BlockSpec((tm, tn), lambda i, j: (i, j)),
        out_shape=jax.ShapeDtypeStruct((m, n), F32),
        compiler_params=_cparams(("parallel", "parallel")),
        name="matmul_residual",
    )(a, w, x, gate.reshape(nb, 1, n))


def _kq(k, q):
    return lax.dot_general(k, q, (((1,), (1,)), ((), ())), preferred_element_type=F32)


def _causal_tile_mask_t(t):
    r = lax.broadcasted_iota(jnp.int32, (t, t), 0)
    c = lax.broadcasted_iota(jnp.int32, (t, t), 1)
    return r <= c


def _t5_bucket_np(dist):
    n = np.maximum(dist, 0)
    max_exact = NUM_BUCKETS // 2
    nf = np.maximum(n, 1).astype(np.float32)
    large = max_exact + (np.log(nf / np.float32(max_exact)) / np.float32(math.log(MAX_DISTANCE / max_exact))
                         * np.float32(NUM_BUCKETS - max_exact)).astype(np.int32)
    large = np.minimum(large, NUM_BUCKETS - 1)
    return np.where(n < max_exact, n, large).astype(np.int32)


def _bias_tile_kernel(bk_ref, rel_ref, o_ref):
    m = pl.program_id(0)
    bk = bk_ref[...]
    acc = jnp.zeros(bk.shape, F32)
    for b in range(NUM_BUCKETS):
        acc = jnp.where(bk == b, rel_ref[b, m], acc)
    o_ref[0] = acc * LOG2E


def _bias_tiles(rel_bias, t):
    assert t >= MAX_DISTANCE
    n_maps = rel_bias.shape[1]
    kk = np.arange(t)[:, None]
    qq = np.arange(t)[None, :]
    buckets = np.stack([_t5_bucket_np(qq - kk), _t5_bucket_np(t + qq - kk)])
    return pl.pallas_call(
        _bias_tile_kernel,
        grid=(n_maps,),
        in_specs=[pl.BlockSpec((2, t, t), lambda m: (0, 0, 0)),
                  pl.BlockSpec(memory_space=pltpu.SMEM)],
        out_specs=pl.BlockSpec((1, 2, t, t), lambda m: (m, 0, 0, 0)),
        out_shape=jax.ShapeDtypeStruct((n_maps, 2, t, t), F32),
        compiler_params=_cparams(("parallel",)),
        name="t5_bias_tiles",
    )(jnp.asarray(buckets), rel_bias)


def _pipelined_sweep(qi, n_chain, t, logits_fn, bias_fn, values_fn, raw_scr, p_scr, acc_ref, near_tile):
    acc_ref[...] = jnp.zeros_like(acc_ref)
    p_scr[...] = jnp.zeros_like(p_scr)
    first = logits_fn(0)
    for c in range(n_chain):
        raw_scr[c] = first[c]

    def flush_values(kj, state):
        for c in range(n_chain):
            acc_ref[c] = state[c][2] * acc_ref[c] + jnp.dot(values_fn(c, kj), p_scr[c], preferred_element_type=F32)

    def step(kj, state, kind, prefetch):
        slot = kj % 2
        flush_values(jnp.maximum(kj - 1, 0), state)
        new_state = []
        for c in range(n_chain):
            m_old, l_old, _ = state[c]
            u, row_const = bias_fn(c, kj, raw_scr[slot * n_chain + c], kind)
            m_new = jnp.maximum(m_old, jnp.max(u, axis=0, keepdims=True) + row_const)
            alpha = jnp.exp2(m_old - m_new)
            p = jnp.exp2(u + (row_const - m_new))
            p_scr[c] = p.astype(p_scr.dtype)
            new_state.append((m_new, alpha * l_old + jnp.sum(p, axis=0, keepdims=True), alpha))
        if prefetch:
            nxt = logits_fn(kj + 1)
            for c in range(n_chain):
                raw_scr[(1 - slot) * n_chain + c] = nxt[c]
        return tuple(new_state)

    init = (jnp.full((1, t), NEG_INF, F32), jnp.zeros((1, t), F32), jnp.ones((1, t), F32))
    state = (init,) * n_chain
    n_far = jnp.maximum(qi - 1, 0) if near_tile else qi
    state = lax.fori_loop(0, n_far, lambda kj, s: step(kj, s, "far", True), state)
    if near_tile:
        state = lax.fori_loop(n_far, qi, lambda kj, s: step(kj, s, "near", True), state)
    state = step(qi, state, "diag", False)
    flush_values(qi, state)
    return [s[1] for s in state]


def _sweep_scratch(n_chain, dv, t):
    return [pltpu.VMEM((n_chain, dv, t), F32), pltpu.VMEM((2 * n_chain, t, t), F32),
            pltpu.VMEM((n_chain, t, t), BF16)]


def _logsig_cumsum_kernel(f_ref, fb_ref, o_ref, carry_ref):
    @pl.when(pl.program_id(1) == 0)
    def _():
        carry_ref[...] = jnp.zeros_like(carry_ref)

    z = f_ref[0] + fb_ref[...]
    logf = -(jnp.maximum(-z, 0.0) + jnp.log(1.0 + jnp.exp(-jnp.abs(z))))
    t = z.shape[0]
    r = lax.broadcasted_iota(jnp.int32, (t, t), 0)
    c = lax.broadcasted_iota(jnp.int32, (t, t), 1)
    tri = jnp.where(r >= c, 1.0, 0.0).astype(BF16)
    hi = logf.astype(BF16)
    r1 = logf - hi.astype(F32)
    mid = r1.astype(BF16)
    lo = (r1 - mid.astype(F32)).astype(BF16)
    cum = (jnp.dot(tri, hi, preferred_element_type=F32)
           + jnp.dot(tri, mid, preferred_element_type=F32)
           + jnp.dot(tri, lo, preferred_element_type=F32)) + carry_ref[...]
    o_ref[0] = cum * LOG2E
    carry_ref[...] = cum[t - 1:t, :]


def _logsig_cumsum(fg, f_bias):
    b, s, w = fg.shape
    tc = _row_tile(s, 512)
    fb = jnp.pad(f_bias, (0, w - f_bias.shape[0])).reshape(1, w)
    return pl.pallas_call(
        _logsig_cumsum_kernel,
        grid=(b, s // tc),
        in_specs=[pl.BlockSpec((1, tc, w), lambda i, j: (i, j, 0)),
                  pl.BlockSpec((1, w), lambda i, j: (0, 0))],
        out_specs=pl.BlockSpec((1, tc, w), lambda i, j: (i, j, 0)),
        out_shape=jax.ShapeDtypeStruct((b, s, w), F32),
        scratch_shapes=[pltpu.VMEM((1, w), F32)],
        compiler_params=_cparams(("parallel", "arbitrary")),
        name="fox_logsig_cumsum",
    )(fg, fb)


def _fox_kernel(q_ref, k_ref, vt_ref, cq_ref, ck_ref, o_ref, acc_ref, raw_scr, p_scr, *, t, group):
    qi = pl.program_id(2)
    hd = HEAD_DIM
    q = q_ref[0]

    def key_rows(kj):
        return pl.ds(pl.multiple_of(kj * t, t), t)

    def logits_fn(kj):
        k = k_ref[0, key_rows(kj), :]
        return [_kq(k[:, g * hd:(g + 1) * hd], q[:, g * hd:(g + 1) * hd]) for g in range(group)]

    def bias_fn(g, kj, raw, kind):
        ck = jnp.tile(ck_ref[0, g, key_rows(kj), :], (1, t // LANES))
        u = raw - ck
        if kind == "diag":
            u = jnp.where(_causal_tile_mask_t(t), u, NEG_INF)
        return u, cq_ref[0, g]

    def values_fn(g, kj):
        return vt_ref[0, g * hd:(g + 1) * hd, key_rows(kj)]

    denom = _pipelined_sweep(qi, group, t, logits_fn, bias_fn, values_fn, raw_scr, p_scr, acc_ref, near_tile=False)
    o_ref[0] = jnp.concatenate([(acc_ref[g] / denom[g]).T for g in range(group)], axis=1).astype(o_ref.dtype)


def _fox_attention(qk, vt, cum, n_heads, group=HEADS_PER_STEP):
    b, s, _ = qk.shape
    t = _row_tile(s, 256)
    hd = HEAD_DIM
    group = min(group, n_heads)
    w = group * hd
    ng = n_heads // group
    cum_t = jnp.transpose(cum[:, :, :n_heads], (0, 2, 1))
    cum_q = cum_t[:, :, None, :]
    cum_k = jnp.broadcast_to(cum_t[..., None], (b, n_heads, s, LANES))
    return pl.pallas_call(
        functools.partial(_fox_kernel, t=t, group=group),
        grid=(b, ng, s // t),
        in_specs=[pl.BlockSpec((1, t, w), lambda i, h, j: (i, j, h)),
                  pl.BlockSpec((1, s, w), lambda i, h, j: (i, 0, ng + h)),
                  pl.BlockSpec((1, w, s), lambda i, h, j: (i, h, 0)),
                  pl.BlockSpec((1, group, 1, t), lambda i, h, j: (i, h, 0, j)),
                  pl.BlockSpec((1, group, s, LANES), lambda i, h, j: (i, h, 0, 0))],
        out_specs=pl.BlockSpec((1, t, w), lambda i, h, j: (i, j, h)),
        out_shape=jax.ShapeDtypeStruct((b, s, n_heads * hd), BF16),
        scratch_shapes=_sweep_scratch(group, hd, t),
        compiler_params=_cparams(("parallel", "parallel", "arbitrary")),
        name="fox_attention",
    )(qk, qk, vt, cum_q, cum_k)


def _diff_kernel(q_ref, k_ref, vt_ref, bias_ref, rel_ref, lam_ref, gain_ref, o_ref, acc_ref, raw_scr, p_scr,
                 *, t, lam_init, group):
    hg = pl.program_id(1)
    qi = pl.program_id(2)
    q = q_ref[0]
    hd = HEAD_DIM
    n_chain = 2 * group

    def key_rows(kj):
        return pl.ds(pl.multiple_of(kj * t, t), t)

    def logits_fn(kj):
        k = k_ref[0, key_rows(kj), :]
        return [_kq(k[:, c * hd:(c + 1) * hd], q[:, c * hd:(c + 1) * hd]) for c in range(n_chain)]

    def bias_fn(c, kj, raw, kind):
        if kind == "far":
            return raw, rel_ref[NUM_BUCKETS - 1, n_chain * hg + c] * LOG2E
        if kind == "near":
            return raw + bias_ref[c, 1], 0.0
        return jnp.where(_causal_tile_mask_t(t), raw + bias_ref[c, 0], NEG_INF), 0.0

    def values_fn(c, kj):
        return vt_ref[0, (c // 2) * 2 * hd:(c // 2 + 1) * 2 * hd, key_rows(kj)]

    denom = _pipelined_sweep(qi, n_chain, t, logits_fn, bias_fn, values_fn, raw_scr, p_scr, acc_ref, near_tile=True)

    lp = lam_ref[...]
    lam = (jnp.exp(jnp.sum(lp[0:1] * lp[1:2], axis=-1, keepdims=True))
           - jnp.exp(jnp.sum(lp[2:3] * lp[3:4], axis=-1, keepdims=True)) + lam_init)
    outs = []
    for g in range(group):
        o = (acc_ref[2 * g] / denom[2 * g] - lam * (acc_ref[2 * g + 1] / denom[2 * g + 1])).T
        ms = jnp.mean(o * o, axis=-1, keepdims=True)
        outs.append(o * lax.rsqrt(ms + RMS_EPS) * gain_ref[...] * (1.0 - lam_init))
    o_ref[0] = jnp.concatenate(outs, axis=1).astype(o_ref.dtype)


def _diff_attention(qk, vt, bias_tiles, rel_bias, lam_p, sub_gain, n_heads, lam_init, group=HEADS_PER_STEP // 2):
    b, s, _ = qk.shape
    t = bias_tiles.shape[-1]
    group = min(group, n_heads)
    hw = 2 * HEAD_DIM
    w = group * hw
    ng = n_heads // group
    return pl.pallas_call(
        functools.partial(_diff_kernel, t=t, lam_init=lam_init, group=group),
        grid=(b, ng, s // t),
        in_specs=[pl.BlockSpec((1, t, w), lambda i, h, j: (i, j, h)),
                  pl.BlockSpec((1, s, w), lambda i, h, j: (i, 0, ng + h)),
                  pl.BlockSpec((1, w, s), lambda i, h, j: (i, h, 0)),
                  pl.BlockSpec((2 * group, 2, t, t), lambda i, h, j: (h, 0, 0, 0)),
                  pl.BlockSpec(memory_space=pltpu.SMEM),
                  pl.BlockSpec((4, HEAD_DIM), lambda i, h, j: (0, 0)),
                  pl.BlockSpec((1, hw), lambda i, h, j: (0, 0))],
        out_specs=pl.BlockSpec((1, t, w), lambda i, h, j: (i, j, h)),
        out_shape=jax.ShapeDtypeStruct((b, s, n_heads * hw), BF16),
        scratch_shapes=_sweep_scratch(2 * group, hw, t),
        compiler_params=_cparams(("parallel", "parallel", "arbitrary")),
        name="diff_attention",
    )(qk, qk, vt, bias_tiles, rel_bias, lam_p, sub_gain.reshape(1, hw))


def _float_key(x):
    bits = pltpu.bitcast(x, jnp.int32)
    return jnp.where(bits < 0, bits ^ jnp.int32(0x7FFFFFFF), bits)


def _indexer_kernel(q_ref, kx_ref, wt_ref, mask_ref, key_ref, *, t, s, n_heads, n_sel, kc):
    qi = pl.program_id(1)
    qpos = qi * t + lax.broadcasted_iota(jnp.int32, (1, t), 1)
    n_chunks = ((qi + 1) * t + kc - 1) // kc
    wt = wt_ref[0]
    qh = [q_ref[0][:, h * IDX_DIM:(h + 1) * IDX_DIM].astype(BF16) for h in range(n_heads)]

    def kpos_of(c):
        return c * kc + lax.broadcasted_iota(jnp.int32, (kc, 1), 0)

    def rows(c):
        return pl.ds(pl.multiple_of(c * kc, kc), kc)

    def score_chunk(c, carry):
        kk = kx_ref[0, rows(c), :][:, :IDX_DIM].astype(BF16)
        score = jnp.zeros((kc, t), F32)
        for h in range(n_heads):
            d = lax.dot_general(kk, qh[h], (((1,), (1,)), ((), ())), preferred_element_type=F32)
            score = score + wt[h:h + 1, :] * jnp.maximum(d, 0.0)
        score = jnp.where(kpos_of(c) <= qpos, score + 0.0, NEG_INF)
        key_ref[rows(c), :] = _float_key(score)
        return carry

    lax.fori_loop(0, n_chunks, score_chunk, 0)

    def count(pred):
        def body(c, acc):
            hit = jnp.where(pred(key_ref[rows(c), :], kpos_of(c)), 1.0, 0.0)
            return acc + jnp.sum(hit.reshape(kc // COUNT_ROWS, COUNT_ROWS, t), axis=0)
        acc = lax.fori_loop(0, n_chunks, body, jnp.zeros((COUNT_ROWS, t), F32))
        return jnp.sum(acc, axis=0, keepdims=True)

    def value_bit(state):
        it, tau_u, n_at = state
        cand_u = tau_u | jnp.left_shift(jnp.int32(1), 31 - it)
        cand = cand_u ^ jnp.int32(INT_MIN)
        n_cand = count(lambda key, kpos: key >= cand)
        accept = n_cand >= n_sel
        return it + 1, jnp.where(accept, cand_u, tau_u), jnp.where(accept, n_cand, n_at)

    def unresolved(state):
        it, _, n_at = state
        return (it < 32) & (jnp.max(n_at) > n_sel)

    n_all = jnp.broadcast_to((n_chunks * kc).astype(F32), (1, t))
    _, tau_u, n_at = lax.while_loop(unresolved, value_bit, (jnp.int32(0), jnp.zeros((1, t), jnp.int32), n_all))
    tau = tau_u ^ jnp.int32(INT_MIN)

    need = n_sel - count(lambda key, kpos: key > tau)
    n_bits = int(math.log2(s))

    def index_bit(it, bound):
        cand = bound | jnp.left_shift(jnp.int32(1), n_bits - 1 - it)
        below = count(lambda key, kpos: (key == tau) & (kpos < cand))
        return jnp.where(below < need, cand, bound)

    tied = jnp.max(n_at) > n_sel
    bound = lax.cond(tied,
                     lambda: lax.fori_loop(0, n_bits, index_bit, jnp.zeros((1, t), jnp.int32)),
                     lambda: jnp.full((1, t), s - 1, jnp.int32))

    def write_chunk(c, carry):
        key = key_ref[rows(c), :]
        kpos = kpos_of(c)
        sel = ((key > tau) | ((key == tau) & (kpos <= bound))) & (kpos <= qpos)
        mask_ref[0, rows(c), :] = jnp.where(sel, 1, 0).astype(jnp.int8)
        return carry

    def zero_chunk(c, carry):
        mask_ref[0, rows(c), :] = jnp.zeros((kc, t), jnp.int8)
        return carry

    lax.fori_loop(0, n_chunks, write_chunk, 0)
    lax.fori_loop(n_chunks, s // kc, zero_chunk, 0)


def _indexer_mask(rest, w_t, t, n_heads, n_sel, qidx_block, kx_block):
    b, s, _ = rest.shape
    assert s & (s - 1) == 0
    kc = min(s, 512)
    assert kc >= n_sel and kc % t == 0
    wq = n_heads * IDX_DIM
    return pl.pallas_call(
        functools.partial(_indexer_kernel, t=t, s=s, n_heads=n_heads, n_sel=n_sel, kc=kc),
        grid=(b, s // t),
        in_specs=[pl.BlockSpec((1, t, wq), lambda i, j: (i, j, qidx_block)),
                  pl.BlockSpec((1, s, LANES), lambda i, j: (i, 0, kx_block)),
                  pl.BlockSpec((1, n_heads, t), lambda i, j: (i, 0, j))],
        out_specs=pl.BlockSpec((1, s, t), lambda i, j: (i, 0, j)),
        out_shape=jax.ShapeDtypeStruct((b, s, s), jnp.int8),
        scratch_shapes=[pltpu.VMEM((s, t), jnp.int32)],
        compiler_params=_cparams(("parallel", "parallel")),
        name="dsa_indexer_topk_mask",
    )(rest, rest, w_t)


def _dsa_kernel(q_ref, k_ref, vt_ref, bias_ref, mask_ref, rel_ref, o_ref, acc_ref, raw_scr, p_scr, *, t, group):
    hg = pl.program_id(1)
    qi = pl.program_id(2)
    q = q_ref[0]
    hd = HEAD_DIM

    def key_rows(kj):
        return pl.ds(pl.multiple_of(kj * t, t), t)

    def logits_fn(kj):
        k = k_ref[0, key_rows(kj), :]
        return [_kq(k[:, g * hd:(g + 1) * hd], q[:, g * hd:(g + 1) * hd]) for g in range(group)]

    def bias_fn(g, kj, raw, kind):
        keep = mask_ref[0, key_rows(kj), :].astype(F32) > 0.5
        if kind == "far":
            return jnp.where(keep, raw, NEG_INF), rel_ref[NUM_BUCKETS - 1, group * hg + g] * LOG2E
        return jnp.where(keep, raw + bias_ref[g, 1 if kind == "near" else 0], NEG_INF), 0.0

    def values_fn(g, kj):
        return vt_ref[0, g * hd:(g + 1) * hd, key_rows(kj)]

    denom = _pipelined_sweep(qi, group, t, logits_fn, bias_fn, values_fn, raw_scr, p_scr, acc_ref, near_tile=True)
    o_ref[0] = jnp.concatenate([(acc_ref[g] / denom[g]).T for g in range(group)], axis=1).astype(o_ref.dtype)


def _dsa_attention(q, k, vt, mask_t, bias_tiles, rel_bias, n_heads, group=HEADS_PER_STEP):
    b, s, _ = q.shape
    t = bias_tiles.shape[-1]
    hd = HEAD_DIM
    group = min(group, n_heads)
    w = group * hd
    ng = n_heads // group
    return pl.pallas_call(
        functools.partial(_dsa_kernel, t=t, group=group),
        grid=(b, ng, s // t),
        in_specs=[pl.BlockSpec((1, t, w), lambda i, h, j: (i, j, h)),
                  pl.BlockSpec((1, s, w), lambda i, h, j: (i, 0, h)),
                  pl.BlockSpec((1, w, s), lambda i, h, j: (i, h, 0)),
                  pl.BlockSpec((group, 2, t, t), lambda i, h, j: (h, 0, 0, 0)),
                  pl.BlockSpec((1, s, t), lambda i, h, j: (i, 0, j)),
                  pl.BlockSpec(memory_space=pltpu.SMEM)],
        out_specs=pl.BlockSpec((1, t, w), lambda i, h, j: (i, j, h)),
        out_shape=jax.ShapeDtypeStruct((b, s, n_heads * hd), BF16),
        scratch_shapes=_sweep_scratch(group, hd, t),
        compiler_params=_cparams(("parallel", "parallel", "arbitrary")),
        name="dsa_attention",
    )(q, k, vt, bias_tiles, mask_t, rel_bias)


def _router_kernel(h_ref, w_ref, b_ref, idx_ref, wt_ref):
    logits = jnp.dot(h_ref[...].astype(BF16), w_ref[...], preferred_element_type=F32) + b_ref[...]
    lane = lax.broadcasted_iota(jnp.int32, logits.shape, 1).astype(F32)
    top_v, top_i = [], []
    for _ in range(TOP_K):
        mx = jnp.max(logits, axis=-1, keepdims=True)
        first = jnp.min(jnp.where(logits == mx, lane, float(LANES)), axis=-1, keepdims=True)
        top_v.append(mx)
        top_i.append(first)
        logits = jnp.where(lane == first, -jnp.inf, logits)
    e = [jnp.exp(v - top_v[0]) for v in top_v]
    denom = e[0] + e[1] + e[2] + e[3]
    idx = jnp.zeros(logits.shape, F32)
    wts = jnp.zeros(logits.shape, F32)
    for k in range(TOP_K):
        idx = jnp.where(lane == float(k), top_i[k], idx)
        wts = jnp.where(lane == float(k), e[k] / denom, wts)
    idx_ref[...] = idx.astype(jnp.int32)
    wt_ref[...] = wts


def _router(h2, w_router, b_router):
    n, d = h2.shape
    ne = w_router.shape[1]
    tm = _row_tile(n, 1024)
    w = _pad_cols(w_router).astype(BF16)
    bias = jnp.concatenate([b_router, jnp.full((LANES - ne,), NEG_INF, F32)]).reshape(1, LANES)
    out = pl.BlockSpec((tm, LANES), lambda i: (i, 0))
    return pl.pallas_call(
        _router_kernel,
        grid=(n // tm,),
        in_specs=[pl.BlockSpec((tm, d), lambda i: (i, 0)),
                  pl.BlockSpec((d, LANES), lambda i: (0, 0)),
                  pl.BlockSpec((1, LANES), lambda i: (0, 0))],
        out_specs=[out, out],
        out_shape=[jax.ShapeDtypeStruct((n, LANES), jnp.int32), jax.ShapeDtypeStruct((n, LANES), F32)],
        compiler_params=_cparams(("parallel",)),
        name="moe_router_top4",
    )(h2, w, bias)


def _split_even_odd_kernel(w_ref, p_ref, o_ref):
    o_ref[0] = jnp.dot(w_ref[0].astype(BF16), p_ref[...], preferred_element_type=F32).astype(o_ref.dtype)


def _split_even_odd_cols(w):
    e, d, n = w.shape
    f = n // 2
    perm = np.zeros((n, n), np.float32)
    perm[2 * np.arange(f), np.arange(f)] = 1.0
    perm[2 * np.arange(f) + 1, f + np.arange(f)] = 1.0
    td = _row_tile(d, 512)
    return pl.pallas_call(
        _split_even_odd_kernel,
        grid=(e, d // td),
        in_specs=[pl.BlockSpec((1, td, n), lambda i, j: (i, j, 0)),
                  pl.BlockSpec((n, n), lambda i, j: (0, 0))],
        out_specs=pl.BlockSpec((1, td, n), lambda i, j: (i, j, 0)),
        out_shape=jax.ShapeDtypeStruct((e, d, n), BF16),
        compiler_params=_cparams(("parallel", "parallel")),
        name="gate_up_split_columns",
    )(w, jnp.asarray(perm, BF16))


def _expert_ffn(h, wgu, bgu, wd, bd):
    f = wgu.shape[1] // 2
    gu = jnp.dot(h, wgu, preferred_element_type=F32) + bgu
    g = jnp.minimum(gu[:, :f], SWIGLU_LIMIT)
    u = jnp.clip(gu[:, f:], -SWIGLU_LIMIT, SWIGLU_LIMIT)
    act = (u + 1.0) * (g * _sigmoid(g * SWIGLU_ALPHA))
    return jnp.dot(act.astype(BF16), wd, preferred_element_type=F32) + bd


def _row_gather_start(src_hbm, rows_ref, dst, sem, n_rows):
    for r in range(n_rows):
        copy = pltpu.make_async_copy(src_hbm.at[pl.ds(rows_ref[0, 0, r], 1)], dst.at[pl.ds(r, 1)], sem)
        copy.start(priority=r % 2)


def _row_gather_wait(src_hbm, dst, sem):
    pltpu.make_async_copy(src_hbm.at[pl.ds(0, dst.shape[0])], dst, sem).wait()


def _moe_expert_kernel(te_ref, tok_ref, tok_next_ref, wrow_ref, wgu_ref, bgu_ref, wd_ref, bd_ref, h_hbm,
                       o_ref, xbuf, sem, *, tm):
    i = pl.program_id(0)
    slot = i % 2

    @pl.when(i == 0)
    def _():
        _row_gather_start(h_hbm, tok_ref, xbuf.at[0], sem.at[0], tm)

    _row_gather_start(h_hbm, tok_next_ref, xbuf.at[1 - slot], sem.at[1 - slot], tm)
    _row_gather_wait(h_hbm, xbuf.at[slot], sem.at[slot])
    y = _expert_ffn(xbuf[slot].astype(BF16), wgu_ref[0], bgu_ref[0], wd_ref[0], bd_ref[0])
    o_ref[...] = y * jnp.tile(wrow_ref[...], (1, y.shape[1] // LANES))

    @pl.when(i == pl.num_programs(0) - 1)
    def _():
        _row_gather_wait(h_hbm, xbuf.at[1 - slot], sem.at[1 - slot])


def _moe_experts_sparse(h2, tile_expert, tok_tiles, w_rows, wgu, bgu, wd, bd, tm):
    n, d = h2.shape
    ne, _, f2 = wgu.shape
    n_tiles = tile_expert.shape[0]
    exp3 = lambda i, te: (te[i], 0, 0)
    smem_rows = lambda off: pl.BlockSpec((1, 1, tm), lambda i, te: (i + off, 0, 0), memory_space=pltpu.SMEM)
    grid_spec = pltpu.PrefetchScalarGridSpec(
        num_scalar_prefetch=1,
        grid=(n_tiles,),
        in_specs=[smem_rows(0), smem_rows(1),
                  pl.BlockSpec((tm, LANES), lambda i, te: (i, 0)),
                  pl.BlockSpec((1, d, f2), exp3), pl.BlockSpec((1, 1, f2), exp3),
                  pl.BlockSpec((1, f2 // 2, d), exp3), pl.BlockSpec((1, 1, d), exp3),
                  pl.BlockSpec(memory_space=pl.ANY)],
        out_specs=pl.BlockSpec((tm, d), lambda i, te: (i, 0)),
        scratch_shapes=[pltpu.VMEM((2, tm, d), F32), pltpu.SemaphoreType.DMA((2,))],
    )
    return pl.pallas_call(
        functools.partial(_moe_expert_kernel, tm=tm),
        grid_spec=grid_spec,
        out_shape=jax.ShapeDtypeStruct((n_tiles * tm, d), F32),
        compiler_params=_cparams(("arbitrary",)),
        name="moe_experts_sparse",
    )(tile_expert, tok_tiles, tok_tiles, w_rows, wgu, bgu.reshape(ne, 1, f2), wd, bd.reshape(ne, 1, d), h2)


def _moe_combine_kernel(pos_ref, pos_next_ref, x_ref, g2_ref, y_hbm, o_ref, ybuf, sem, *, tm):
    i = pl.program_id(0)
    slot = i % 2
    n_rows = TOP_K * tm

    @pl.when(i == 0)
    def _():
        _row_gather_start(y_hbm, pos_ref, ybuf.at[0], sem.at[0], n_rows)

    _row_gather_start(y_hbm, pos_next_ref, ybuf.at[1 - slot], sem.at[1 - slot], n_rows)
    _row_gather_wait(y_hbm, ybuf.at[slot], sem.at[slot])
    acc = ybuf[slot, 0:tm, :]
    for k in range(1, TOP_K):
        acc = acc + ybuf[slot, k * tm:(k + 1) * tm, :]
    o_ref[...] = x_ref[...] + g2_ref[0] * acc

    @pl.when(i == pl.num_programs(0) - 1)
    def _():
        _row_gather_wait(y_hbm, ybuf.at[1 - slot], sem.at[1 - slot])


def _moe_combine(y_rows, pos_tiles, x2, g2, rows_per_batch, tm):
    n, d = x2.shape
    nb = g2.shape[0]
    tiles_per_batch = rows_per_batch // tm
    smem_rows = lambda off: pl.BlockSpec((1, 1, TOP_K * tm), lambda i: (i + off, 0, 0), memory_space=pltpu.SMEM)
    return pl.pallas_call(
        functools.partial(_moe_combine_kernel, tm=tm),
        grid=(n // tm,),
        in_specs=[smem_rows(0), smem_rows(1),
                  pl.BlockSpec((tm, d), lambda i: (i, 0)),
                  pl.BlockSpec((1, 1, d), lambda i: (i // tiles_per_batch, 0, 0)),
                  pl.BlockSpec(memory_space=pl.ANY)],
        out_specs=pl.BlockSpec((tm, d), lambda i: (i, 0)),
        out_shape=jax.ShapeDtypeStruct((n, d), F32),
        scratch_shapes=[pltpu.VMEM((2, TOP_K * tm, d), F32), pltpu.SemaphoreType.DMA((2,))],
        compiler_params=_cparams(("arbitrary",)),
        name="moe_combine",
    )(pos_tiles, pos_tiles, x2, g2.reshape(nb, 1, d), y_rows)


def _moe_dispatch_plan(idx4, w4, ne, tm):
    n = idx4.shape[0]
    n_pairs = n * TOP_K
    n_rows = n_pairs + ne * tm
    big = 1 << (n_pairs + tm - 1).bit_length()
    unused = jnp.int32(ne * big)
    eid = idx4.reshape(-1)
    experts = jnp.arange(ne, dtype=jnp.int32)
    counts = jnp.sum((eid[:, None] == experts[None, :]).astype(jnp.int32), axis=0)
    n_pad = (-counts) % tm
    j = jnp.arange(tm, dtype=jnp.int32)
    pad_keys = jnp.where(j[None, :] < n_pad[:, None], experts[:, None] * big + n_pairs + j[None, :], unused)
    keys = jnp.concatenate([eid * big + jnp.arange(n_pairs, dtype=jnp.int32), pad_keys.reshape(-1)])
    wvals = jnp.concatenate([w4.reshape(-1), jnp.zeros((ne * tm,), F32)])
    skeys, w_rows = lax.sort((keys, wvals), num_keys=1)
    slot = skeys % big
    is_pair = (skeys < unused) & (slot < n_pairs)
    tok_rows = jnp.where(is_pair, slot // TOP_K, 0)
    tile_e = jnp.minimum(skeys[::tm] // big, ne - 1)
    _, row_of_pair = lax.sort((jnp.where(is_pair, slot, n_rows), jnp.arange(n_rows, dtype=jnp.int32)), num_keys=1)
    pos = row_of_pair[:n_pairs].reshape(n, TOP_K)
    return tile_e, tok_rows, w_rows, pos


def _dsa_layer(h, x, g1, w_in, kv_gain, w_uk, w_uv, w_out, rel_bias, bias_tiles):
    b, s, d = h.shape
    n_heads = d // HEAD_DIM
    rank = kv_gain.shape[0]
    idx_heads = n_heads // 2
    n_sel = min(TOPK_MAX, s // 4)
    o1 = n_heads * HEAD_DIM
    h2 = h.reshape(b * s, d)
    q = _matmul(h2, w_in[:, :o1].astype(BF16), BF16, col_scale=_query_col_scale(o1, o1)).reshape(b, s, o1)
    w_rest = _pad_cols(w_in[:, o1:]).astype(BF16)
    rest = _matmul(h2, w_rest, F32, tn=w_rest.shape[1]).reshape(b, s, -1)
    assert rank % LANES == 0 and (idx_heads * IDX_DIM) == rank
    ckv = _norm(rest, kv_gain, BF16)
    w_k = jnp.transpose(w_uk, (2, 0, 1)).reshape(rank, o1)
    w_v = jnp.transpose(w_uv, (1, 0, 2)).reshape(rank, o1)
    ckv2 = ckv.reshape(b * s, rank)
    k = _matmul(ckv2, w_k.astype(BF16), BF16).reshape(b, s, o1)
    vt = _matmul_transposed(ckv2, w_v.T.astype(BF16), BF16, s)
    t = bias_tiles.shape[-1]
    w_off = 2 * rank + IDX_DIM
    w_t = jnp.transpose(rest[:, :, w_off:w_off + idx_heads], (0, 2, 1))
    mask_t = _indexer_mask(rest, w_t, t, idx_heads, n_sel, qidx_block=1, kx_block=2 * rank // LANES)
    o = _dsa_attention(q, k, vt, mask_t, bias_tiles, rel_bias, n_heads)
    return _matmul_residual(o.reshape(b * s, o1), w_out.astype(BF16), x.reshape(b * s, d), g1, s)


def _fox_layer(h, x, g1, w_in, f_bias, w_out):
    b, s, d = h.shape
    n_heads = f_bias.shape[0]
    hd3 = 3 * n_heads * HEAD_DIM
    h2 = h.reshape(b * s, d)
    hd2 = 2 * hd3 // 3
    qk = _matmul(h2, w_in[:, :hd2].astype(BF16), BF16, col_scale=_query_col_scale(hd2 // 2, hd2)).reshape(b, s, hd2)
    vt = _matmul_transposed(h2, w_in[:, hd2:hd3].T.astype(BF16), BF16, s)
    w_f = _pad_cols(w_in[:, hd3:]).astype(BF16)
    fg = _matmul(h2, w_f, F32, tn=w_f.shape[1]).reshape(b, s, -1)
    cum = _logsig_cumsum(fg, f_bias)
    o = _fox_attention(qk, vt, cum, n_heads)
    return _matmul_residual(o.reshape(b * s, -1), w_out.astype(BF16), x.reshape(b * s, d), g1, s)


def _diff_layer(h, x, g1, w_in, lam_p, sub_gain, w_out, rel_bias, bias_tiles, lam_init):
    b, s, d = h.shape
    n_heads = w_in.shape[1] // (6 * HEAD_DIM)
    h2 = h.reshape(b * s, d)
    n_qk = 2 * w_in.shape[1] // 3
    qk = _matmul(h2, w_in[:, :n_qk].astype(BF16), BF16, col_scale=_query_col_scale(n_qk // 2, n_qk)).reshape(b, s, n_qk)
    vt = _matmul_transposed(h2, w_in[:, n_qk:].T.astype(BF16), BF16, s)
    o = _diff_attention(qk, vt, bias_tiles, rel_bias, lam_p, sub_gain, n_heads, lam_init)
    return _matmul_residual(o.reshape(b * s, -1), w_out.astype(BF16), x.reshape(b * s, d), g1, s)


def _moe_layer(h, x2, g2, w_router, b_router, wgu, b_gu, w_dn, b_dn):
    b, s, d = h.shape
    n = b * s
    ne = wgu.shape[0]
    h2 = h.reshape(n, d)
    idx, wts = _router(h2, w_router, b_router)
    tm = min(MOE_ROW_TILE, n * TOP_K)
    tmc = _row_tile(s, MOE_TOKEN_TILE)
    tile_e, tok_rows, w_rows, pos = _moe_dispatch_plan(idx[:, :TOP_K], wts[:, :TOP_K], ne, tm)
    tok_tiles = jnp.pad(tok_rows.reshape(-1, 1, tm), ((0, 1), (0, 0), (0, 0)))
    w_rows = jnp.broadcast_to(w_rows[:, None], (w_rows.shape[0], LANES))
    bgu = jnp.concatenate([b_gu[:, 0::2], b_gu[:, 1::2]], axis=1)
    y_rows = _moe_experts_sparse(h2, tile_e, tok_tiles, w_rows, wgu, bgu, w_dn.astype(BF16), b_dn, tm)
    pos_tiles = jnp.transpose(pos.reshape(n // tmc, tmc, TOP_K), (0, 2, 1)).reshape(n // tmc, TOP_K * tmc)
    pos_tiles = jnp.pad(pos_tiles, ((0, 1), (0, 0)))[:, None, :]
    return _moe_combine(y_rows, pos_tiles, x2, g2, s, tmc)


def kernel(x, c, rel_bias, norm_mix, norm_ffn, w_mod, b_mod, dsa_w_in, dsa_kv_gain, dsa_w_uk, dsa_w_uv, dsa_w_out, fox_w_in, fox_forget_bias, fox_w_out, diff_w_in, diff_lambda, diff_subln_gain, diff_w_out, w_router, b_router, w_gate_up, b_gate_up, w_down, b_down, final_norm):
    b, s, d = x.shape
    depth = w_mod.shape[0]
    mod = _modulation(c, w_mod, b_mod)
    t_attn = _row_tile(s, 256)
    bias_tiles = _bias_tiles(rel_bias, t_attn)
    ne, ff2 = w_gate_up.shape[1], w_gate_up.shape[3]
    wgu_all = _split_even_odd_cols(w_gate_up.reshape(depth * ne, d, ff2)).reshape(depth, ne, d, ff2)
    ia = ib = ic = 0
    for i in range(depth):
        sh1, sc1, g1, sh2, sc2, g2 = (mod[i, :, k * d:(k + 1) * d] for k in range(6))
        h = _norm_mod(x, norm_mix[i], sc1, sh1)
        kind = i % N_MIXERS
        if kind == 0:
            x2 = _dsa_layer(h, x, g1, dsa_w_in[ia], dsa_kv_gain[ia], dsa_w_uk[ia], dsa_w_uv[ia],
                            dsa_w_out[ia], rel_bias, bias_tiles)
            ia += 1
        elif kind == 1:
            x2 = _fox_layer(h, x, g1, fox_w_in[ib], fox_forget_bias[ib], fox_w_out[ib])
            ib += 1
        else:
            lam_init = 0.8 - 0.6 * math.exp(-0.3 * i)
            x2 = _diff_layer(h, x, g1, diff_w_in[ic], diff_lambda[ic], diff_subln_gain[ic],
                             diff_w_out[ic], rel_bias, bias_tiles, lam_init)
            ic += 1
        x = x2.reshape(b, s, d)
        h = _norm_mod(x, norm_ffn[i], sc2, sh2, out_dtype=F32)
        x = _moe_layer(h, x2, g2, w_router[i], b_router[i], wgu_all[i], b_gate_up[i],
                       w_down[i], b_down[i]).reshape(b, s, d)
    return _norm(x, final_norm, x.dtype)
```

```python
import functools
import math

import numpy as np
import jax
import jax.numpy as jnp
from jax import lax
from jax.experimental import pallas as pl
from jax.experimental.pallas import tpu as pltpu

HEAD_DIM = 128
IDX_DIM = 64
TOPK_MAX = 256
TOP_K = 4
SWIGLU_LIMIT = 7.0
SWIGLU_ALPHA = 1.702
NUM_BUCKETS = 32
MAX_DISTANCE = 128
RMS_EPS = 1e-6
NEG_INF = -1e30
N_MIXERS = 3
LANES = 128
COUNT_ROWS = 64
INT_MIN = -(2 ** 31)
LOG2E = 1.4426950408889634
QUERY_SCALE = HEAD_DIM ** -0.5 * LOG2E

F32 = jnp.float32
BF16 = jnp.bfloat16

VMEM_LIMIT = 56 * 1024 * 1024
HEADS_PER_STEP = 4
MOE_ROW_TILE = 512
MOE_TOKEN_TILE = 128


def _cparams(sem):
    return pltpu.CompilerParams(dimension_semantics=sem, vmem_limit_bytes=VMEM_LIMIT)


def _sigmoid(x):
    return 1.0 / (1.0 + jnp.exp(-x))


def _row_tile(s, want):
    t = min(s, want)
    assert s % t == 0
    return t


def _pad_cols(w, mult=LANES):
    n = w.shape[-1]
    pad = (-n) % mult
    if pad:
        w = jnp.pad(w, [(0, 0)] * (w.ndim - 1) + [(0, pad)])
    return w


def _mod_kernel(c_ref, w_ref, b_ref, o_ref):
    c = c_ref[...]
    cs = (c * _sigmoid(c)).astype(BF16)
    o_ref[0] = jnp.dot(cs, w_ref[0].astype(BF16), preferred_element_type=F32) + b_ref[0]


def _modulation(c, w_mod, b_mod):
    depth, d, n = w_mod.shape
    b = c.shape[0]
    tn = min(n, 1024)
    return pl.pallas_call(
        _mod_kernel,
        grid=(depth, n // tn),
        in_specs=[pl.BlockSpec((b, d), lambda i, j: (0, 0)),
                  pl.BlockSpec((1, d, tn), lambda i, j: (i, 0, j)),
                  pl.BlockSpec((1, 1, tn), lambda i, j: (i, 0, j))],
        out_specs=pl.BlockSpec((1, b, tn), lambda i, j: (i, 0, j)),
        out_shape=jax.ShapeDtypeStruct((depth, b, n), F32),
        compiler_params=_cparams(("parallel", "parallel")),
        name="adaln_modulation",
    )(c, w_mod, b_mod.reshape(depth, 1, n))


def _norm_mod_kernel(x_ref, g_ref, sc_ref, sh_ref, o_ref):
    x = x_ref[0]
    ms = jnp.mean(x * x, axis=-1, keepdims=True)
    y = x * lax.rsqrt(ms + RMS_EPS) * g_ref[...]
    o_ref[0] = (y * (1.0 + sc_ref[0]) + sh_ref[0]).astype(o_ref.dtype)


def _norm_kernel(x_ref, g_ref, o_ref):
    x = x_ref[0]
    ms = jnp.mean(x * x, axis=-1, keepdims=True)
    o_ref[0] = (x * lax.rsqrt(ms + RMS_EPS) * g_ref[...]).astype(o_ref.dtype)


def _norm_mod(x, g, sc, sh, out_dtype=BF16):
    b, s, d = x.shape
    ts = _row_tile(s, 512)
    vec = pl.BlockSpec((1, 1, d), lambda i, j: (i, 0, 0))
    return pl.pallas_call(
        _norm_mod_kernel,
        grid=(b, s // ts),
        in_specs=[pl.BlockSpec((1, ts, d), lambda i, j: (i, j, 0)),
                  pl.BlockSpec((1, d), lambda i, j: (0, 0)), vec, vec],
        out_specs=pl.BlockSpec((1, ts, d), lambda i, j: (i, j, 0)),
        out_shape=jax.ShapeDtypeStruct((b, s, d), out_dtype),
        compiler_params=_cparams(("parallel", "parallel")),
        name="rmsnorm_adaln",
    )(x, g.reshape(1, d), sc.reshape(b, 1, d), sh.reshape(b, 1, d))


def _norm(x, g, out_dtype, col_block=0):
    b, s, _ = x.shape
    d = g.shape[-1]
    ts = _row_tile(s, 512)
    return pl.pallas_call(
        _norm_kernel,
        grid=(b, s // ts),
        in_specs=[pl.BlockSpec((1, ts, d), lambda i, j: (i, j, col_block)),
                  pl.BlockSpec((1, d), lambda i, j: (0, 0))],
        out_specs=pl.BlockSpec((1, ts, d), lambda i, j: (i, j, 0)),
        out_shape=jax.ShapeDtypeStruct((b, s, d), out_dtype),
        compiler_params=_cparams(("parallel", "parallel")),
        name="rmsnorm",
    )(x, g.reshape(1, d))


def _mm_kernel(a_ref, b_ref, o_ref):
    o_ref[...] = jnp.dot(a_ref[...], b_ref[...], preferred_element_type=F32).astype(o_ref.dtype)


def _mm_res_kernel(a_ref, b_ref, x_ref, g_ref, o_ref):
    y = jnp.dot(a_ref[...], b_ref[...], preferred_element_type=F32)
    o_ref[...] = x_ref[...] + g_ref[0] * y


def _mm_scaled_kernel(a_ref, b_ref, s_ref, o_ref):
    y = jnp.dot(a_ref[...], b_ref[...], preferred_element_type=F32)
    o_ref[...] = (y * s_ref[...]).astype(o_ref.dtype)


def _matmul(a, w, out_dtype, tm=1024, tn=512, col_scale=None):
    m, k = a.shape
    n = w.shape[1]
    tm = _row_tile(m, tm)
    tn = _row_tile(n, tn)
    in_specs = [pl.BlockSpec((tm, k), lambda i, j: (i, 0)),
                pl.BlockSpec((k, tn), lambda i, j: (0, j))]
    args = (a, w)
    if col_scale is not None:
        in_specs.append(pl.BlockSpec((1, tn), lambda i, j: (0, j)))
        args += (col_scale.reshape(1, n),)
    return pl.pallas_call(
        _mm_kernel if col_scale is None else _mm_scaled_kernel,
        grid=(m // tm, n // tn),
        in_specs=in_specs,
        out_specs=pl.BlockSpec((tm, tn), lambda i, j: (i, j)),
        out_shape=jax.ShapeDtypeStruct((m, n), out_dtype),
        compiler_params=_cparams(("parallel", "parallel")),
        name="matmul",
    )(*args)


def _mm_t_kernel(w_ref, a_ref, o_ref):
    o_ref[0] = lax.dot_general(w_ref[...], a_ref[...], (((1,), (1,)), ((), ())),
                               preferred_element_type=F32).astype(o_ref.dtype)


def _matmul_transposed(a, w_t, out_dtype, rows_per_batch, tm=1024, tn=512):
    m, k = a.shape
    n = w_t.shape[0]
    tm = _row_tile(rows_per_batch, tm)
    tn = _row_tile(n, tn)
    tiles_per_batch = rows_per_batch // tm
    return pl.pallas_call(
        _mm_t_kernel,
        grid=(m // tm, n // tn),
        in_specs=[pl.BlockSpec((tn, k), lambda i, j: (j, 0)),
                  pl.BlockSpec((tm, k), lambda i, j: (i, 0))],
        out_specs=pl.BlockSpec((1, tn, tm), lambda i, j: (i // tiles_per_batch, j, i % tiles_per_batch)),
        out_shape=jax.ShapeDtypeStruct((m // rows_per_batch, n, rows_per_batch), out_dtype),
        compiler_params=_cparams(("parallel", "parallel")),
        name="matmul_transposed",
    )(w_t, a)


def _query_col_scale(n_query_cols, n_cols):
    return jnp.concatenate([jnp.full((n_query_cols,), QUERY_SCALE, F32), jnp.ones((n_cols - n_query_cols,), F32)])


def _matmul_residual(a, w, x, gate, rows_per_batch, tm=1024, tn=512):
    m, k = a.shape
    n = w.shape[1]
    tm = _row_tile(rows_per_batch, tm)
    tn = _row_tile(n, tn)
    tiles_per_batch = rows_per_batch // tm
    nb = gate.shape[0]
    return pl.pallas_call(
        _mm_res_kernel,
        grid=(m // tm, n // tn),
        in_specs=[pl.BlockSpec((tm, k), lambda i, j: (i, 0)),
                  pl.BlockSpec((k, tn), lambda i, j: (0, j)),
                  pl.BlockSpec((tm, tn), lambda i, j: (i, j)),
                  pl.BlockSpec((1, 1, tn), lambda i, j: (i // tiles_per_batch, 0, j))],
        out_specs=pl.BlockSpec((tm, tn), lambda i, j: (i, j)),
        out_shape=jax.ShapeDtypeStruct((m, n), F32),
        compiler_params=_cparams(("parallel", "parallel")),
        name="matmul_residual",
    )(a, w, x, gate.reshape(nb, 1, n))


def _kq(k, q):
    return lax.dot_general(k, q, (((1,), (1,)), ((), ())), preferred_element_type=F32)


def _causal_tile_mask_t(t):
    r = lax.broadcasted_iota(jnp.int32, (t, t), 0)
    c = lax.broadcasted_iota(jnp.int32, (t, t), 1)
    return r <= c


def _t5_bucket_np(dist):
    n = np.maximum(dist, 0)
    max_exact = NUM_BUCKETS // 2
    nf = np.maximum(n, 1).astype(np.float32)
    large = max_exact + (np.log(nf / np.float32(max_exact)) / np.float32(math.log(MAX_DISTANCE / max_exact))
                         * np.float32(NUM_BUCKETS - max_exact)).astype(np.int32)
    large = np.minimum(large, NUM_BUCKETS - 1)
    return np.where(n < max_exact, n, large).astype(np.int32)


def _bias_tile_kernel(bk_ref, rel_ref, o_ref):
    m = pl.program_id(0)
    bk = bk_ref[...]
    acc = jnp.zeros(bk.shape, F32)
    for b in range(NUM_BUCKETS):
        acc = jnp.where(bk == b, rel_ref[b, m], acc)
    o_ref[0] = acc * LOG2E


def _bias_tiles(rel_bias, t):
    assert t >= MAX_DISTANCE
    n_maps = rel_bias.shape[1]
    kk = np.arange(t)[:, None]
    qq = np.arange(t)[None, :]
    buckets = np.stack([_t5_bucket_np(qq - kk), _t5_bucket_np(t + qq - kk)])
    return pl.pallas_call(
        _bias_tile_kernel,
        grid=(n_maps,),
        in_specs=[pl.BlockSpec((2, t, t), lambda m: (0, 0, 0)),
                  pl.BlockSpec(memory_space=pltpu.SMEM)],
        out_specs=pl.BlockSpec((1, 2, t, t), lambda m: (m, 0, 0, 0)),
        out_shape=jax.ShapeDtypeStruct((n_maps, 2, t, t), F32),
        compiler_params=_cparams(("parallel",)),
        name="t5_bias_tiles",
    )(jnp.asarray(buckets), rel_bias)


def _pipelined_sweep(qi, n_chain, t, logits_fn, bias_fn, values_fn, raw_scr, p_scr, acc_ref, near_tile):
    acc_ref[...] = jnp.zeros_like(acc_ref)
    p_scr[...] = jnp.zeros_like(p_scr)
    first = logits_fn(0)
    for c in range(n_chain):
        raw_scr[c] = first[c]

    def flush_values(kj, state):
        for c in range(n_chain):
            acc_ref[c] = state[c][2] * acc_ref[c] + jnp.dot(values_fn(c, kj), p_scr[c], preferred_element_type=F32)

    def step(kj, state, kind, prefetch):
        slot = kj % 2
        flush_values(jnp.maximum(kj - 1, 0), state)
        new_state = []
        for c in range(n_chain):
            m_old, l_old, _ = state[c]
            u, row_const = bias_fn(c, kj, raw_scr[slot * n_chain + c], kind)
            m_new = jnp.maximum(m_old, jnp.max(u, axis=0, keepdims=True) + row_const)
            alpha = jnp.exp2(m_old - m_new)
            p = jnp.exp2(u + (row_const - m_new))
            p_scr[c] = p.astype(p_scr.dtype)
            new_state.append((m_new, alpha * l_old + jnp.sum(p, axis=0, keepdims=True), alpha))
        if prefetch:
            nxt = logits_fn(kj + 1)
            for c in range(n_chain):
                raw_scr[(1 - slot) * n_chain + c] = nxt[c]
        return tuple(new_state)

    init = (jnp.full((1, t), NEG_INF, F32), jnp.zeros((1, t), F32), jnp.ones((1, t), F32))
    state = (init,) * n_chain
    n_far = jnp.maximum(qi - 1, 0) if near_tile else qi
    state = lax.fori_loop(0, n_far, lambda kj, s: step(kj, s, "far", True), state)
    if near_tile:
        state = lax.fori_loop(n_far, qi, lambda kj, s: step(kj, s, "near", True), state)
    state = step(qi, state, "diag", False)
    flush_values(qi, state)
    return [s[1] for s in state]


def _sweep_scratch(n_chain, dv, t):
    return [pltpu.VMEM((n_chain, dv, t), F32), pltpu.VMEM((2 * n_chain, t, t), F32),
            pltpu.VMEM((n_chain, t, t), BF16)]


def _logsig_cumsum_kernel(f_ref, fb_ref, o_ref, carry_ref):
    @pl.when(pl.program_id(1) == 0)
    def _():
        carry_ref[...] = jnp.zeros_like(carry_ref)

    z = f_ref[0] + fb_ref[...]
    logf = -(jnp.maximum(-z, 0.0) + jnp.log(1.0 + jnp.exp(-jnp.abs(z))))
    t = z.shape[0]
    r = lax.broadcasted_iota(jnp.int32, (t, t), 0)
    c = lax.broadcasted_iota(jnp.int32, (t, t), 1)
    tri = jnp.where(r >= c, 1.0, 0.0).astype(BF16)
    hi = logf.astype(BF16)
    r1 = logf - hi.astype(F32)
    mid = r1.astype(BF16)
    lo = (r1 - mid.astype(F32)).astype(BF16)
    cum = (jnp.dot(tri, hi, preferred_element_type=F32)
           + jnp.dot(tri, mid, preferred_element_type=F32)
           + jnp.dot(tri, lo, preferred_element_type=F32)) + carry_ref[...]
    o_ref[0] = cum * LOG2E
    carry_ref[...] = cum[t - 1:t, :]


def _logsig_cumsum(fg, f_bias):
    b, s, w = fg.shape
    tc = _row_tile(s, 512)
    fb = jnp.pad(f_bias, (0, w - f_bias.shape[0])).reshape(1, w)
    return pl.pallas_call(
        _logsig_cumsum_kernel,
        grid=(b, s // tc),
        in_specs=[pl.BlockSpec((1, tc, w), lambda i, j: (i, j, 0)),
                  pl.BlockSpec((1, w), lambda i, j: (0, 0))],
        out_specs=pl.BlockSpec((1, tc, w), lambda i, j: (i, j, 0)),
        out_shape=jax.ShapeDtypeStruct((b, s, w), F32),
        scratch_shapes=[pltpu.VMEM((1, w), F32)],
        compiler_params=_cparams(("parallel", "arbitrary")),
        name="fox_logsig_cumsum",
    )(fg, fb)


def _fox_kernel(q_ref, k_ref, vt_ref, cq_ref, ck_ref, o_ref, acc_ref, raw_scr, p_scr, *, t, group):
    qi = pl.program_id(2)
    hd = HEAD_DIM
    q = q_ref[0]

    def key_rows(kj):
        return pl.ds(pl.multiple_of(kj * t, t), t)

    def logits_fn(kj):
        k = k_ref[0, key_rows(kj), :]
        return [_kq(k[:, g * hd:(g + 1) * hd], q[:, g * hd:(g + 1) * hd]) for g in range(group)]

    def bias_fn(g, kj, raw, kind):
        ck = jnp.tile(ck_ref[0, g, key_rows(kj), :], (1, t // LANES))
        u = raw - ck
        if kind == "diag":
            u = jnp.where(_causal_tile_mask_t(t), u, NEG_INF)
        return u, cq_ref[0, g]

    def values_fn(g, kj):
        return vt_ref[0, g * hd:(g + 1) * hd, key_rows(kj)]

    denom = _pipelined_sweep(qi, group, t, logits_fn, bias_fn, values_fn, raw_scr, p_scr, acc_ref, near_tile=False)
    o_ref[0] = jnp.concatenate([(acc_ref[g] / denom[g]).T for g in range(group)], axis=1).astype(o_ref.dtype)


def _fox_attention(qk, vt, cum, n_heads, group=HEADS_PER_STEP):
    b, s, _ = qk.shape
    t = _row_tile(s, 256)
    hd = HEAD_DIM
    group = min(group, n_heads)
    w = group * hd
    ng = n_heads // group
    cum_t = jnp.transpose(cum[:, :, :n_heads], (0, 2, 1))
    cum_q = cum_t[:, :, None, :]
    cum_k = jnp.broadcast_to(cum_t[..., None], (b, n_heads, s, LANES))
    return pl.pallas_call(
        functools.partial(_fox_kernel, t=t, group=group),
        grid=(b, ng, s // t),
        in_specs=[pl.BlockSpec((1, t, w), lambda i, h, j: (i, j, h)),
                  pl.BlockSpec((1, s, w), lambda i, h, j: (i, 0, ng + h)),
                  pl.BlockSpec((1, w, s), lambda i, h, j: (i, h, 0)),
                  pl.BlockSpec((1, group, 1, t), lambda i, h, j: (i, h, 0, j)),
                  pl.BlockSpec((1, group, s, LANES), lambda i, h, j: (i, h, 0, 0))],
        out_specs=pl.BlockSpec((1, t, w), lambda i, h, j: (i, j, h)),
        out_shape=jax.ShapeDtypeStruct((b, s, n_heads * hd), BF16),
        scratch_shapes=_sweep_scratch(group, hd, t),
        compiler_params=_cparams(("parallel", "parallel", "arbitrary")),
        name="fox_attention",
    )(qk, qk, vt, cum_q, cum_k)


def _diff_kernel(q_ref, k_ref, vt_ref, bias_ref, rel_ref, lam_ref, gain_ref, o_ref, acc_ref, raw_scr, p_scr,
                 *, t, lam_init, group):
    hg = pl.program_id(1)
    qi = pl.program_id(2)
    q = q_ref[0]
    hd = HEAD_DIM
    n_chain = 2 * group

    def key_rows(kj):
        return pl.ds(pl.multiple_of(kj * t, t), t)

    def logits_fn(kj):
        k = k_ref[0, key_rows(kj), :]
        return [_kq(k[:, c * hd:(c + 1) * hd], q[:, c * hd:(c + 1) * hd]) for c in range(n_chain)]

    def bias_fn(c, kj, raw, kind):
        if kind == "far":
            return raw, rel_ref[NUM_BUCKETS - 1, n_chain * hg + c] * LOG2E
        if kind == "near":
            return raw + bias_ref[c, 1], 0.0
        return jnp.where(_causal_tile_mask_t(t), raw + bias_ref[c, 0], NEG_INF), 0.0

    def values_fn(c, kj):
        return vt_ref[0, (c // 2) * 2 * hd:(c // 2 + 1) * 2 * hd, key_rows(kj)]

    denom = _pipelined_sweep(qi, n_chain, t, logits_fn, bias_fn, values_fn, raw_scr, p_scr, acc_ref, near_tile=True)

    lp = lam_ref[...]
    lam = (jnp.exp(jnp.sum(lp[0:1] * lp[1:2], axis=-1, keepdims=True))
           - jnp.exp(jnp.sum(lp[2:3] * lp[3:4], axis=-1, keepdims=True)) + lam_init)
    outs = []
    for g in range(group):
        o = (acc_ref[2 * g] / denom[2 * g] - lam * (acc_ref[2 * g + 1] / denom[2 * g + 1])).T
        ms = jnp.mean(o * o, axis=-1, keepdims=True)
        outs.append(o * lax.rsqrt(ms + RMS_EPS) * gain_ref[...] * (1.0 - lam_init))
    o_ref[0] = jnp.concatenate(outs, axis=1).astype(o_ref.dtype)


def _diff_attention(qk, vt, bias_tiles, rel_bias, lam_p, sub_gain, n_heads, lam_init, group=HEADS_PER_STEP // 2):
    b, s, _ = qk.shape
    t = bias_tiles.shape[-1]
    group = min(group, n_heads)
    hw = 2 * HEAD_DIM
    w = group * hw
    ng = n_heads // group
    return pl.pallas_call(
        functools.partial(_diff_kernel, t=t, lam_init=lam_init, group=group),
        grid=(b, ng, s // t),
        in_specs=[pl.BlockSpec((1, t, w), lambda i, h, j: (i, j, h)),
                  pl.BlockSpec((1, s, w), lambda i, h, j: (i, 0, ng + h)),
                  pl.BlockSpec((1, w, s), lambda i, h, j: (i, h, 0)),
                  pl.BlockSpec((2 * group, 2, t, t), lambda i, h, j: (h, 0, 0, 0)),
                  pl.BlockSpec(memory_space=pltpu.SMEM),
                  pl.BlockSpec((4, HEAD_DIM), lambda i, h, j: (0, 0)),
                  pl.BlockSpec((1, hw), lambda i, h, j: (0, 0))],
        out_specs=pl.BlockSpec((1, t, w), lambda i, h, j: (i, j, h)),
        out_shape=jax.ShapeDtypeStruct((b, s, n_heads * hw), BF16),
        scratch_shapes=_sweep_scratch(2 * group, hw, t),
        compiler_params=_cparams(("parallel", "parallel", "arbitrary")),
        name="diff_attention",
    )(qk, qk, vt, bias_tiles, rel_bias, lam_p, sub_gain.reshape(1, hw))


def _float_key(x):
    bits = pltpu.bitcast(x, jnp.int32)
    return jnp.where(bits < 0, bits ^ jnp.int32(0x7FFFFFFF), bits)


def _indexer_kernel(q_ref, kx_ref, wt_ref, mask_ref, key_ref, *, t, s, n_heads, n_sel, kc):
    qi = pl.program_id(1)
    qpos = qi * t + lax.broadcasted_iota(jnp.int32, (1, t), 1)
    n_chunks = ((qi + 1) * t + kc - 1) // kc
    wt = wt_ref[0]
    qh = [q_ref[0][:, h * IDX_DIM:(h + 1) * IDX_DIM].astype(BF16) for h in range(n_heads)]

    def kpos_of(c):
        return c * kc + lax.broadcasted_iota(jnp.int32, (kc, 1), 0)

    def rows(c):
        return pl.ds(pl.multiple_of(c * kc, kc), kc)

    def score_chunk(c, carry):
        kk = kx_ref[0, rows(c), :][:, :IDX_DIM].astype(BF16)
        score = jnp.zeros((kc, t), F32)
        for h in range(n_heads):
            d = lax.dot_general(kk, qh[h], (((1,), (1,)), ((), ())), preferred_element_type=F32)
            score = score + wt[h:h + 1, :] * jnp.maximum(d, 0.0)
        score = jnp.where(kpos_of(c) <= qpos, score + 0.0, NEG_INF)
        key_ref[rows(c), :] = _float_key(score)
        return carry

    lax.fori_loop(0, n_chunks, score_chunk, 0)

    def count(pred):
        def body(c, acc):
            hit = jnp.where(pred(key_ref[rows(c), :], kpos_of(c)), 1.0, 0.0)
            return acc + jnp.sum(hit.reshape(kc // COUNT_ROWS, COUNT_ROWS, t), axis=0)
        acc = lax.fori_loop(0, n_chunks, body, jnp.zeros((COUNT_ROWS, t), F32))
        return jnp.sum(acc, axis=0, keepdims=True)

    def value_bit(it, state):
        tau_u, n_at = state
        cand_u = tau_u | jnp.left_shift(jnp.int32(1), 31 - it)
        cand = cand_u ^ jnp.int32(INT_MIN)
        n_cand = count(lambda key, kpos: key >= cand)
        accept = n_cand >= n_sel
        return jnp.where(accept, cand_u, tau_u), jnp.where(accept, n_cand, n_at)

    n_all = jnp.broadcast_to((n_chunks * kc).astype(F32), (1, t))
    tau_u, n_at = lax.fori_loop(0, 32, value_bit, (jnp.zeros((1, t), jnp.int32), n_all))
    tau = tau_u ^ jnp.int32(INT_MIN)

    need = n_sel - count(lambda key, kpos: key > tau)
    n_bits = int(math.log2(s))

    def index_bit(it, bound):
        cand = bound | jnp.left_shift(jnp.int32(1), n_bits - 1 - it)
        below = count(lambda key, kpos: (key == tau) & (kpos < cand))
        return jnp.where(below < need, cand, bound)

    tied = jnp.max(n_at) > n_sel
    bound = lax.cond(tied,
                     lambda: lax.fori_loop(0, n_bits, index_bit, jnp.zeros((1, t), jnp.int32)),
                     lambda: jnp.full((1, t), s - 1, jnp.int32))

    def write_chunk(c, carry):
        key = key_ref[rows(c), :]
        kpos = kpos_of(c)
        sel = ((key > tau) | ((key == tau) & (kpos <= bound))) & (kpos <= qpos)
        mask_ref[0, rows(c), :] = jnp.where(sel, 1, 0).astype(jnp.int8)
        return carry

    def zero_chunk(c, carry):
        mask_ref[0, rows(c), :] = jnp.zeros((kc, t), jnp.int8)
        return carry

    lax.fori_loop(0, n_chunks, write_chunk, 0)
    lax.fori_loop(n_chunks, s // kc, zero_chunk, 0)


def _indexer_mask(rest, w_t, t, n_heads, n_sel, qidx_block, kx_block):
    b, s, _ = rest.shape
    assert s & (s - 1) == 0
    kc = min(s, 512)
    assert kc >= n_sel and kc % t == 0
    wq = n_heads * IDX_DIM
    return pl.pallas_call(
        functools.partial(_indexer_kernel, t=t, s=s, n_heads=n_heads, n_sel=n_sel, kc=kc),
        grid=(b, s // t),
        in_specs=[pl.BlockSpec((1, t, wq), lambda i, j: (i, j, qidx_block)),
                  pl.BlockSpec((1, s, LANES), lambda i, j: (i, 0, kx_block)),
                  pl.BlockSpec((1, n_heads, t), lambda i, j: (i, 0, j))],
        out_specs=pl.BlockSpec((1, s, t), lambda i, j: (i, 0, j)),
        out_shape=jax.ShapeDtypeStruct((b, s, s), jnp.int8),
        scratch_shapes=[pltpu.VMEM((s, t), jnp.int32)],
        compiler_params=_cparams(("parallel", "parallel")),
        name="dsa_indexer_topk_mask",
    )(rest, rest, w_t)


def _dsa_kernel(q_ref, k_ref, vt_ref, bias_ref, mask_ref, rel_ref, o_ref, acc_ref, raw_scr, p_scr, *, t, group):
    hg = pl.program_id(1)
    qi = pl.program_id(2)
    q = q_ref[0]
    hd = HEAD_DIM

    def key_rows(kj):
        return pl.ds(pl.multiple_of(kj * t, t), t)

    def logits_fn(kj):
        k = k_ref[0, key_rows(kj), :]
        return [_kq(k[:, g * hd:(g + 1) * hd], q[:, g * hd:(g + 1) * hd]) for g in range(group)]

    def bias_fn(g, kj, raw, kind):
        keep = mask_ref[0, key_rows(kj), :].astype(F32) > 0.5
        if kind == "far":
            return jnp.where(keep, raw, NEG_INF), rel_ref[NUM_BUCKETS - 1, group * hg + g] * LOG2E
        return jnp.where(keep, raw + bias_ref[g, 1 if kind == "near" else 0], NEG_INF), 0.0

    def values_fn(g, kj):
        return vt_ref[0, g * hd:(g + 1) * hd, key_rows(kj)]

    denom = _pipelined_sweep(qi, group, t, logits_fn, bias_fn, values_fn, raw_scr, p_scr, acc_ref, near_tile=True)
    o_ref[0] = jnp.concatenate([(acc_ref[g] / denom[g]).T for g in range(group)], axis=1).astype(o_ref.dtype)


def _dsa_attention(q, k, vt, mask_t, bias_tiles, rel_bias, n_heads, group=HEADS_PER_STEP):
    b, s, _ = q.shape
    t = bias_tiles.shape[-1]
    hd = HEAD_DIM
    group = min(group, n_heads)
    w = group * hd
    ng = n_heads // group
    return pl.pallas_call(
        functools.partial(_dsa_kernel, t=t, group=group),
        grid=(b, ng, s // t),
        in_specs=[pl.BlockSpec((1, t, w), lambda i, h, j: (i, j, h)),
                  pl.BlockSpec((1, s, w), lambda i, h, j: (i, 0, h)),
                  pl.BlockSpec((1, w, s), lambda i, h, j: (i, h, 0)),
                  pl.BlockSpec((group, 2, t, t), lambda i, h, j: (h, 0, 0, 0)),
                  pl.BlockSpec((1, s, t), lambda i, h, j: (i, 0, j)),
                  pl.BlockSpec(memory_space=pltpu.SMEM)],
        out_specs=pl.BlockSpec((1, t, w), lambda i, h, j: (i, j, h)),
        out_shape=jax.ShapeDtypeStruct((b, s, n_heads * hd), BF16),
        scratch_shapes=_sweep_scratch(group, hd, t),
        compiler_params=_cparams(("parallel", "parallel", "arbitrary")),
        name="dsa_attention",
    )(q, k, vt, bias_tiles, mask_t, rel_bias)


def _router_kernel(h_ref, w_ref, b_ref, idx_ref, wt_ref):
    logits = jnp.dot(h_ref[...].astype(BF16), w_ref[...], preferred_element_type=F32) + b_ref[...]
    lane = lax.broadcasted_iota(jnp.int32, logits.shape, 1).astype(F32)
    top_v, top_i = [], []
    for _ in range(TOP_K):
        mx = jnp.max(logits, axis=-1, keepdims=True)
        first = jnp.min(jnp.where(logits == mx, lane, float(LANES)), axis=-1, keepdims=True)
        top_v.append(mx)
        top_i.append(first)
        logits = jnp.where(lane == first, -jnp.inf, logits)
    e = [jnp.exp(v - top_v[0]) for v in top_v]
    denom = e[0] + e[1] + e[2] + e[3]
    idx = jnp.zeros(logits.shape, F32)
    wts = jnp.zeros(logits.shape, F32)
    for k in range(TOP_K):
        idx = jnp.where(lane == float(k), top_i[k], idx)
        wts = jnp.where(lane == float(k), e[k] / denom, wts)
    idx_ref[...] = idx.astype(jnp.int32)
    wt_ref[...] = wts


def _router(h2, w_router, b_router):
    n, d = h2.shape
    ne = w_router.shape[1]
    tm = _row_tile(n, 1024)
    w = _pad_cols(w_router).astype(BF16)
    bias = jnp.concatenate([b_router, jnp.full((LANES - ne,), NEG_INF, F32)]).reshape(1, LANES)
    out = pl.BlockSpec((tm, LANES), lambda i: (i, 0))
    return pl.pallas_call(
        _router_kernel,
        grid=(n // tm,),
        in_specs=[pl.BlockSpec((tm, d), lambda i: (i, 0)),
                  pl.BlockSpec((d, LANES), lambda i: (0, 0)),
                  pl.BlockSpec((1, LANES), lambda i: (0, 0))],
        out_specs=[out, out],
        out_shape=[jax.ShapeDtypeStruct((n, LANES), jnp.int32), jax.ShapeDtypeStruct((n, LANES), F32)],
        compiler_params=_cparams(("parallel",)),
        name="moe_router_top4",
    )(h2, w, bias)


GU_GROUP = 2 * LANES


def _split_even_odd_kernel(w_ref, p_ref, o_ref):
    for g in range(w_ref.shape[2] // GU_GROUP):
        cols = slice(g * GU_GROUP, (g + 1) * GU_GROUP)
        o_ref[0, :, cols] = jnp.dot(w_ref[0, :, cols].astype(BF16), p_ref[...],
                                    preferred_element_type=F32).astype(o_ref.dtype)


def _split_even_odd_cols(w):
    e, d, n = w.shape
    assert n % GU_GROUP == 0
    half = GU_GROUP // 2
    perm = np.zeros((GU_GROUP, GU_GROUP), np.float32)
    perm[2 * np.arange(half), np.arange(half)] = 1.0
    perm[2 * np.arange(half) + 1, half + np.arange(half)] = 1.0
    td = _row_tile(d, 512)
    return pl.pallas_call(
        _split_even_odd_kernel,
        grid=(e, d // td),
        in_specs=[pl.BlockSpec((1, td, n), lambda i, j: (i, j, 0)),
                  pl.BlockSpec((GU_GROUP, GU_GROUP), lambda i, j: (0, 0))],
        out_specs=pl.BlockSpec((1, td, n), lambda i, j: (i, j, 0)),
        out_shape=jax.ShapeDtypeStruct((e, d, n), BF16),
        compiler_params=_cparams(("parallel", "parallel")),
        name="gate_up_split_columns",
    )(w, jnp.asarray(perm, BF16))


def _split_even_odd_bias(b):
    e, n = b.shape
    return jnp.transpose(b.reshape(e, n // GU_GROUP, GU_GROUP // 2, 2), (0, 1, 3, 2)).reshape(e, n)


def _expert_ffn(h, wgu, bgu, wd, bd):
    gu = jnp.dot(h, wgu, preferred_element_type=F32) + bgu
    half = GU_GROUP // 2
    acts = []
    for g in range(gu.shape[1] // GU_GROUP):
        gate = jnp.minimum(gu[:, g * GU_GROUP:g * GU_GROUP + half], SWIGLU_LIMIT)
        up = jnp.clip(gu[:, g * GU_GROUP + half:(g + 1) * GU_GROUP], -SWIGLU_LIMIT, SWIGLU_LIMIT)
        acts.append(((up + 1.0) * (gate * _sigmoid(gate * SWIGLU_ALPHA))).astype(BF16))
    act = jnp.concatenate(acts, axis=1)
    return jnp.dot(act, wd, preferred_element_type=F32) + bd


def _row_gather_start(src_hbm, rows_ref, dst, sem, n_rows):
    for r in range(n_rows):
        copy = pltpu.make_async_copy(src_hbm.at[pl.ds(rows_ref[0, 0, r], 1)], dst.at[pl.ds(r, 1)], sem)
        copy.start(priority=r % 2)


def _row_gather_wait(src_hbm, dst, sem):
    pltpu.make_async_copy(src_hbm.at[pl.ds(0, dst.shape[0])], dst, sem).wait()


def _moe_expert_kernel(te_ref, tok_ref, tok_next_ref, wrow_ref, wgu_ref, bgu_ref, wd_ref, bd_ref, h_hbm,
                       o_ref, xbuf, sem, *, tm):
    i = pl.program_id(0)
    slot = i % 2

    @pl.when(i == 0)
    def _():
        _row_gather_start(h_hbm, tok_ref, xbuf.at[0], sem.at[0], tm)

    _row_gather_start(h_hbm, tok_next_ref, xbuf.at[1 - slot], sem.at[1 - slot], tm)
    _row_gather_wait(h_hbm, xbuf.at[slot], sem.at[slot])
    y = _expert_ffn(xbuf[slot].astype(BF16), wgu_ref[0], bgu_ref[0], wd_ref[0], bd_ref[0])
    o_ref[...] = y * jnp.tile(wrow_ref[...], (1, y.shape[1] // LANES))

    @pl.when(i == pl.num_programs(0) - 1)
    def _():
        _row_gather_wait(h_hbm, xbuf.at[1 - slot], sem.at[1 - slot])


def _moe_experts_sparse(h2, tile_expert, tok_tiles, w_rows, wgu, bgu, wd, bd, tm):
    n, d = h2.shape
    ne, _, f2 = wgu.shape
    n_tiles = tile_expert.shape[0]
    exp3 = lambda i, te: (te[i], 0, 0)
    smem_rows = lambda off: pl.BlockSpec((1, 1, tm), lambda i, te: (i + off, 0, 0), memory_space=pltpu.SMEM)
    grid_spec = pltpu.PrefetchScalarGridSpec(
        num_scalar_prefetch=1,
        grid=(n_tiles,),
        in_specs=[smem_rows(0), smem_rows(1),
                  pl.BlockSpec((tm, LANES), lambda i, te: (i, 0)),
                  pl.BlockSpec((1, d, f2), exp3), pl.BlockSpec((1, 1, f2), exp3),
                  pl.BlockSpec((1, f2 // 2, d), exp3), pl.BlockSpec((1, 1, d), exp3),
                  pl.BlockSpec(memory_space=pl.ANY)],
        out_specs=pl.BlockSpec((tm, d), lambda i, te: (i, 0)),
        scratch_shapes=[pltpu.VMEM((2, tm, d), F32), pltpu.SemaphoreType.DMA((2,))],
    )
    return pl.pallas_call(
        functools.partial(_moe_expert_kernel, tm=tm),
        grid_spec=grid_spec,
        out_shape=jax.ShapeDtypeStruct((n_tiles * tm, d), F32),
        compiler_params=_cparams(("arbitrary",)),
        name="moe_experts_sparse",
    )(tile_expert, tok_tiles, tok_tiles, w_rows, wgu, bgu.reshape(ne, 1, f2), wd, bd.reshape(ne, 1, d), h2)


def _moe_combine_kernel(pos_ref, pos_next_ref, x_ref, g2_ref, y_hbm, o_ref, ybuf, sem, *, tm):
    i = pl.program_id(0)
    slot = i % 2
    n_rows = TOP_K * tm

    @pl.when(i == 0)
    def _():
        _row_gather_start(y_hbm, pos_ref, ybuf.at[0], sem.at[0], n_rows)

    _row_gather_start(y_hbm, pos_next_ref, ybuf.at[1 - slot], sem.at[1 - slot], n_rows)
    _row_gather_wait(y_hbm, ybuf.at[slot], sem.at[slot])
    acc = ybuf[slot, 0:tm, :]
    for k in range(1, TOP_K):
        acc = acc + ybuf[slot, k * tm:(k + 1) * tm, :]
    o_ref[...] = x_ref[...] + g2_ref[0] * acc

    @pl.when(i == pl.num_programs(0) - 1)
    def _():
        _row_gather_wait(y_hbm, ybuf.at[1 - slot], sem.at[1 - slot])


def _moe_combine(y_rows, pos_tiles, x2, g2, rows_per_batch, tm):
    n, d = x2.shape
    nb = g2.shape[0]
    tiles_per_batch = rows_per_batch // tm
    smem_rows = lambda off: pl.BlockSpec((1, 1, TOP_K * tm), lambda i: (i + off, 0, 0), memory_space=pltpu.SMEM)
    return pl.pallas_call(
        functools.partial(_moe_combine_kernel, tm=tm),
        grid=(n // tm,),
        in_specs=[smem_rows(0), smem_rows(1),
                  pl.BlockSpec((tm, d), lambda i: (i, 0)),
                  pl.BlockSpec((1, 1, d), lambda i: (i // tiles_per_batch, 0, 0)),
                  pl.BlockSpec(memory_space=pl.ANY)],
        out_specs=pl.BlockSpec((tm, d), lambda i: (i, 0)),
        out_shape=jax.ShapeDtypeStruct((n, d), F32),
        scratch_shapes=[pltpu.VMEM((2, TOP_K * tm, d), F32), pltpu.SemaphoreType.DMA((2,))],
        compiler_params=_cparams(("arbitrary",)),
        name="moe_combine",
    )(pos_tiles, pos_tiles, x2, g2.reshape(nb, 1, d), y_rows)


def _moe_dispatch_plan(idx4, w4, ne, tm):
    n = idx4.shape[0]
    n_pairs = n * TOP_K
    n_rows = n_pairs + ne * tm
    big = 1 << (n_pairs + tm - 1).bit_length()
    unused = jnp.int32(ne * big)
    eid = idx4.reshape(-1)
    experts = jnp.arange(ne, dtype=jnp.int32)
    counts = jnp.sum((eid[:, None] == experts[None, :]).astype(jnp.int32), axis=0)
    n_pad = (-counts) % tm
    j = jnp.arange(tm, dtype=jnp.int32)
    pad_keys = jnp.where(j[None, :] < n_pad[:, None], experts[:, None] * big + n_pairs + j[None, :], unused)
    keys = jnp.concatenate([eid * big + jnp.arange(n_pairs, dtype=jnp.int32), pad_keys.reshape(-1)])
    wvals = jnp.concatenate([w4.reshape(-1), jnp.zeros((ne * tm,), F32)])
    skeys, w_rows = lax.sort((keys, wvals), num_keys=1)
    slot = skeys % big
    is_pair = (skeys < unused) & (slot < n_pairs)
    tok_rows = jnp.where(is_pair, slot // TOP_K, 0)
    tile_e = jnp.minimum(skeys[::tm] // big, ne - 1)
    _, row_of_pair = lax.sort((jnp.where(is_pair, slot, n_rows), jnp.arange(n_rows, dtype=jnp.int32)), num_keys=1)
    pos = row_of_pair[:n_pairs].reshape(n, TOP_K)
    return tile_e, tok_rows, w_rows, pos


def _dsa_layer(h, x, g1, w_in, kv_gain, w_uk, w_uv, w_out, rel_bias, bias_tiles):
    b, s, d = h.shape
    n_heads = d // HEAD_DIM
    rank = kv_gain.shape[0]
    idx_heads = n_heads // 2
    n_sel = min(TOPK_MAX, s // 4)
    o1 = n_heads * HEAD_DIM
    h2 = h.reshape(b * s, d)
    q = _matmul(h2, w_in[:, :o1].astype(BF16), BF16, col_scale=_query_col_scale(o1, o1)).reshape(b, s, o1)
    w_rest = _pad_cols(w_in[:, o1:]).astype(BF16)
    rest = _matmul(h2, w_rest, F32, tn=w_rest.shape[1]).reshape(b, s, -1)
    assert rank % LANES == 0 and (idx_heads * IDX_DIM) == rank
    ckv = _norm(rest, kv_gain, BF16)
    w_k = jnp.transpose(w_uk, (2, 0, 1)).reshape(rank, o1)
    w_v = jnp.transpose(w_uv, (1, 0, 2)).reshape(rank, o1)
    ckv2 = ckv.reshape(b * s, rank)
    k = _matmul(ckv2, w_k.astype(BF16), BF16).reshape(b, s, o1)
    vt = _matmul_transposed(ckv2, w_v.T.astype(BF16), BF16, s)
    t = bias_tiles.shape[-1]
    w_off = 2 * rank + IDX_DIM
    w_t = jnp.transpose(rest[:, :, w_off:w_off + idx_heads], (0, 2, 1))
    mask_t = _indexer_mask(rest, w_t, t, idx_heads, n_sel, qidx_block=1, kx_block=2 * rank // LANES)
    o = _dsa_attention(q, k, vt, mask_t, bias_tiles, rel_bias, n_heads)
    return _matmul_residual(o.reshape(b * s, o1), w_out.astype(BF16), x.reshape(b * s, d), g1, s)


def _fox_layer(h, x, g1, w_in, f_bias, w_out):
    b, s, d = h.shape
    n_heads = f_bias.shape[0]
    hd3 = 3 * n_heads * HEAD_DIM
    h2 = h.reshape(b * s, d)
    hd2 = 2 * hd3 // 3
    qk = _matmul(h2, w_in[:, :hd2].astype(BF16), BF16, col_scale=_query_col_scale(hd2 // 2, hd2)).reshape(b, s, hd2)
    vt = _matmul_transposed(h2, w_in[:, hd2:hd3].T.astype(BF16), BF16, s)
    w_f = _pad_cols(w_in[:, hd3:]).astype(BF16)
    fg = _matmul(h2, w_f, F32, tn=w_f.shape[1]).reshape(b, s, -1)
    cum = _logsig_cumsum(fg, f_bias)
    o = _fox_attention(qk, vt, cum, n_heads)
    return _matmul_residual(o.reshape(b * s, -1), w_out.astype(BF16), x.reshape(b * s, d), g1, s)


def _diff_layer(h, x, g1, w_in, lam_p, sub_gain, w_out, rel_bias, bias_tiles, lam_init):
    b, s, d = h.shape
    n_heads = w_in.shape[1] // (6 * HEAD_DIM)
    h2 = h.reshape(b * s, d)
    n_qk = 2 * w_in.shape[1] // 3
    qk = _matmul(h2, w_in[:, :n_qk].astype(BF16), BF16, col_scale=_query_col_scale(n_qk // 2, n_qk)).reshape(b, s, n_qk)
    vt = _matmul_transposed(h2, w_in[:, n_qk:].T.astype(BF16), BF16, s)
    o = _diff_attention(qk, vt, bias_tiles, rel_bias, lam_p, sub_gain, n_heads, lam_init)
    return _matmul_residual(o.reshape(b * s, -1), w_out.astype(BF16), x.reshape(b * s, d), g1, s)


def _moe_layer(h, x2, g2, w_router, b_router, wgu, b_gu, w_dn, b_dn):
    b, s, d = h.shape
    n = b * s
    ne = wgu.shape[0]
    h2 = h.reshape(n, d)
    idx, wts = _router(h2, w_router, b_router)
    tm = min(MOE_ROW_TILE, n * TOP_K)
    tmc = _row_tile(s, MOE_TOKEN_TILE)
    tile_e, tok_rows, w_rows, pos = _moe_dispatch_plan(idx[:, :TOP_K], wts[:, :TOP_K], ne, tm)
    tok_tiles = jnp.pad(tok_rows.reshape(-1, 1, tm), ((0, 1), (0, 0), (0, 0)))
    w_rows = jnp.broadcast_to(w_rows[:, None], (w_rows.shape[0], LANES))
    bgu = _split_even_odd_bias(b_gu)
    y_rows = _moe_experts_sparse(h2, tile_e, tok_tiles, w_rows, wgu, bgu, w_dn.astype(BF16), b_dn, tm)
    pos_tiles = jnp.transpose(pos.reshape(n // tmc, tmc, TOP_K), (0, 2, 1)).reshape(n // tmc, TOP_K * tmc)
    pos_tiles = jnp.pad(pos_tiles, ((0, 1), (0, 0)))[:, None, :]
    return _moe_combine(y_rows, pos_tiles, x2, g2, s, tmc)


def kernel(x, c, rel_bias, norm_mix, norm_ffn, w_mod, b_mod, dsa_w_in, dsa_kv_gain, dsa_w_uk, dsa_w_uv, dsa_w_out, fox_w_in, fox_forget_bias, fox_w_out, diff_w_in, diff_lambda, diff_subln_gain, diff_w_out, w_router, b_router, w_gate_up, b_gate_up, w_down, b_down, final_norm):
    b, s, d = x.shape
    depth = w_mod.shape[0]
    mod = _modulation(c, w_mod, b_mod)
    t_attn = _row_tile(s, 256)
    bias_tiles = _bias_tiles(rel_bias, t_attn)
    ne, ff2 = w_gate_up.shape[1], w_gate_up.shape[3]
    wgu_all = _split_even_odd_cols(w_gate_up.reshape(depth * ne, d, ff2)).reshape(depth, ne, d, ff2)
    ia = ib = ic = 0
    for i in range(depth):
        sh1, sc1, g1, sh2, sc2, g2 = (mod[i, :, k * d:(k + 1) * d] for k in range(6))
        h = _norm_mod(x, norm_mix[i], sc1, sh1)
        kind = i % N_MIXERS
        if kind == 0:
            x2 = _dsa_layer(h, x, g1, dsa_w_in[ia], dsa_kv_gain[ia], dsa_w_uk[ia], dsa_w_uv[ia],
                            dsa_w_out[ia], rel_bias, bias_tiles)
            ia += 1
        elif kind == 1:
            x2 = _fox_layer(h, x, g1, fox_w_in[ib], fox_forget_bias[ib], fox_w_out[ib])
            ib += 1
        else:
            lam_init = 0.8 - 0.6 * math.exp(-0.3 * i)
            x2 = _diff_layer(h, x, g1, diff_w_in[ic], diff_lambda[ic], diff_subln_gain[ic],
                             diff_w_out[ic], rel_bias, bias_tiles, lam_init)
            ic += 1
        x = x2.reshape(b, s, d)
        h = _norm_mod(x, norm_ffn[i], sc2, sh2, out_dtype=F32)
        x = _moe_layer(h, x2, g2, w_router[i], b_router[i], wgu_all[i], b_gate_up[i],
                       w_down[i], b_down[i]).reshape(b, s, d)
    return _norm(x, final_norm, x.dtype)
```

```python
import functools
import math

import numpy as np
import jax
import jax.numpy as jnp
from jax import lax
from jax.experimental import pallas as pl
from jax.experimental.pallas import tpu as pltpu

HEAD_DIM = 128
IDX_DIM = 64
TOPK_MAX = 256
TOP_K = 4
SWIGLU_LIMIT = 7.0
SWIGLU_ALPHA = 1.702
NUM_BUCKETS = 32
MAX_DISTANCE = 128
RMS_EPS = 1e-6
NEG_INF = -1e30
N_MIXERS = 3
LANES = 128
COUNT_ROWS = 64
INT_MIN = -(2 ** 31)
LOG2E = 1.4426950408889634
QUERY_SCALE = HEAD_DIM ** -0.5 * LOG2E

F32 = jnp.float32
BF16 = jnp.bfloat16

VMEM_LIMIT = 56 * 1024 * 1024
HEADS_PER_STEP = 4
MOE_ROW_TILE = 512
MOE_TOKEN_TILE = 128


def _cparams(sem):
    return pltpu.CompilerParams(dimension_semantics=sem, vmem_limit_bytes=VMEM_LIMIT)


def _sigmoid(x):
    return 1.0 / (1.0 + jnp.exp(-x))


def _row_tile(s, want):
    t = min(s, want)
    assert s % t == 0
    return t


def _pad_cols(w, mult=LANES):
    n = w.shape[-1]
    pad = (-n) % mult
    if pad:
        w = jnp.pad(w, [(0, 0)] * (w.ndim - 1) + [(0, pad)])
    return w


def _mod_kernel(c_ref, w_ref, b_ref, o_ref):
    c = c_ref[...]
    cs = (c * _sigmoid(c)).astype(BF16)
    o_ref[0] = jnp.dot(cs, w_ref[0].astype(BF16), preferred_element_type=F32) + b_ref[0]


def _modulation(c, w_mod, b_mod):
    depth, d, n = w_mod.shape
    b = c.shape[0]
    tn = min(n, 1024)
    return pl.pallas_call(
        _mod_kernel,
        grid=(depth, n // tn),
        in_specs=[pl.BlockSpec((b, d), lambda i, j: (0, 0)),
                  pl.BlockSpec((1, d, tn), lambda i, j: (i, 0, j)),
                  pl.BlockSpec((1, 1, tn), lambda i, j: (i, 0, j))],
        out_specs=pl.BlockSpec((1, b, tn), lambda i, j: (i, 0, j)),
        out_shape=jax.ShapeDtypeStruct((depth, b, n), F32),
        compiler_params=_cparams(("parallel", "parallel")),
        name="adaln_modulation",
    )(c, w_mod, b_mod.reshape(depth, 1, n))


def _norm_mod_kernel(x_ref, g_ref, sc_ref, sh_ref, o_ref):
    x = x_ref[0]
    ms = jnp.mean(x * x, axis=-1, keepdims=True)
    y = x * lax.rsqrt(ms + RMS_EPS) * g_ref[...]
    o_ref[0] = (y * (1.0 + sc_ref[0]) + sh_ref[0]).astype(o_ref.dtype)


def _norm_kernel(x_ref, g_ref, o_ref):
    x = x_ref[0]
    ms = jnp.mean(x * x, axis=-1, keepdims=True)
    o_ref[0] = (x * lax.rsqrt(ms + RMS_EPS) * g_ref[...]).astype(o_ref.dtype)


def _norm_mod(x, g, sc, sh, out_dtype=BF16):
    b, s, d = x.shape
    ts = _row_tile(s, 512)
    vec = pl.BlockSpec((1, 1, d), lambda i, j: (i, 0, 0))
    return pl.pallas_call(
        _norm_mod_kernel,
        grid=(b, s // ts),
        in_specs=[pl.BlockSpec((1, ts, d), lambda i, j: (i, j, 0)),
                  pl.BlockSpec((1, d), lambda i, j: (0, 0)), vec, vec],
        out_specs=pl.BlockSpec((1, ts, d), lambda i, j: (i, j, 0)),
        out_shape=jax.ShapeDtypeStruct((b, s, d), out_dtype),
        compiler_params=_cparams(("parallel", "parallel")),
        name="rmsnorm_adaln",
    )(x, g.reshape(1, d), sc.reshape(b, 1, d), sh.reshape(b, 1, d))


def _norm(x, g, out_dtype, col_block=0):
    b, s, _ = x.shape
    d = g.shape[-1]
    ts = _row_tile(s, 512)
    return pl.pallas_call(
        _norm_kernel,
        grid=(b, s // ts),
        in_specs=[pl.BlockSpec((1, ts, d), lambda i, j: (i, j, col_block)),
                  pl.BlockSpec((1, d), lambda i, j: (0, 0))],
        out_specs=pl.BlockSpec((1, ts, d), lambda i, j: (i, j, 0)),
        out_shape=jax.ShapeDtypeStruct((b, s, d), out_dtype),
        compiler_params=_cparams(("parallel", "parallel")),
        name="rmsnorm",
    )(x, g.reshape(1, d))


def _mm_kernel(a_ref, b_ref, o_ref):
    o_ref[...] = jnp.dot(a_ref[...], b_ref[...], preferred_element_type=F32).astype(o_ref.dtype)


def _mm_res_kernel(a_ref, b_ref, x_ref, g_ref, o_ref):
    y = jnp.dot(a_ref[...], b_ref[...], preferred_element_type=F32)
    o_ref[...] = x_ref[...] + g_ref[0] * y


def _mm_scaled_kernel(a_ref, b_ref, s_ref, o_ref):
    y = jnp.dot(a_ref[...], b_ref[...], preferred_element_type=F32)
    o_ref[...] = (y * s_ref[...]).astype(o_ref.dtype)


def _matmul(a, w, out_dtype, tm=1024, tn=512, col_scale=None):
    m, k = a.shape
    n = w.shape[1]
    tm = _row_tile(m, tm)
    tn = _row_tile(n, tn)
    in_specs = [pl.BlockSpec((tm, k), lambda i, j: (i, 0)),
                pl.BlockSpec((k, tn), lambda i, j: (0, j))]
    args = (a, w)
    if col_scale is not None:
        in_specs.append(pl.BlockSpec((1, tn), lambda i, j: (0, j)))
        args += (col_scale.reshape(1, n),)
    return pl.pallas_call(
        _mm_kernel if col_scale is None else _mm_scaled_kernel,
        grid=(m // tm, n // tn),
        in_specs=in_specs,
        out_specs=pl.BlockSpec((tm, tn), lambda i, j: (i, j)),
        out_shape=jax.ShapeDtypeStruct((m, n), out_dtype),
        compiler_params=_cparams(("parallel", "parallel")),
        name="matmul",
    )(*args)


def _mm_t_kernel(w_ref, a_ref, o_ref):
    o_ref[0] = lax.dot_general(w_ref[...], a_ref[...], (((1,), (1,)), ((), ())),
                               preferred_element_type=F32).astype(o_ref.dtype)


def _matmul_transposed(a, w_t, out_dtype, rows_per_batch, tm=1024, tn=512):
    m, k = a.shape
    n = w_t.shape[0]
    tm = _row_tile(rows_per_batch, tm)
    tn = _row_tile(n, tn)
    tiles_per_batch = rows_per_batch // tm
    return pl.pallas_call(
        _mm_t_kernel,
        grid=(m // tm, n // tn),
        in_specs=[pl.BlockSpec((tn, k), lambda i, j: (j, 0)),
                  pl.BlockSpec((tm, k), lambda i, j: (i, 0))],
        out_specs=pl.BlockSpec((1, tn, tm), lambda i, j: (i // tiles_per_batch, j, i % tiles_per_batch)),
        out_shape=jax.ShapeDtypeStruct((m // rows_per_batch, n, rows_per_batch), out_dtype),
        compiler_params=_cparams(("parallel", "parallel")),
        name="matmul_transposed",
    )(w_t, a)


def _query_col_scale(n_query_cols, n_cols):
    return jnp.concatenate([jnp.full((n_query_cols,), QUERY_SCALE, F32), jnp.ones((n_cols - n_query_cols,), F32)])


def _matmul_residual(a, w, x, gate, rows_per_batch, tm=1024, tn=512):
    m, k = a.shape
    n = w.shape[1]
    tm = _row_tile(rows_per_batch, tm)
    tn = _row_tile(n, tn)
    tiles_per_batch = rows_per_batch // tm
    nb = gate.shape[0]
    return pl.pallas_call(
        _mm_res_kernel,
        grid=(m // tm, n // tn),
        in_specs=[pl.BlockSpec((tm, k), lambda i, j: (i, 0)),
                  pl.BlockSpec((k, tn), lambda i, j: (0, j)),
                  pl.BlockSpec((tm, tn), lambda i, j: (i, j)),
                  pl.BlockSpec((1, 1, tn), lambda i, j: (i // tiles_per_batch, 0, j))],
        out_specs=pl.BlockSpec((tm, tn), lambda i, j: (i, j)),
        out_shape=jax.ShapeDtypeStruct((m, n), F32),
        compiler_params=_cparams(("parallel", "parallel")),
        name="matmul_residual",
    )(a, w, x, gate.reshape(nb, 1, n))


def _kq(k, q):
    return lax.dot_general(k, q, (((1,), (1,)), ((), ())), preferred_element_type=F32)


def _causal_tile_mask_t(t):
    r = lax.broadcasted_iota(jnp.int32, (t, t), 0)
    c = lax.broadcasted_iota(jnp.int32, (t, t), 1)
    return r <= c


def _t5_bucket_np(dist):
    n = np.maximum(dist, 0)
    max_exact = NUM_BUCKETS // 2
    nf = np.maximum(n, 1).astype(np.float32)
    large = max_exact + (np.log(nf / np.float32(max_exact)) / np.float32(math.log(MAX_DISTANCE / max_exact))
                         * np.float32(NUM_BUCKETS - max_exact)).astype(np.int32)
    large = np.minimum(large, NUM_BUCKETS - 1)
    return np.where(n < max_exact, n, large).astype(np.int32)


def _bias_tile_kernel(bk_ref, rel_ref, o_ref):
    m = pl.program_id(0)
    bk = bk_ref[...]
    acc = jnp.zeros(bk.shape, F32)
    for b in range(NUM_BUCKETS):
        acc = jnp.where(bk == b, rel_ref[b, m], acc)
    o_ref[0] = acc * LOG2E


def _bias_tiles(rel_bias, t):
    assert t >= MAX_DISTANCE
    n_maps = rel_bias.shape[1]
    kk = np.arange(t)[:, None]
    qq = np.arange(t)[None, :]
    buckets = np.stack([_t5_bucket_np(qq - kk), _t5_bucket_np(t + qq - kk)])
    return pl.pallas_call(
        _bias_tile_kernel,
        grid=(n_maps,),
        in_specs=[pl.BlockSpec((2, t, t), lambda m: (0, 0, 0)),
                  pl.BlockSpec(memory_space=pltpu.SMEM)],
        out_specs=pl.BlockSpec((1, 2, t, t), lambda m: (m, 0, 0, 0)),
        out_shape=jax.ShapeDtypeStruct((n_maps, 2, t, t), F32),
        compiler_params=_cparams(("parallel",)),
        name="t5_bias_tiles",
    )(jnp.asarray(buckets), rel_bias)


def _pipelined_sweep(qi, n_chain, t, logits_fn, bias_fn, values_fn, raw_scr, p_scr, acc_ref, near_tile):
    acc_ref[...] = jnp.zeros_like(acc_ref)
    p_scr[...] = jnp.zeros_like(p_scr)
    first = logits_fn(0)
    for c in range(n_chain):
        raw_scr[c] = first[c]

    def flush_values(kj, state):
        for c in range(n_chain):
            acc_ref[c] = state[c][2] * acc_ref[c] + jnp.dot(values_fn(c, kj), p_scr[c], preferred_element_type=F32)

    def step(kj, state, kind, prefetch):
        slot = kj % 2
        flush_values(jnp.maximum(kj - 1, 0), state)
        new_state = []
        for c in range(n_chain):
            m_old, l_old, _ = state[c]
            u, row_const = bias_fn(c, kj, raw_scr[slot * n_chain + c], kind)
            m_new = jnp.maximum(m_old, jnp.max(u, axis=0, keepdims=True) + row_const)
            alpha = jnp.exp2(m_old - m_new)
            p = jnp.exp2(u + (row_const - m_new))
            p_scr[c] = p.astype(p_scr.dtype)
            new_state.append((m_new, alpha * l_old + jnp.sum(p, axis=0, keepdims=True), alpha))
        if prefetch:
            nxt = logits_fn(kj + 1)
            for c in range(n_chain):
                raw_scr[(1 - slot) * n_chain + c] = nxt[c]
        return tuple(new_state)

    init = (jnp.full((1, t), NEG_INF, F32), jnp.zeros((1, t), F32), jnp.ones((1, t), F32))
    state = (init,) * n_chain
    n_far = jnp.maximum(qi - 1, 0) if near_tile else qi
    state = lax.fori_loop(0, n_far, lambda kj, s: step(kj, s, "far", True), state)
    if near_tile:
        state = lax.fori_loop(n_far, qi, lambda kj, s: step(kj, s, "near", True), state)
    state = step(qi, state, "diag", False)
    flush_values(qi, state)
    return [s[1] for s in state]


def _sweep_scratch(n_chain, dv, t):
    return [pltpu.VMEM((n_chain, dv, t), F32), pltpu.VMEM((2 * n_chain, t, t), F32),
            pltpu.VMEM((n_chain, t, t), BF16)]


def _logsig_cumsum_kernel(f_ref, fb_ref, o_ref, carry_ref):
    @pl.when(pl.program_id(1) == 0)
    def _():
        carry_ref[...] = jnp.zeros_like(carry_ref)

    z = f_ref[0] + fb_ref[...]
    logf = -(jnp.maximum(-z, 0.0) + jnp.log(1.0 + jnp.exp(-jnp.abs(z))))
    t = z.shape[0]
    r = lax.broadcasted_iota(jnp.int32, (t, t), 0)
    c = lax.broadcasted_iota(jnp.int32, (t, t), 1)
    tri = jnp.where(r >= c, 1.0, 0.0).astype(BF16)
    hi = logf.astype(BF16)
    r1 = logf - hi.astype(F32)
    mid = r1.astype(BF16)
    lo = (r1 - mid.astype(F32)).astype(BF16)
    cum = (jnp.dot(tri, hi, preferred_element_type=F32)
           + jnp.dot(tri, mid, preferred_element_type=F32)
           + jnp.dot(tri, lo, preferred_element_type=F32)) + carry_ref[...]
    o_ref[0] = cum * LOG2E
    carry_ref[...] = cum[t - 1:t, :]


def _logsig_cumsum(fg, f_bias):
    b, s, w = fg.shape
    tc = _row_tile(s, 512)
    fb = jnp.pad(f_bias, (0, w - f_bias.shape[0])).reshape(1, w)
    return pl.pallas_call(
        _logsig_cumsum_kernel,
        grid=(b, s // tc),
        in_specs=[pl.BlockSpec((1, tc, w), lambda i, j: (i, j, 0)),
                  pl.BlockSpec((1, w), lambda i, j: (0, 0))],
        out_specs=pl.BlockSpec((1, tc, w), lambda i, j: (i, j, 0)),
        out_shape=jax.ShapeDtypeStruct((b, s, w), F32),
        scratch_shapes=[pltpu.VMEM((1, w), F32)],
        compiler_params=_cparams(("parallel", "arbitrary")),
        name="fox_logsig_cumsum",
    )(fg, fb)


def _fox_kernel(q_ref, k_ref, vt_ref, cq_ref, ck_ref, o_ref, acc_ref, raw_scr, p_scr, *, t, group):
    qi = pl.program_id(2)
    hd = HEAD_DIM
    q = q_ref[0]

    def key_rows(kj):
        return pl.ds(pl.multiple_of(kj * t, t), t)

    def logits_fn(kj):
        k = k_ref[0, key_rows(kj), :]
        return [_kq(k[:, g * hd:(g + 1) * hd], q[:, g * hd:(g + 1) * hd]) for g in range(group)]

    def bias_fn(g, kj, raw, kind):
        ck = jnp.tile(ck_ref[0, g, key_rows(kj), :], (1, t // LANES))
        u = raw - ck
        if kind == "diag":
            u = jnp.where(_causal_tile_mask_t(t), u, NEG_INF)
        return u, cq_ref[0, g]

    def values_fn(g, kj):
        return vt_ref[0, g * hd:(g + 1) * hd, key_rows(kj)]

    denom = _pipelined_sweep(qi, group, t, logits_fn, bias_fn, values_fn, raw_scr, p_scr, acc_ref, near_tile=False)
    o_ref[0] = jnp.concatenate([(acc_ref[g] / denom[g]).T for g in range(group)], axis=1).astype(o_ref.dtype)


def _fox_attention(qk, vt, cum, n_heads, group=HEADS_PER_STEP):
    b, s, _ = qk.shape
    t = _row_tile(s, 256)
    hd = HEAD_DIM
    group = min(group, n_heads)
    w = group * hd
    ng = n_heads // group
    cum_t = jnp.transpose(cum[:, :, :n_heads], (0, 2, 1))
    cum_q = cum_t[:, :, None, :]
    cum_k = jnp.broadcast_to(cum_t[..., None], (b, n_heads, s, LANES))
    return pl.pallas_call(
        functools.partial(_fox_kernel, t=t, group=group),
        grid=(b, ng, s // t),
        in_specs=[pl.BlockSpec((1, t, w), lambda i, h, j: (i, j, h)),
                  pl.BlockSpec((1, s, w), lambda i, h, j: (i, 0, ng + h)),
                  pl.BlockSpec((1, w, s), lambda i, h, j: (i, h, 0)),
                  pl.BlockSpec((1, group, 1, t), lambda i, h, j: (i, h, 0, j)),
                  pl.BlockSpec((1, group, s, LANES), lambda i, h, j: (i, h, 0, 0))],
        out_specs=pl.BlockSpec((1, t, w), lambda i, h, j: (i, j, h)),
        out_shape=jax.ShapeDtypeStruct((b, s, n_heads * hd), BF16),
        scratch_shapes=_sweep_scratch(group, hd, t),
        compiler_params=_cparams(("parallel", "parallel", "arbitrary")),
        name="fox_attention",
    )(qk, qk, vt, cum_q, cum_k)


def _diff_kernel(q_ref, k_ref, vt_ref, bias_ref, rel_ref, lam_ref, gain_ref, o_ref, acc_ref, raw_scr, p_scr,
                 *, t, lam_init, group):
    hg = pl.program_id(1)
    qi = pl.program_id(2)
    q = q_ref[0]
    hd = HEAD_DIM
    n_chain = 2 * group

    def key_rows(kj):
        return pl.ds(pl.multiple_of(kj * t, t), t)

    def logits_fn(kj):
        k = k_ref[0, key_rows(kj), :]
        return [_kq(k[:, c * hd:(c + 1) * hd], q[:, c * hd:(c + 1) * hd]) for c in range(n_chain)]

    def bias_fn(c, kj, raw, kind):
        if kind == "far":
            return raw, rel_ref[NUM_BUCKETS - 1, n_chain * hg + c] * LOG2E
        if kind == "near":
            return raw + bias_ref[c, 1], 0.0
        return jnp.where(_causal_tile_mask_t(t), raw + bias_ref[c, 0], NEG_INF), 0.0

    def values_fn(c, kj):
        return vt_ref[0, (c // 2) * 2 * hd:(c // 2 + 1) * 2 * hd, key_rows(kj)]

    denom = _pipelined_sweep(qi, n_chain, t, logits_fn, bias_fn, values_fn, raw_scr, p_scr, acc_ref, near_tile=True)

    lp = lam_ref[...]
    lam = (jnp.exp(jnp.sum(lp[0:1] * lp[1:2], axis=-1, keepdims=True))
           - jnp.exp(jnp.sum(lp[2:3] * lp[3:4], axis=-1, keepdims=True)) + lam_init)
    outs = []
    for g in range(group):
        o = (acc_ref[2 * g] / denom[2 * g] - lam * (acc_ref[2 * g + 1] / denom[2 * g + 1])).T
        ms = jnp.mean(o * o, axis=-1, keepdims=True)
        outs.append(o * lax.rsqrt(ms + RMS_EPS) * gain_ref[...] * (1.0 - lam_init))
    o_ref[0] = jnp.concatenate(outs, axis=1).astype(o_ref.dtype)


def _diff_attention(qk, vt, bias_tiles, rel_bias, lam_p, sub_gain, n_heads, lam_init, group=HEADS_PER_STEP // 2):
    b, s, _ = qk.shape
    t = bias_tiles.shape[-1]
    group = min(group, n_heads)
    hw = 2 * HEAD_DIM
    w = group * hw
    ng = n_heads // group
    return pl.pallas_call(
        functools.partial(_diff_kernel, t=t, lam_init=lam_init, group=group),
        grid=(b, ng, s // t),
        in_specs=[pl.BlockSpec((1, t, w), lambda i, h, j: (i, j, h)),
                  pl.BlockSpec((1, s, w), lambda i, h, j: (i, 0, ng + h)),
                  pl.BlockSpec((1, w, s), lambda i, h, j: (i, h, 0)),
                  pl.BlockSpec((2 * group, 2, t, t), lambda i, h, j: (h, 0, 0, 0)),
                  pl.BlockSpec(memory_space=pltpu.SMEM),
                  pl.BlockSpec((4, HEAD_DIM), lambda i, h, j: (0, 0)),
                  pl.BlockSpec((1, hw), lambda i, h, j: (0, 0))],
        out_specs=pl.BlockSpec((1, t, w), lambda i, h, j: (i, j, h)),
        out_shape=jax.ShapeDtypeStruct((b, s, n_heads * hw), BF16),
        scratch_shapes=_sweep_scratch(2 * group, hw, t),
        compiler_params=_cparams(("parallel", "parallel", "arbitrary")),
        name="diff_attention",
    )(qk, qk, vt, bias_tiles, rel_bias, lam_p, sub_gain.reshape(1, hw))


def _float_key(x):
    bits = pltpu.bitcast(x, jnp.int32)
    return jnp.where(bits < 0, bits ^ jnp.int32(0x7FFFFFFF), bits)


def _indexer_kernel(q_ref, kx_ref, wt_ref, mask_ref, key_ref, *, t, s, n_heads, n_sel, kc):
    qi = pl.program_id(1)
    qpos = qi * t + lax.broadcasted_iota(jnp.int32, (1, t), 1)
    n_chunks = ((qi + 1) * t + kc - 1) // kc
    wt = wt_ref[0]
    qh = [q_ref[0][:, h * IDX_DIM:(h + 1) * IDX_DIM].astype(BF16) for h in range(n_heads)]

    def kpos_of(c):
        return c * kc + lax.broadcasted_iota(jnp.int32, (kc, 1), 0)

    def rows(c):
        return pl.ds(pl.multiple_of(c * kc, kc), kc)

    def score_chunk(c, carry):
        kk = kx_ref[0, rows(c), :][:, :IDX_DIM].astype(BF16)
        score = jnp.zeros((kc, t), F32)
        for h in range(n_heads):
            d = lax.dot_general(kk, qh[h], (((1,), (1,)), ((), ())), preferred_element_type=F32)
            score = score + wt[h:h + 1, :] * jnp.maximum(d, 0.0)
        score = jnp.where(kpos_of(c) <= qpos, score + 0.0, NEG_INF)
        key_ref[rows(c), :] = _float_key(score)
        return carry

    lax.fori_loop(0, n_chunks, score_chunk, 0)

    def count(pred):
        def body(c, acc):
            hit = jnp.where(pred(key_ref[rows(c), :], kpos_of(c)), 1.0, 0.0)
            return acc + jnp.sum(hit.reshape(kc // COUNT_ROWS, COUNT_ROWS, t), axis=0)
        acc = lax.fori_loop(0, n_chunks, body, jnp.zeros((COUNT_ROWS, t), F32))
        return jnp.sum(acc, axis=0, keepdims=True)

    def value_bit(it, state):
        tau_u, n_at = state
        cand_u = tau_u | jnp.left_shift(jnp.int32(1), 31 - it)
        cand = cand_u ^ jnp.int32(INT_MIN)
        n_cand = count(lambda key, kpos: key >= cand)
        accept = n_cand >= n_sel
        return jnp.where(accept, cand_u, tau_u), jnp.where(accept, n_cand, n_at)

    n_all = jnp.broadcast_to((n_chunks * kc).astype(F32), (1, t))
    tau_u, n_at = lax.fori_loop(0, 32, value_bit, (jnp.zeros((1, t), jnp.int32), n_all))
    tau = tau_u ^ jnp.int32(INT_MIN)

    need = n_sel - count(lambda key, kpos: key > tau)
    n_bits = int(math.log2(s))

    def index_bit(it, bound):
        cand = bound | jnp.left_shift(jnp.int32(1), n_bits - 1 - it)
        below = count(lambda key, kpos: (key == tau) & (kpos < cand))
        return jnp.where(below < need, cand, bound)

    tied = jnp.max(n_at) > n_sel
    bound = lax.cond(tied,
                     lambda: lax.fori_loop(0, n_bits, index_bit, jnp.zeros((1, t), jnp.int32)),
                     lambda: jnp.full((1, t), s - 1, jnp.int32))

    def write_chunk(c, carry):
        key = key_ref[rows(c), :]
        kpos = kpos_of(c)
        sel = ((key > tau) | ((key == tau) & (kpos <= bound))) & (kpos <= qpos)
        mask_ref[0, rows(c), :] = jnp.where(sel, 1, 0).astype(jnp.int8)
        return carry

    def zero_chunk(c, carry):
        mask_ref[0, rows(c), :] = jnp.zeros((kc, t), jnp.int8)
        return carry

    lax.fori_loop(0, n_chunks, write_chunk, 0)
    lax.fori_loop(n_chunks, s // kc, zero_chunk, 0)


def _indexer_mask(rest, w_t, t, n_heads, n_sel, qidx_block, kx_block):
    b, s, _ = rest.shape
    assert s & (s - 1) == 0
    kc = min(s, 512)
    assert kc >= n_sel and kc % t == 0
    wq = n_heads * IDX_DIM
    return pl.pallas_call(
        functools.partial(_indexer_kernel, t=t, s=s, n_heads=n_heads, n_sel=n_sel, kc=kc),
        grid=(b, s // t),
        in_specs=[pl.BlockSpec((1, t, wq), lambda i, j: (i, j, qidx_block)),
                  pl.BlockSpec((1, s, LANES), lambda i, j: (i, 0, kx_block)),
                  pl.BlockSpec((1, n_heads, t), lambda i, j: (i, 0, j))],
        out_specs=pl.BlockSpec((1, s, t), lambda i, j: (i, 0, j)),
        out_shape=jax.ShapeDtypeStruct((b, s, s), jnp.int8),
        scratch_shapes=[pltpu.VMEM((s, t), jnp.int32)],
        compiler_params=_cparams(("parallel", "parallel")),
        name="dsa_indexer_topk_mask",
    )(rest, rest, w_t)


def _dsa_kernel(q_ref, k_ref, vt_ref, bias_ref, mask_ref, rel_ref, o_ref, acc_ref, raw_scr, p_scr, *, t, group):
    hg = pl.program_id(1)
    qi = pl.program_id(2)
    q = q_ref[0]
    hd = HEAD_DIM

    def key_rows(kj):
        return pl.ds(pl.multiple_of(kj * t, t), t)

    def logits_fn(kj):
        k = k_ref[0, key_rows(kj), :]
        return [_kq(k[:, g * hd:(g + 1) * hd], q[:, g * hd:(g + 1) * hd]) for g in range(group)]

    def bias_fn(g, kj, raw, kind):
        keep = mask_ref[0, key_rows(kj), :].astype(F32) > 0.5
        if kind == "far":
            return jnp.where(keep, raw, NEG_INF), rel_ref[NUM_BUCKETS - 1, group * hg + g] * LOG2E
        return jnp.where(keep, raw + bias_ref[g, 1 if kind == "near" else 0], NEG_INF), 0.0

    def values_fn(g, kj):
        return vt_ref[0, g * hd:(g + 1) * hd, key_rows(kj)]

    denom = _pipelined_sweep(qi, group, t, logits_fn, bias_fn, values_fn, raw_scr, p_scr, acc_ref, near_tile=True)
    o_ref[0] = jnp.concatenate([(acc_ref[g] / denom[g]).T for g in range(group)], axis=1).astype(o_ref.dtype)


def _dsa_attention(q, k, vt, mask_t, bias_tiles, rel_bias, n_heads, group=HEADS_PER_STEP):
    b, s, _ = q.shape
    t = bias_tiles.shape[-1]
    hd = HEAD_DIM
    group = min(group, n_heads)
    w = group * hd
    ng = n_heads // group
    return pl.pallas_call(
        functools.partial(_dsa_kernel, t=t, group=group),
        grid=(b, ng, s // t),
        in_specs=[pl.BlockSpec((1, t, w), lambda i, h, j: (i, j, h)),
                  pl.BlockSpec((1, s, w), lambda i, h, j: (i, 0, h)),
                  pl.BlockSpec((1, w, s), lambda i, h, j: (i, h, 0)),
                  pl.BlockSpec((group, 2, t, t), lambda i, h, j: (h, 0, 0, 0)),
                  pl.BlockSpec((1, s, t), lambda i, h, j: (i, 0, j)),
                  pl.BlockSpec(memory_space=pltpu.SMEM)],
        out_specs=pl.BlockSpec((1, t, w), lambda i, h, j: (i, j, h)),
        out_shape=jax.ShapeDtypeStruct((b, s, n_heads * hd), BF16),
        scratch_shapes=_sweep_scratch(group, hd, t),
        compiler_params=_cparams(("parallel", "parallel", "arbitrary")),
        name="dsa_attention",
    )(q, k, vt, bias_tiles, mask_t, rel_bias)


def _router_kernel(h_ref, w_ref, b_ref, idx_ref, wt_ref, cnt_ref, seen_ref):
    @pl.when(pl.program_id(0) == 0)
    def _():
        seen_ref[...] = jnp.zeros_like(seen_ref)

    logits = jnp.dot(h_ref[...].astype(BF16), w_ref[...], preferred_element_type=F32) + b_ref[...]
    tm = logits.shape[0]
    lane = lax.broadcasted_iota(jnp.int32, logits.shape, 1).astype(F32)
    top_v, top_i = [], []
    for _ in range(TOP_K):
        mx = jnp.max(logits, axis=-1, keepdims=True)
        first = jnp.min(jnp.where(logits == mx, lane, float(LANES)), axis=-1, keepdims=True)
        top_v.append(mx)
        top_i.append(first)
        logits = jnp.where(lane == first, -jnp.inf, logits)
    e = [jnp.exp(v - top_v[0]) for v in top_v]
    denom = e[0] + e[1] + e[2] + e[3]
    chosen = jnp.zeros(logits.shape, F32)
    for k in range(TOP_K):
        chosen = chosen + jnp.where(lane == top_i[k], 1.0, 0.0)
    earlier = (lax.broadcasted_iota(jnp.int32, (tm, tm), 0) > lax.broadcasted_iota(jnp.int32, (tm, tm), 1))
    before = jnp.dot(jnp.where(earlier, 1.0, 0.0).astype(BF16), chosen.astype(BF16),
                     preferred_element_type=F32) + seen_ref[...]
    idx = jnp.zeros(logits.shape, F32)
    wts = jnp.zeros(logits.shape, F32)
    for k in range(TOP_K):
        rank = jnp.sum(jnp.where(lane == top_i[k], before, 0.0), axis=-1, keepdims=True)
        idx = jnp.where(lane == float(k), top_i[k], idx)
        idx = jnp.where(lane == float(TOP_K + k), rank, idx)
        wts = jnp.where(lane == float(k), e[k] / denom, wts)
    idx_ref[...] = idx.astype(jnp.int32)
    wt_ref[...] = wts
    seen_ref[...] += jnp.sum(chosen, axis=0, keepdims=True)
    cnt_ref[...] = seen_ref[...].astype(jnp.int32)


def _router(h2, w_router, b_router):
    n, d = h2.shape
    ne = w_router.shape[1]
    tm = _row_tile(n, 1024)
    w = _pad_cols(w_router).astype(BF16)
    bias = jnp.concatenate([b_router, jnp.full((LANES - ne,), NEG_INF, F32)]).reshape(1, LANES)
    out = pl.BlockSpec((tm, LANES), lambda i: (i, 0))
    return pl.pallas_call(
        _router_kernel,
        grid=(n // tm,),
        in_specs=[pl.BlockSpec((tm, d), lambda i: (i, 0)),
                  pl.BlockSpec((d, LANES), lambda i: (0, 0)),
                  pl.BlockSpec((1, LANES), lambda i: (0, 0))],
        out_specs=[out, out, pl.BlockSpec((1, LANES), lambda i: (0, 0))],
        out_shape=[jax.ShapeDtypeStruct((n, LANES), jnp.int32), jax.ShapeDtypeStruct((n, LANES), F32),
                   jax.ShapeDtypeStruct((1, LANES), jnp.int32)],
        scratch_shapes=[pltpu.VMEM((1, LANES), F32)],
        compiler_params=_cparams(("arbitrary",)),
        name="moe_router_top4",
    )(h2, w, bias)


GU_GROUP = 2 * LANES


def _split_even_odd_kernel(w_ref, p_ref, o_ref):
    for g in range(w_ref.shape[2] // GU_GROUP):
        cols = slice(g * GU_GROUP, (g + 1) * GU_GROUP)
        o_ref[0, :, cols] = jnp.dot(w_ref[0, :, cols].astype(BF16), p_ref[...],
                                    preferred_element_type=F32).astype(o_ref.dtype)


def _split_even_odd_cols(w):
    e, d, n = w.shape
    assert n % GU_GROUP == 0
    half = GU_GROUP // 2
    perm = np.zeros((GU_GROUP, GU_GROUP), np.float32)
    perm[2 * np.arange(half), np.arange(half)] = 1.0
    perm[2 * np.arange(half) + 1, half + np.arange(half)] = 1.0
    td = _row_tile(d, 512)
    return pl.pallas_call(
        _split_even_odd_kernel,
        grid=(e, d // td),
        in_specs=[pl.BlockSpec((1, td, n), lambda i, j: (i, j, 0)),
                  pl.BlockSpec((GU_GROUP, GU_GROUP), lambda i, j: (0, 0))],
        out_specs=pl.BlockSpec((1, td, n), lambda i, j: (i, j, 0)),
        out_shape=jax.ShapeDtypeStruct((e, d, n), BF16),
        compiler_params=_cparams(("parallel", "parallel")),
        name="gate_up_split_columns",
    )(w, jnp.asarray(perm, BF16))


def _split_even_odd_bias(b):
    e, n = b.shape
    return jnp.transpose(b.reshape(e, n // GU_GROUP, GU_GROUP // 2, 2), (0, 1, 3, 2)).reshape(e, n)


def _expert_ffn(h, wgu, bgu, wd, bd):
    gu = jnp.dot(h, wgu, preferred_element_type=F32) + bgu
    half = GU_GROUP // 2
    acts = []
    for g in range(gu.shape[1] // GU_GROUP):
        gate = jnp.minimum(gu[:, g * GU_GROUP:g * GU_GROUP + half], SWIGLU_LIMIT)
        up = jnp.clip(gu[:, g * GU_GROUP + half:(g + 1) * GU_GROUP], -SWIGLU_LIMIT, SWIGLU_LIMIT)
        acts.append(((up + 1.0) * (gate * _sigmoid(gate * SWIGLU_ALPHA))).astype(BF16))
    act = jnp.concatenate(acts, axis=1)
    return jnp.dot(act, wd, preferred_element_type=F32) + bd


def _row_gather_start(src_hbm, rows_ref, dst, sem, n_rows):
    for r in range(n_rows):
        copy = pltpu.make_async_copy(src_hbm.at[pl.ds(rows_ref[0, 0, r], 1)], dst.at[pl.ds(r, 1)], sem)
        copy.start(priority=r % 2)


def _row_gather_wait(src_hbm, dst, sem):
    pltpu.make_async_copy(src_hbm.at[pl.ds(0, dst.shape[0])], dst, sem).wait()


def _moe_expert_kernel(te_ref, tok_ref, tok_next_ref, wrow_ref, wgu_ref, bgu_ref, wd_ref, bd_ref, h_hbm,
                       o_ref, xbuf, sem, *, tm):
    i = pl.program_id(0)
    slot = i % 2

    @pl.when(i == 0)
    def _():
        _row_gather_start(h_hbm, tok_ref, xbuf.at[0], sem.at[0], tm)

    _row_gather_start(h_hbm, tok_next_ref, xbuf.at[1 - slot], sem.at[1 - slot], tm)
    _row_gather_wait(h_hbm, xbuf.at[slot], sem.at[slot])
    y = _expert_ffn(xbuf[slot].astype(BF16), wgu_ref[0], bgu_ref[0], wd_ref[0], bd_ref[0])
    o_ref[...] = y * jnp.tile(wrow_ref[...], (1, y.shape[1] // LANES))

    @pl.when(i == pl.num_programs(0) - 1)
    def _():
        _row_gather_wait(h_hbm, xbuf.at[1 - slot], sem.at[1 - slot])


def _moe_experts_sparse(h2, tile_expert, tok_tiles, w_rows, wgu, bgu, wd, bd, tm):
    n, d = h2.shape
    ne, _, f2 = wgu.shape
    n_tiles = tile_expert.shape[0]
    exp3 = lambda i, te: (te[i], 0, 0)
    smem_rows = lambda off: pl.BlockSpec((1, 1, tm), lambda i, te: (i + off, 0, 0), memory_space=pltpu.SMEM)
    grid_spec = pltpu.PrefetchScalarGridSpec(
        num_scalar_prefetch=1,
        grid=(n_tiles,),
        in_specs=[smem_rows(0), smem_rows(1),
                  pl.BlockSpec((tm, LANES), lambda i, te: (i, 0)),
                  pl.BlockSpec((1, d, f2), exp3), pl.BlockSpec((1, 1, f2), exp3),
                  pl.BlockSpec((1, f2 // 2, d), exp3), pl.BlockSpec((1, 1, d), exp3),
                  pl.BlockSpec(memory_space=pl.ANY)],
        out_specs=pl.BlockSpec((tm, d), lambda i, te: (i, 0)),
        scratch_shapes=[pltpu.VMEM((2, tm, d), F32), pltpu.SemaphoreType.DMA((2,))],
    )
    return pl.pallas_call(
        functools.partial(_moe_expert_kernel, tm=tm),
        grid_spec=grid_spec,
        out_shape=jax.ShapeDtypeStruct((n_tiles * tm, d), F32),
        compiler_params=_cparams(("arbitrary",)),
        name="moe_experts_sparse",
    )(tile_expert, tok_tiles, tok_tiles, w_rows, wgu, bgu.reshape(ne, 1, f2), wd, bd.reshape(ne, 1, d), h2)


def _moe_combine_kernel(pos_ref, pos_next_ref, x_ref, g2_ref, y_hbm, o_ref, ybuf, sem, *, tm):
    i = pl.program_id(0)
    slot = i % 2
    n_rows = TOP_K * tm

    @pl.when(i == 0)
    def _():
        _row_gather_start(y_hbm, pos_ref, ybuf.at[0], sem.at[0], n_rows)

    _row_gather_start(y_hbm, pos_next_ref, ybuf.at[1 - slot], sem.at[1 - slot], n_rows)
    _row_gather_wait(y_hbm, ybuf.at[slot], sem.at[slot])
    acc = ybuf[slot, 0:tm, :]
    for k in range(1, TOP_K):
        acc = acc + ybuf[slot, k * tm:(k + 1) * tm, :]
    o_ref[...] = x_ref[...] + g2_ref[0] * acc

    @pl.when(i == pl.num_programs(0) - 1)
    def _():
        _row_gather_wait(y_hbm, ybuf.at[1 - slot], sem.at[1 - slot])


def _moe_combine(y_rows, pos_tiles, x2, g2, rows_per_batch, tm):
    n, d = x2.shape
    nb = g2.shape[0]
    tiles_per_batch = rows_per_batch // tm
    smem_rows = lambda off: pl.BlockSpec((1, 1, TOP_K * tm), lambda i: (i + off, 0, 0), memory_space=pltpu.SMEM)
    return pl.pallas_call(
        functools.partial(_moe_combine_kernel, tm=tm),
        grid=(n // tm,),
        in_specs=[smem_rows(0), smem_rows(1),
                  pl.BlockSpec((tm, d), lambda i: (i, 0)),
                  pl.BlockSpec((1, 1, d), lambda i: (i // tiles_per_batch, 0, 0)),
                  pl.BlockSpec(memory_space=pl.ANY)],
        out_specs=pl.BlockSpec((tm, d), lambda i: (i, 0)),
        out_shape=jax.ShapeDtypeStruct((n, d), F32),
        scratch_shapes=[pltpu.VMEM((2, TOP_K * tm, d), F32), pltpu.SemaphoreType.DMA((2,))],
        compiler_params=_cparams(("arbitrary",)),
        name="moe_combine",
    )(pos_tiles, pos_tiles, x2, g2.reshape(nb, 1, d), y_rows)


def _moe_dispatch_plan(idx4, rank4, w4, counts, tm):
    n = idx4.shape[0]
    ne = counts.shape[0]
    n_pairs = n * TOP_K
    big = 1 << (n_pairs + tm - 1).bit_length()
    unused = jnp.int32(ne * big)
    eid = idx4.reshape(-1)
    experts = jnp.arange(ne, dtype=jnp.int32)
    n_pad = (-counts) % tm
    row_off = jnp.cumsum(counts + n_pad) - (counts + n_pad)
    pos = rank4 + jnp.sum(jnp.where(idx4[:, :, None] == experts, row_off, 0), axis=-1)
    j = jnp.arange(tm, dtype=jnp.int32)
    pad_keys = jnp.where(j[None, :] < n_pad[:, None], experts[:, None] * big + n_pairs + j[None, :], unused)
    keys = jnp.concatenate([eid * big + jnp.arange(n_pairs, dtype=jnp.int32), pad_keys.reshape(-1)])
    wvals = jnp.concatenate([w4.reshape(-1), jnp.zeros((ne * tm,), F32)])
    skeys, w_rows = lax.sort((keys, wvals), num_keys=1)
    slot = skeys % big
    is_pair = (skeys < unused) & (slot < n_pairs)
    tok_rows = jnp.where(is_pair, slot // TOP_K, 0)
    tile_e = jnp.minimum(skeys[::tm] // big, ne - 1)
    return tile_e, tok_rows, w_rows, pos


def _dsa_layer(h, x, g1, w_in, kv_gain, w_uk, w_uv, w_out, rel_bias, bias_tiles):
    b, s, d = h.shape
    n_heads = d // HEAD_DIM
    rank = kv_gain.shape[0]
    idx_heads = n_heads // 2
    n_sel = min(TOPK_MAX, s // 4)
    o1 = n_heads * HEAD_DIM
    h2 = h.reshape(b * s, d)
    q = _matmul(h2, w_in[:, :o1].astype(BF16), BF16, col_scale=_query_col_scale(o1, o1)).reshape(b, s, o1)
    w_rest = _pad_cols(w_in[:, o1:]).astype(BF16)
    rest = _matmul(h2, w_rest, F32, tn=w_rest.shape[1]).reshape(b, s, -1)
    assert rank % LANES == 0 and (idx_heads * IDX_DIM) == rank
    ckv = _norm(rest, kv_gain, BF16)
    w_k = jnp.transpose(w_uk, (2, 0, 1)).reshape(rank, o1)
    w_v = jnp.transpose(w_uv, (1, 0, 2)).reshape(rank, o1)
    ckv2 = ckv.reshape(b * s, rank)
    k = _matmul(ckv2, w_k.astype(BF16), BF16).reshape(b, s, o1)
    vt = _matmul_transposed(ckv2, w_v.T.astype(BF16), BF16, s)
    t = bias_tiles.shape[-1]
    w_off = 2 * rank + IDX_DIM
    w_t = jnp.transpose(rest[:, :, w_off:w_off + idx_heads], (0, 2, 1))
    mask_t = _indexer_mask(rest, w_t, t, idx_heads, n_sel, qidx_block=1, kx_block=2 * rank // LANES)
    o = _dsa_attention(q, k, vt, mask_t, bias_tiles, rel_bias, n_heads)
    return _matmul_residual(o.reshape(b * s, o1), w_out.astype(BF16), x.reshape(b * s, d), g1, s)


def _fox_layer(h, x, g1, w_in, f_bias, w_out):
    b, s, d = h.shape
    n_heads = f_bias.shape[0]
    hd3 = 3 * n_heads * HEAD_DIM
    h2 = h.reshape(b * s, d)
    hd2 = 2 * hd3 // 3
    qk = _matmul(h2, w_in[:, :hd2].astype(BF16), BF16, col_scale=_query_col_scale(hd2 // 2, hd2)).reshape(b, s, hd2)
    vt = _matmul_transposed(h2, w_in[:, hd2:hd3].T.astype(BF16), BF16, s)
    w_f = _pad_cols(w_in[:, hd3:]).astype(BF16)
    fg = _matmul(h2, w_f, F32, tn=w_f.shape[1]).reshape(b, s, -1)
    cum = _logsig_cumsum(fg, f_bias)
    o = _fox_attention(qk, vt, cum, n_heads)
    return _matmul_residual(o.reshape(b * s, -1), w_out.astype(BF16), x.reshape(b * s, d), g1, s)


def _diff_layer(h, x, g1, w_in, lam_p, sub_gain, w_out, rel_bias, bias_tiles, lam_init):
    b, s, d = h.shape
    n_heads = w_in.shape[1] // (6 * HEAD_DIM)
    h2 = h.reshape(b * s, d)
    n_qk = 2 * w_in.shape[1] // 3
    qk = _matmul(h2, w_in[:, :n_qk].astype(BF16), BF16, col_scale=_query_col_scale(n_qk // 2, n_qk)).reshape(b, s, n_qk)
    vt = _matmul_transposed(h2, w_in[:, n_qk:].T.astype(BF16), BF16, s)
    o = _diff_attention(qk, vt, bias_tiles, rel_bias, lam_p, sub_gain, n_heads, lam_init)
    return _matmul_residual(o.reshape(b * s, -1), w_out.astype(BF16), x.reshape(b * s, d), g1, s)


def _moe_layer(h, x2, g2, w_router, b_router, wgu, b_gu, w_dn, b_dn):
    b, s, d = h.shape
    n = b * s
    ne = wgu.shape[0]
    h2 = h.reshape(n, d)
    idx, wts, counts = _router(h2, w_router, b_router)
    tm = min(MOE_ROW_TILE, n * TOP_K)
    tmc = _row_tile(s, MOE_TOKEN_TILE)
    tile_e, tok_rows, w_rows, pos = _moe_dispatch_plan(idx[:, :TOP_K], idx[:, TOP_K:2 * TOP_K], wts[:, :TOP_K],
                                                       counts[0, :ne], tm)
    tok_tiles = jnp.pad(tok_rows.reshape(-1, 1, tm), ((0, 1), (0, 0), (0, 0)))
    w_rows = jnp.broadcast_to(w_rows[:, None], (w_rows.shape[0], LANES))
    bgu = _split_even_odd_bias(b_gu)
    y_rows = _moe_experts_sparse(h2, tile_e, tok_tiles, w_rows, wgu, bgu, w_dn.astype(BF16), b_dn, tm)
    pos_tiles = jnp.transpose(pos.reshape(n // tmc, tmc, TOP_K), (0, 2, 1)).reshape(n // tmc, TOP_K * tmc)
    pos_tiles = jnp.pad(pos_tiles, ((0, 1), (0, 0)))[:, None, :]
    return _moe_combine(y_rows, pos_tiles, x2, g2, s, tmc)


def kernel(x, c, rel_bias, norm_mix, norm_ffn, w_mod, b_mod, dsa_w_in, dsa_kv_gain, dsa_w_uk, dsa_w_uv, dsa_w_out, fox_w_in, fox_forget_bias, fox_w_out, diff_w_in, diff_lambda, diff_subln_gain, diff_w_out, w_router, b_router, w_gate_up, b_gate_up, w_down, b_down, final_norm):
    b, s, d = x.shape
    depth = w_mod.shape[0]
    mod = _modulation(c, w_mod, b_mod)
    t_attn = _row_tile(s, 256)
    bias_tiles = _bias_tiles(rel_bias, t_attn)
    ne, ff2 = w_gate_up.shape[1], w_gate_up.shape[3]
    wgu_all = _split_even_odd_cols(w_gate_up.reshape(depth * ne, d, ff2)).reshape(depth, ne, d, ff2)
    ia = ib = ic = 0
    for i in range(depth):
        sh1, sc1, g1, sh2, sc2, g2 = (mod[i, :, k * d:(k + 1) * d] for k in range(6))
        h = _norm_mod(x, norm_mix[i], sc1, sh1)
        kind = i % N_MIXERS
        if kind == 0:
            x2 = _dsa_layer(h, x, g1, dsa_w_in[ia], dsa_kv_gain[ia], dsa_w_uk[ia], dsa_w_uv[ia],
                            dsa_w_out[ia], rel_bias, bias_tiles)
            ia += 1
        elif kind == 1:
            x2 = _fox_layer(h, x, g1, fox_w_in[ib], fox_forget_bias[ib], fox_w_out[ib])
            ib += 1
        else:
            lam_init = 0.8 - 0.6 * math.exp(-0.3 * i)
            x2 = _diff_layer(h, x, g1, diff_w_in[ic], diff_lambda[ic], diff_subln_gain[ic],
                             diff_w_out[ic], rel_bias, bias_tiles, lam_init)
            ic += 1
        x = x2.reshape(b, s, d)
        h = _norm_mod(x, norm_ffn[i], sc2, sh2, out_dtype=F32)
        x = _moe_layer(h, x2, g2, w_router[i], b_router[i], wgu_all[i], b_gate_up[i],
                       w_down[i], b_down[i]).reshape(b, s, d)
    return _norm(x, final_norm, x.dtype)
```

```python
import functools
import math

import numpy as np
import jax
import jax.numpy as jnp
from jax import lax
from jax.experimental import pallas as pl
from jax.experimental.pallas import tpu as pltpu

HEAD_DIM = 128
IDX_DIM = 64
TOPK_MAX = 256
TOP_K = 4
SWIGLU_LIMIT = 7.0
SWIGLU_ALPHA = 1.702
NUM_BUCKETS = 32
MAX_DISTANCE = 128
RMS_EPS = 1e-6
NEG_INF = -1e30
N_MIXERS = 3
LANES = 128
COUNT_ROWS = 64
INT_MIN = -(2 ** 31)
LOG2E = 1.4426950408889634
QUERY_SCALE = HEAD_DIM ** -0.5 * LOG2E

F32 = jnp.float32
BF16 = jnp.bfloat16

VMEM_LIMIT = 56 * 1024 * 1024
HEADS_PER_STEP = 4
MOE_ROW_TILE = 512
MOE_TOKEN_TILE = 128


def _cparams(sem):
    return pltpu.CompilerParams(dimension_semantics=sem, vmem_limit_bytes=VMEM_LIMIT)


def _sigmoid(x):
    return 1.0 / (1.0 + jnp.exp(-x))


def _row_tile(s, want):
    t = min(s, want)
    assert s % t == 0
    return t


def _pad_cols(w, mult=LANES):
    n = w.shape[-1]
    pad = (-n) % mult
    if pad:
        w = jnp.pad(w, [(0, 0)] * (w.ndim - 1) + [(0, pad)])
    return w


def _mod_kernel(c_ref, w_ref, b_ref, o_ref):
    c = c_ref[...]
    cs = (c * _sigmoid(c)).astype(BF16)
    o_ref[0] = jnp.dot(cs, w_ref[0].astype(BF16), preferred_element_type=F32) + b_ref[0]


def _modulation(c, w_mod, b_mod):
    depth, d, n = w_mod.shape
    b = c.shape[0]
    tn = min(n, 1024)
    return pl.pallas_call(
        _mod_kernel,
        grid=(depth, n // tn),
        in_specs=[pl.BlockSpec((b, d), lambda i, j: (0, 0)),
                  pl.BlockSpec((1, d, tn), lambda i, j: (i, 0, j)),
                  pl.BlockSpec((1, 1, tn), lambda i, j: (i, 0, j))],
        out_specs=pl.BlockSpec((1, b, tn), lambda i, j: (i, 0, j)),
        out_shape=jax.ShapeDtypeStruct((depth, b, n), F32),
        compiler_params=_cparams(("parallel", "parallel")),
        name="adaln_modulation",
    )(c, w_mod, b_mod.reshape(depth, 1, n))


def _norm_mod_kernel(x_ref, g_ref, sc_ref, sh_ref, o_ref):
    x = x_ref[0]
    ms = jnp.mean(x * x, axis=-1, keepdims=True)
    y = x * lax.rsqrt(ms + RMS_EPS) * g_ref[...]
    o_ref[0] = (y * (1.0 + sc_ref[0]) + sh_ref[0]).astype(o_ref.dtype)


def _norm_kernel(x_ref, g_ref, o_ref):
    x = x_ref[0]
    ms = jnp.mean(x * x, axis=-1, keepdims=True)
    o_ref[0] = (x * lax.rsqrt(ms + RMS_EPS) * g_ref[...]).astype(o_ref.dtype)


def _pack_bf16_pairs(y):
    bits = pltpu.bitcast(y.astype(jnp.bfloat16).astype(F32), jnp.int32)
    half = y.shape[1] // 2
    return (bits[:, half:] & jnp.int32(-65536)) | lax.shift_right_logical(bits[:, :half], jnp.int32(16))


def _unpack_bf16_pairs(words):
    lo = pltpu.bitcast(lax.shift_left(words, jnp.int32(16)), F32)
    hi = pltpu.bitcast(words & jnp.int32(-65536), F32)
    return jnp.concatenate([lo, hi], axis=1).astype(jnp.bfloat16)


def _norm_mod_packed_kernel(x_ref, g_ref, sc_ref, sh_ref, o_ref):
    x = x_ref[0]
    ms = jnp.mean(x * x, axis=-1, keepdims=True)
    y = x * lax.rsqrt(ms + RMS_EPS) * g_ref[...]
    o_ref[0] = _pack_bf16_pairs(y * (1.0 + sc_ref[0]) + sh_ref[0])


def _norm_mod(x, g, sc, sh, out_dtype=BF16, packed=False):
    b, s, d = x.shape
    ts = _row_tile(s, 512)
    vec = pl.BlockSpec((1, 1, d), lambda i, j: (i, 0, 0))
    d_out = d // 2 if packed else d
    return pl.pallas_call(
        _norm_mod_packed_kernel if packed else _norm_mod_kernel,
        grid=(b, s // ts),
        in_specs=[pl.BlockSpec((1, ts, d), lambda i, j: (i, j, 0)),
                  pl.BlockSpec((1, d), lambda i, j: (0, 0)), vec, vec],
        out_specs=pl.BlockSpec((1, ts, d_out), lambda i, j: (i, j, 0)),
        out_shape=jax.ShapeDtypeStruct((b, s, d_out), jnp.int32 if packed else out_dtype),
        compiler_params=_cparams(("parallel", "parallel")),
        name="rmsnorm_adaln",
    )(x, g.reshape(1, d), sc.reshape(b, 1, d), sh.reshape(b, 1, d))


def _norm(x, g, out_dtype, col_block=0):
    b, s, _ = x.shape
    d = g.shape[-1]
    ts = _row_tile(s, 512)
    return pl.pallas_call(
        _norm_kernel,
        grid=(b, s // ts),
        in_specs=[pl.BlockSpec((1, ts, d), lambda i, j: (i, j, col_block)),
                  pl.BlockSpec((1, d), lambda i, j: (0, 0))],
        out_specs=pl.BlockSpec((1, ts, d), lambda i, j: (i, j, 0)),
        out_shape=jax.ShapeDtypeStruct((b, s, d), out_dtype),
        compiler_params=_cparams(("parallel", "parallel")),
        name="rmsnorm",
    )(x, g.reshape(1, d))


def _mm_kernel(a_ref, b_ref, o_ref):
    o_ref[...] = jnp.dot(a_ref[...], b_ref[...], preferred_element_type=F32).astype(o_ref.dtype)


def _mm_res_kernel(a_ref, b_ref, x_ref, g_ref, o_ref):
    y = jnp.dot(a_ref[...], b_ref[...], preferred_element_type=F32)
    o_ref[...] = x_ref[...] + g_ref[0] * y


def _mm_scaled_kernel(a_ref, b_ref, s_ref, o_ref):
    y = jnp.dot(a_ref[...], b_ref[...], preferred_element_type=F32)
    o_ref[...] = (y * s_ref[...]).astype(o_ref.dtype)


def _matmul(a, w, out_dtype, tm=1024, tn=512, col_scale=None):
    m, k = a.shape
    n = w.shape[1]
    tm = _row_tile(m, tm)
    tn = _row_tile(n, tn)
    in_specs = [pl.BlockSpec((tm, k), lambda i, j: (i, 0)),
                pl.BlockSpec((k, tn), lambda i, j: (0, j))]
    args = (a, w)
    if col_scale is not None:
        in_specs.append(pl.BlockSpec((1, tn), lambda i, j: (0, j)))
        args += (col_scale.reshape(1, n),)
    return pl.pallas_call(
        _mm_kernel if col_scale is None else _mm_scaled_kernel,
        grid=(m // tm, n // tn),
        in_specs=in_specs,
        out_specs=pl.BlockSpec((tm, tn), lambda i, j: (i, j)),
        out_shape=jax.ShapeDtypeStruct((m, n), out_dtype),
        compiler_params=_cparams(("parallel", "parallel")),
        name="matmul",
    )(*args)


def _mm_t_kernel(w_ref, a_ref, o_ref):
    o_ref[0] = lax.dot_general(w_ref[...], a_ref[...], (((1,), (1,)), ((), ())),
                               preferred_element_type=F32).astype(o_ref.dtype)


def _matmul_transposed(a, w_t, out_dtype, rows_per_batch, tm=1024, tn=512):
    m, k = a.shape
    n = w_t.shape[0]
    tm = _row_tile(rows_per_batch, tm)
    tn = _row_tile(n, tn)
    tiles_per_batch = rows_per_batch // tm
    return pl.pallas_call(
        _mm_t_kernel,
        grid=(m // tm, n // tn),
        in_specs=[pl.BlockSpec((tn, k), lambda i, j: (j, 0)),
                  pl.BlockSpec((tm, k), lambda i, j: (i, 0))],
        out_specs=pl.BlockSpec((1, tn, tm), lambda i, j: (i // tiles_per_batch, j, i % tiles_per_batch)),
        out_shape=jax.ShapeDtypeStruct((m // rows_per_batch, n, rows_per_batch), out_dtype),
        compiler_params=_cparams(("parallel", "parallel")),
        name="matmul_transposed",
    )(w_t, a)


def _query_col_scale(n_query_cols, n_cols):
    return jnp.concatenate([jnp.full((n_query_cols,), QUERY_SCALE, F32), jnp.ones((n_cols - n_query_cols,), F32)])


def _matmul_residual(a, w, x, gate, rows_per_batch, tm=1024, tn=512):
    m, k = a.shape
    n = w.shape[1]
    tm = _row_tile(rows_per_batch, tm)
    tn = _row_tile(n, tn)
    tiles_per_batch = rows_per_batch // tm
    nb = gate.shape[0]
    return pl.pallas_call(
        _mm_res_kernel,
        grid=(m // tm, n // tn),
        in_specs=[pl.BlockSpec((tm, k), lambda i, j: (i, 0)),
                  pl.BlockSpec((k, tn), lambda i, j: (0, j)),
                  pl.BlockSpec((tm, tn), lambda i, j: (i, j)),
                  pl.BlockSpec((1, 1, tn), lambda i, j: (i // tiles_per_batch, 0, j))],
        out_specs=pl.BlockSpec((tm, tn), lambda i, j: (i, j)),
        out_shape=jax.ShapeDtypeStruct((m, n), F32),
        compiler_params=_cparams(("parallel", "parallel")),
        name="matmul_residual",
    )(a, w, x, gate.reshape(nb, 1, n))


def _kq(k, q):
    return lax.dot_general(k, q, (((1,), (1,)), ((), ())), preferred_element_type=F32)


def _causal_tile_mask_t(t):
    r = lax.broadcasted_iota(jnp.int32, (t, t), 0)
    c = lax.broadcasted_iota(jnp.int32, (t, t), 1)
    return r <= c


def _t5_bucket_np(dist):
    n = np.maximum(dist, 0)
    max_exact = NUM_BUCKETS // 2
    nf = np.maximum(n, 1).astype(np.float32)
    large = max_exact + (np.log(nf / np.float32(max_exact)) / np.float32(math.log(MAX_DISTANCE / max_exact))
                         * np.float32(NUM_BUCKETS - max_exact)).astype(np.int32)
    large = np.minimum(large, NUM_BUCKETS - 1)
    return np.where(n < max_exact, n, large).astype(np.int32)


def _bias_tile_kernel(bk_ref, rel_ref, o_ref):
    m = pl.program_id(0)
    bk = bk_ref[...]
    acc = jnp.zeros(bk.shape, F32)
    for b in range(NUM_BUCKETS):
        acc = jnp.where(bk == b, rel_ref[b, m], acc)
    o_ref[0] = acc * LOG2E


def _bias_tiles(rel_bias, t):
    assert t >= MAX_DISTANCE
    n_maps = rel_bias.shape[1]
    kk = np.arange(t)[:, None]
    qq = np.arange(t)[None, :]
    buckets = np.stack([_t5_bucket_np(qq - kk), _t5_bucket_np(t + qq - kk)])
    return pl.pallas_call(
        _bias_tile_kernel,
        grid=(n_maps,),
        in_specs=[pl.BlockSpec((2, t, t), lambda m: (0, 0, 0)),
                  pl.BlockSpec(memory_space=pltpu.SMEM)],
        out_specs=pl.BlockSpec((1, 2, t, t), lambda m: (m, 0, 0, 0)),
        out_shape=jax.ShapeDtypeStruct((n_maps, 2, t, t), F32),
        compiler_params=_cparams(("parallel",)),
        name="t5_bias_tiles",
    )(jnp.asarray(buckets), rel_bias)


def _pipelined_sweep(qi, n_chain, t, logits_fn, bias_fn, values_fn, raw_scr, p_scr, acc_ref, near_tile):
    acc_ref[...] = jnp.zeros_like(acc_ref)
    p_scr[...] = jnp.zeros_like(p_scr)
    first = logits_fn(0)
    for c in range(n_chain):
        raw_scr[c] = first[c]

    def flush_values(kj, state):
        for c in range(n_chain):
            acc_ref[c] = state[c][2] * acc_ref[c] + jnp.dot(values_fn(c, kj), p_scr[c], preferred_element_type=F32)

    def step(kj, state, kind, prefetch):
        slot = kj % 2
        flush_values(jnp.maximum(kj - 1, 0), state)
        new_state = []
        for c in range(n_chain):
            m_old, l_old, _ = state[c]
            u, row_const = bias_fn(c, kj, raw_scr[slot * n_chain + c], kind)
            m_new = jnp.maximum(m_old, jnp.max(u, axis=0, keepdims=True) + row_const)
            alpha = jnp.exp2(m_old - m_new)
            p = jnp.exp2(u + (row_const - m_new))
            p_scr[c] = p.astype(p_scr.dtype)
            new_state.append((m_new, alpha * l_old + jnp.sum(p, axis=0, keepdims=True), alpha))
        if prefetch:
            nxt = logits_fn(kj + 1)
            for c in range(n_chain):
                raw_scr[(1 - slot) * n_chain + c] = nxt[c]
        return tuple(new_state)

    init = (jnp.full((1, t), NEG_INF, F32), jnp.zeros((1, t), F32), jnp.ones((1, t), F32))
    state = (init,) * n_chain
    n_far = jnp.maximum(qi - 1, 0) if near_tile else qi
    state = lax.fori_loop(0, n_far, lambda kj, s: step(kj, s, "far", True), state)
    if near_tile:
        state = lax.fori_loop(n_far, qi, lambda kj, s: step(kj, s, "near", True), state)
    state = step(qi, state, "diag", False)
    flush_values(qi, state)
    return [s[1] for s in state]


def _sweep_scratch(n_chain, dv, t):
    return [pltpu.VMEM((n_chain, dv, t), F32), pltpu.VMEM((2 * n_chain, t, t), F32),
            pltpu.VMEM((n_chain, t, t), BF16)]


def _logsig_cumsum_kernel(f_ref, fb_ref, o_ref, carry_ref):
    @pl.when(pl.program_id(1) == 0)
    def _():
        carry_ref[...] = jnp.zeros_like(carry_ref)

    z = f_ref[0] + fb_ref[...]
    logf = -(jnp.maximum(-z, 0.0) + jnp.log(1.0 + jnp.exp(-jnp.abs(z))))
    t = z.shape[0]
    r = lax.broadcasted_iota(jnp.int32, (t, t), 0)
    c = lax.broadcasted_iota(jnp.int32, (t, t), 1)
    tri = jnp.where(r >= c, 1.0, 0.0).astype(BF16)
    hi = logf.astype(BF16)
    r1 = logf - hi.astype(F32)
    mid = r1.astype(BF16)
    lo = (r1 - mid.astype(F32)).astype(BF16)
    cum = (jnp.dot(tri, hi, preferred_element_type=F32)
           + jnp.dot(tri, mid, preferred_element_type=F32)
           + jnp.dot(tri, lo, preferred_element_type=F32)) + carry_ref[...]
    o_ref[0] = cum * LOG2E
    carry_ref[...] = cum[t - 1:t, :]


def _logsig_cumsum(fg, f_bias):
    b, s, w = fg.shape
    tc = _row_tile(s, 512)
    fb = jnp.pad(f_bias, (0, w - f_bias.shape[0])).reshape(1, w)
    return pl.pallas_call(
        _logsig_cumsum_kernel,
        grid=(b, s // tc),
        in_specs=[pl.BlockSpec((1, tc, w), lambda i, j: (i, j, 0)),
                  pl.BlockSpec((1, w), lambda i, j: (0, 0))],
        out_specs=pl.BlockSpec((1, tc, w), lambda i, j: (i, j, 0)),
        out_shape=jax.ShapeDtypeStruct((b, s, w), F32),
        scratch_shapes=[pltpu.VMEM((1, w), F32)],
        compiler_params=_cparams(("parallel", "arbitrary")),
        name="fox_logsig_cumsum",
    )(fg, fb)


def _fox_kernel(q_ref, k_ref, vt_ref, cq_ref, ck_ref, o_ref, acc_ref, raw_scr, p_scr, *, t, group):
    qi = pl.program_id(2)
    hd = HEAD_DIM
    q = q_ref[0]

    def key_rows(kj):
        return pl.ds(pl.multiple_of(kj * t, t), t)

    def logits_fn(kj):
        k = k_ref[0, key_rows(kj), :]
        return [_kq(k[:, g * hd:(g + 1) * hd], q[:, g * hd:(g + 1) * hd]) for g in range(group)]

    def bias_fn(g, kj, raw, kind):
        ck = jnp.tile(ck_ref[0, g, key_rows(kj), :], (1, t // LANES))
        u = raw - ck
        if kind == "diag":
            u = jnp.where(_causal_tile_mask_t(t), u, NEG_INF)
        return u, cq_ref[0, g]

    def values_fn(g, kj):
        return vt_ref[0, g * hd:(g + 1) * hd, key_rows(kj)]

    denom = _pipelined_sweep(qi, group, t, logits_fn, bias_fn, values_fn, raw_scr, p_scr, acc_ref, near_tile=False)
    o_ref[0] = jnp.concatenate([(acc_ref[g] / denom[g]).T for g in range(group)], axis=1).astype(o_ref.dtype)


def _fox_attention(qk, vt, cum, n_heads, group=HEADS_PER_STEP):
    b, s, _ = qk.shape
    t = _row_tile(s, 256)
    hd = HEAD_DIM
    group = min(group, n_heads)
    w = group * hd
    ng = n_heads // group
    cum_t = jnp.transpose(cum[:, :, :n_heads], (0, 2, 1))
    cum_q = cum_t[:, :, None, :]
    cum_k = jnp.broadcast_to(cum_t[..., None], (b, n_heads, s, LANES))
    return pl.pallas_call(
        functools.partial(_fox_kernel, t=t, group=group),
        grid=(b, ng, s // t),
        in_specs=[pl.BlockSpec((1, t, w), lambda i, h, j: (i, j, h)),
                  pl.BlockSpec((1, s, w), lambda i, h, j: (i, 0, ng + h)),
                  pl.BlockSpec((1, w, s), lambda i, h, j: (i, h, 0)),
                  pl.BlockSpec((1, group, 1, t), lambda i, h, j: (i, h, 0, j)),
                  pl.BlockSpec((1, group, s, LANES), lambda i, h, j: (i, h, 0, 0))],
        out_specs=pl.BlockSpec((1, t, w), lambda i, h, j: (i, j, h)),
        out_shape=jax.ShapeDtypeStruct((b, s, n_heads * hd), BF16),
        scratch_shapes=_sweep_scratch(group, hd, t),
        compiler_params=_cparams(("parallel", "parallel", "arbitrary")),
        name="fox_attention",
    )(qk, qk, vt, cum_q, cum_k)


def _diff_kernel(q_ref, k_ref, vt_ref, bias_ref, rel_ref, lam_ref, gain_ref, o_ref, acc_ref, raw_scr, p_scr,
                 *, t, lam_init, group):
    hg = pl.program_id(1)
    qi = pl.program_id(2)
    q = q_ref[0]
    hd = HEAD_DIM
    n_chain = 2 * group

    def key_rows(kj):
        return pl.ds(pl.multiple_of(kj * t, t), t)

    def logits_fn(kj):
        k = k_ref[0, key_rows(kj), :]
        return [_kq(k[:, c * hd:(c + 1) * hd], q[:, c * hd:(c + 1) * hd]) for c in range(n_chain)]

    def bias_fn(c, kj, raw, kind):
        if kind == "far":
            return raw, rel_ref[NUM_BUCKETS - 1, n_chain * hg + c] * LOG2E
        if kind == "near":
            return raw + bias_ref[c, 1], 0.0
        return jnp.where(_causal_tile_mask_t(t), raw + bias_ref[c, 0], NEG_INF), 0.0

    def values_fn(c, kj):
        return vt_ref[0, (c // 2) * 2 * hd:(c // 2 + 1) * 2 * hd, key_rows(kj)]

    denom = _pipelined_sweep(qi, n_chain, t, logits_fn, bias_fn, values_fn, raw_scr, p_scr, acc_ref, near_tile=True)

    lp = lam_ref[...]
    lam = (jnp.exp(jnp.sum(lp[0:1] * lp[1:2], axis=-1, keepdims=True))
           - jnp.exp(jnp.sum(lp[2:3] * lp[3:4], axis=-1, keepdims=True)) + lam_init)
    outs = []
    for g in range(group):
        o = (acc_ref[2 * g] / denom[2 * g] - lam * (acc_ref[2 * g + 1] / denom[2 * g + 1])).T
        ms = jnp.mean(o * o, axis=-1, keepdims=True)
        outs.append(o * lax.rsqrt(ms + RMS_EPS) * gain_ref[...] * (1.0 - lam_init))
    o_ref[0] = jnp.concatenate(outs, axis=1).astype(o_ref.dtype)


def _diff_attention(qk, vt, bias_tiles, rel_bias, lam_p, sub_gain, n_heads, lam_init, group=HEADS_PER_STEP // 2):
    b, s, _ = qk.shape
    t = bias_tiles.shape[-1]
    group = min(group, n_heads)
    hw = 2 * HEAD_DIM
    w = group * hw
    ng = n_heads // group
    return pl.pallas_call(
        functools.partial(_diff_kernel, t=t, lam_init=lam_init, group=group),
        grid=(b, ng, s // t),
        in_specs=[pl.BlockSpec((1, t, w), lambda i, h, j: (i, j, h)),
                  pl.BlockSpec((1, s, w), lambda i, h, j: (i, 0, ng + h)),
                  pl.BlockSpec((1, w, s), lambda i, h, j: (i, h, 0)),
                  pl.BlockSpec((2 * group, 2, t, t), lambda i, h, j: (h, 0, 0, 0)),
                  pl.BlockSpec(memory_space=pltpu.SMEM),
                  pl.BlockSpec((4, HEAD_DIM), lambda i, h, j: (0, 0)),
                  pl.BlockSpec((1, hw), lambda i, h, j: (0, 0))],
        out_specs=pl.BlockSpec((1, t, w), lambda i, h, j: (i, j, h)),
        out_shape=jax.ShapeDtypeStruct((b, s, n_heads * hw), BF16),
        scratch_shapes=_sweep_scratch(2 * group, hw, t),
        compiler_params=_cparams(("parallel", "parallel", "arbitrary")),
        name="diff_attention",
    )(qk, qk, vt, bias_tiles, rel_bias, lam_p, sub_gain.reshape(1, hw))


def _float_key(x):
    bits = pltpu.bitcast(x, jnp.int32)
    return jnp.where(bits < 0, bits ^ jnp.int32(0x7FFFFFFF), bits)


def _indexer_kernel(q_ref, kx_ref, wt_ref, mask_ref, key_ref, *, t, s, n_heads, n_sel, kc):
    qi = pl.program_id(1)
    qpos = qi * t + lax.broadcasted_iota(jnp.int32, (1, t), 1)
    n_chunks = ((qi + 1) * t + kc - 1) // kc
    wt = wt_ref[0]
    qh = [q_ref[0][:, h * IDX_DIM:(h + 1) * IDX_DIM].astype(BF16) for h in range(n_heads)]

    def kpos_of(c):
        return c * kc + lax.broadcasted_iota(jnp.int32, (kc, 1), 0)

    def rows(c):
        return pl.ds(pl.multiple_of(c * kc, kc), kc)

    def score_chunk(c, carry):
        kk = kx_ref[0, rows(c), :][:, :IDX_DIM].astype(BF16)
        score = jnp.zeros((kc, t), F32)
        for h in range(n_heads):
            d = lax.dot_general(kk, qh[h], (((1,), (1,)), ((), ())), preferred_element_type=F32)
            score = score + wt[h:h + 1, :] * jnp.maximum(d, 0.0)
        score = jnp.where(kpos_of(c) <= qpos, score + 0.0, NEG_INF)
        key_ref[rows(c), :] = _float_key(score)
        return carry

    lax.fori_loop(0, n_chunks, score_chunk, 0)

    def count(pred):
        def body(c, acc):
            hit = jnp.where(pred(key_ref[rows(c), :], kpos_of(c)), 1.0, 0.0)
            return acc + jnp.sum(hit.reshape(kc // COUNT_ROWS, COUNT_ROWS, t), axis=0)
        acc = lax.fori_loop(0, n_chunks, body, jnp.zeros((COUNT_ROWS, t), F32))
        return jnp.sum(acc, axis=0, keepdims=True)

    def value_bit(it, state):
        tau_u, n_at = state
        cand_u = tau_u | jnp.left_shift(jnp.int32(1), 31 - it)
        cand = cand_u ^ jnp.int32(INT_MIN)
        n_cand = count(lambda key, kpos: key >= cand)
        accept = n_cand >= n_sel
        return jnp.where(accept, cand_u, tau_u), jnp.where(accept, n_cand, n_at)

    n_all = jnp.broadcast_to((n_chunks * kc).astype(F32), (1, t))
    tau_u, n_at = lax.fori_loop(0, 32, value_bit, (jnp.zeros((1, t), jnp.int32), n_all))
    tau = tau_u ^ jnp.int32(INT_MIN)

    need = n_sel - count(lambda key, kpos: key > tau)
    n_bits = int(math.log2(s))

    def index_bit(it, bound):
        cand = bound | jnp.left_shift(jnp.int32(1), n_bits - 1 - it)
        below = count(lambda key, kpos: (key == tau) & (kpos < cand))
        return jnp.where(below < need, cand, bound)

    tied = jnp.max(n_at) > n_sel
    bound = lax.cond(tied,
                     lambda: lax.fori_loop(0, n_bits, index_bit, jnp.zeros((1, t), jnp.int32)),
                     lambda: jnp.full((1, t), s - 1, jnp.int32))

    def write_chunk(c, carry):
        key = key_ref[rows(c), :]
        kpos = kpos_of(c)
        sel = ((key > tau) | ((key == tau) & (kpos <= bound))) & (kpos <= qpos)
        mask_ref[0, rows(c), :] = jnp.where(sel, 1, 0).astype(jnp.int8)
        return carry

    def zero_chunk(c, carry):
        mask_ref[0, rows(c), :] = jnp.zeros((kc, t), jnp.int8)
        return carry

    lax.fori_loop(0, n_chunks, write_chunk, 0)
    lax.fori_loop(n_chunks, s // kc, zero_chunk, 0)


def _indexer_mask(rest, w_t, t, n_heads, n_sel, qidx_block, kx_block):
    b, s, _ = rest.shape
    assert s & (s - 1) == 0
    kc = min(s, 512)
    assert kc >= n_sel and kc % t == 0
    wq = n_heads * IDX_DIM
    return pl.pallas_call(
        functools.partial(_indexer_kernel, t=t, s=s, n_heads=n_heads, n_sel=n_sel, kc=kc),
        grid=(b, s // t),
        in_specs=[pl.BlockSpec((1, t, wq), lambda i, j: (i, j, qidx_block)),
                  pl.BlockSpec((1, s, LANES), lambda i, j: (i, 0, kx_block)),
                  pl.BlockSpec((1, n_heads, t), lambda i, j: (i, 0, j))],
        out_specs=pl.BlockSpec((1, s, t), lambda i, j: (i, 0, j)),
        out_shape=jax.ShapeDtypeStruct((b, s, s), jnp.int8),
        scratch_shapes=[pltpu.VMEM((s, t), jnp.int32)],
        compiler_params=_cparams(("parallel", "parallel")),
        name="dsa_indexer_topk_mask",
    )(rest, rest, w_t)


def _dsa_kernel(q_ref, k_ref, vt_ref, bias_ref, mask_ref, rel_ref, o_ref, acc_ref, raw_scr, p_scr, *, t, group):
    hg = pl.program_id(1)
    qi = pl.program_id(2)
    q = q_ref[0]
    hd = HEAD_DIM

    def key_rows(kj):
        return pl.ds(pl.multiple_of(kj * t, t), t)

    def logits_fn(kj):
        k = k_ref[0, key_rows(kj), :]
        return [_kq(k[:, g * hd:(g + 1) * hd], q[:, g * hd:(g + 1) * hd]) for g in range(group)]

    def bias_fn(g, kj, raw, kind):
        keep = mask_ref[0, key_rows(kj), :].astype(F32) > 0.5
        if kind == "far":
            return jnp.where(keep, raw, NEG_INF), rel_ref[NUM_BUCKETS - 1, group * hg + g] * LOG2E
        return jnp.where(keep, raw + bias_ref[g, 1 if kind == "near" else 0], NEG_INF), 0.0

    def values_fn(g, kj):
        return vt_ref[0, g * hd:(g + 1) * hd, key_rows(kj)]

    denom = _pipelined_sweep(qi, group, t, logits_fn, bias_fn, values_fn, raw_scr, p_scr, acc_ref, near_tile=True)
    o_ref[0] = jnp.concatenate([(acc_ref[g] / denom[g]).T for g in range(group)], axis=1).astype(o_ref.dtype)


def _dsa_attention(q, k, vt, mask_t, bias_tiles, rel_bias, n_heads, group=HEADS_PER_STEP):
    b, s, _ = q.shape
    t = bias_tiles.shape[-1]
    hd = HEAD_DIM
    group = min(group, n_heads)
    w = group * hd
    ng = n_heads // group
    return pl.pallas_call(
        functools.partial(_dsa_kernel, t=t, group=group),
        grid=(b, ng, s // t),
        in_specs=[pl.BlockSpec((1, t, w), lambda i, h, j: (i, j, h)),
                  pl.BlockSpec((1, s, w), lambda i, h, j: (i, 0, h)),
                  pl.BlockSpec((1, w, s), lambda i, h, j: (i, h, 0)),
                  pl.BlockSpec((group, 2, t, t), lambda i, h, j: (h, 0, 0, 0)),
                  pl.BlockSpec((1, s, t), lambda i, h, j: (i, 0, j)),
                  pl.BlockSpec(memory_space=pltpu.SMEM)],
        out_specs=pl.BlockSpec((1, t, w), lambda i, h, j: (i, j, h)),
        out_shape=jax.ShapeDtypeStruct((b, s, n_heads * hd), BF16),
        scratch_shapes=_sweep_scratch(group, hd, t),
        compiler_params=_cparams(("parallel", "parallel", "arbitrary")),
        name="dsa_attention",
    )(q, k, vt, bias_tiles, mask_t, rel_bias)


def _router_kernel(h_ref, w_ref, b_ref, idx_ref, wt_ref, cnt_ref, seen_ref):
    @pl.when(pl.program_id(0) == 0)
    def _():
        seen_ref[...] = jnp.zeros_like(seen_ref)

    logits = jnp.dot(_unpack_bf16_pairs(h_ref[...]), w_ref[...], preferred_element_type=F32) + b_ref[...]
    tm = logits.shape[0]
    lane = lax.broadcasted_iota(jnp.int32, logits.shape, 1).astype(F32)
    top_v, top_i = [], []
    for _ in range(TOP_K):
        mx = jnp.max(logits, axis=-1, keepdims=True)
        first = jnp.min(jnp.where(logits == mx, lane, float(LANES)), axis=-1, keepdims=True)
        top_v.append(mx)
        top_i.append(first)
        logits = jnp.where(lane == first, -jnp.inf, logits)
    e = [jnp.exp(v - top_v[0]) for v in top_v]
    denom = e[0] + e[1] + e[2] + e[3]
    chosen = jnp.zeros(logits.shape, F32)
    for k in range(TOP_K):
        chosen = chosen + jnp.where(lane == top_i[k], 1.0, 0.0)
    earlier = (lax.broadcasted_iota(jnp.int32, (tm, tm), 0) > lax.broadcasted_iota(jnp.int32, (tm, tm), 1))
    before = jnp.dot(jnp.where(earlier, 1.0, 0.0).astype(BF16), chosen.astype(BF16),
                     preferred_element_type=F32) + seen_ref[...]
    idx = jnp.zeros(logits.shape, F32)
    wts = jnp.zeros(logits.shape, F32)
    for k in range(TOP_K):
        rank = jnp.sum(jnp.where(lane == top_i[k], before, 0.0), axis=-1, keepdims=True)
        idx = jnp.where(lane == float(k), top_i[k], idx)
        idx = jnp.where(lane == float(TOP_K + k), rank, idx)
        wts = jnp.where(lane == float(k), e[k] / denom, wts)
    idx_ref[...] = idx.astype(jnp.int32)
    wt_ref[...] = wts
    seen_ref[...] += jnp.sum(chosen, axis=0, keepdims=True)
    cnt_ref[...] = seen_ref[...].astype(jnp.int32)


def _router(h2, w_router, b_router):
    n, dp = h2.shape
    d = w_router.shape[0]
    ne = w_router.shape[1]
    tm = _row_tile(n, 1024)
    w = _pad_cols(w_router).astype(BF16)
    bias = jnp.concatenate([b_router, jnp.full((LANES - ne,), NEG_INF, F32)]).reshape(1, LANES)
    out = pl.BlockSpec((tm, LANES), lambda i: (i, 0))
    return pl.pallas_call(
        _router_kernel,
        grid=(n // tm,),
        in_specs=[pl.BlockSpec((tm, dp), lambda i: (i, 0)),
                  pl.BlockSpec((d, LANES), lambda i: (0, 0)),
                  pl.BlockSpec((1, LANES), lambda i: (0, 0))],
        out_specs=[out, out, pl.BlockSpec((1, LANES), lambda i: (0, 0))],
        out_shape=[jax.ShapeDtypeStruct((n, LANES), jnp.int32), jax.ShapeDtypeStruct((n, LANES), F32),
                   jax.ShapeDtypeStruct((1, LANES), jnp.int32)],
        scratch_shapes=[pltpu.VMEM((1, LANES), F32)],
        compiler_params=_cparams(("arbitrary",)),
        name="moe_router_top4",
    )(h2, w, bias)


GU_GROUP = 2 * LANES


def _split_even_odd_kernel(w_ref, p_ref, o_ref):
    for g in range(w_ref.shape[2] // GU_GROUP):
        cols = slice(g * GU_GROUP, (g + 1) * GU_GROUP)
        o_ref[0, :, cols] = jnp.dot(w_ref[0, :, cols].astype(BF16), p_ref[...],
                                    preferred_element_type=F32).astype(o_ref.dtype)


def _split_even_odd_cols(w):
    e, d, n = w.shape
    assert n % GU_GROUP == 0
    half = GU_GROUP // 2
    perm = np.zeros((GU_GROUP, GU_GROUP), np.float32)
    perm[2 * np.arange(half), np.arange(half)] = 1.0
    perm[2 * np.arange(half) + 1, half + np.arange(half)] = 1.0
    td = _row_tile(d, 512)
    return pl.pallas_call(
        _split_even_odd_kernel,
        grid=(e, d // td),
        in_specs=[pl.BlockSpec((1, td, n), lambda i, j: (i, j, 0)),
                  pl.BlockSpec((GU_GROUP, GU_GROUP), lambda i, j: (0, 0))],
        out_specs=pl.BlockSpec((1, td, n), lambda i, j: (i, j, 0)),
        out_shape=jax.ShapeDtypeStruct((e, d, n), BF16),
        compiler_params=_cparams(("parallel", "parallel")),
        name="gate_up_split_columns",
    )(w, jnp.asarray(perm, BF16))


def _split_even_odd_bias(b):
    e, n = b.shape
    return jnp.transpose(b.reshape(e, n // GU_GROUP, GU_GROUP // 2, 2), (0, 1, 3, 2)).reshape(e, n)


def _expert_ffn(h, wgu, bgu, wd, bd):
    gu = jnp.dot(h, wgu, preferred_element_type=F32) + bgu
    half = GU_GROUP // 2
    acts = []
    for g in range(gu.shape[1] // GU_GROUP):
        gate = jnp.minimum(gu[:, g * GU_GROUP:g * GU_GROUP + half], SWIGLU_LIMIT)
        up = jnp.clip(gu[:, g * GU_GROUP + half:(g + 1) * GU_GROUP], -SWIGLU_LIMIT, SWIGLU_LIMIT)
        acts.append(((up + 1.0) * (gate * _sigmoid(gate * SWIGLU_ALPHA))).astype(BF16))
    act = jnp.concatenate(acts, axis=1)
    return jnp.dot(act, wd, preferred_element_type=F32) + bd


def _row_gather_start(src_hbm, rows_ref, dst, sem, n_rows):
    for r in range(n_rows):
        copy = pltpu.make_async_copy(src_hbm.at[pl.ds(rows_ref[0, 0, r], 1)], dst.at[pl.ds(r, 1)], sem)
        copy.start(priority=r % 2)


def _row_gather_wait(src_hbm, dst, sem):
    pltpu.make_async_copy(src_hbm.at[pl.ds(0, dst.shape[0])], dst, sem).wait()


def _moe_expert_kernel(te_ref, tok_ref, tok_next_ref, wrow_ref, wgu_ref, bgu_ref, wd_ref, bd_ref, h_hbm,
                       o_ref, xbuf, sem, *, tm):
    i = pl.program_id(0)
    slot = i % 2

    @pl.when(i == 0)
    def _():
        _row_gather_start(h_hbm, tok_ref, xbuf.at[0], sem.at[0], tm)

    _row_gather_start(h_hbm, tok_next_ref, xbuf.at[1 - slot], sem.at[1 - slot], tm)
    _row_gather_wait(h_hbm, xbuf.at[slot], sem.at[slot])
    y = _expert_ffn(_unpack_bf16_pairs(xbuf[slot]), wgu_ref[0], bgu_ref[0], wd_ref[0], bd_ref[0])
    o_ref[...] = y * jnp.tile(wrow_ref[...], (1, y.shape[1] // LANES))

    @pl.when(i == pl.num_programs(0) - 1)
    def _():
        _row_gather_wait(h_hbm, xbuf.at[1 - slot], sem.at[1 - slot])


def _moe_experts_sparse(h2, tile_expert, tok_tiles, w_rows, wgu, bgu, wd, bd, tm):
    n, dp = h2.shape
    ne, d, f2 = wgu.shape
    n_tiles = tile_expert.shape[0]
    exp3 = lambda i, te: (te[i], 0, 0)
    smem_rows = lambda off: pl.BlockSpec((1, 1, tm), lambda i, te: (i + off, 0, 0), memory_space=pltpu.SMEM)
    grid_spec = pltpu.PrefetchScalarGridSpec(
        num_scalar_prefetch=1,
        grid=(n_tiles,),
        in_specs=[smem_rows(0), smem_rows(1),
                  pl.BlockSpec((tm, LANES), lambda i, te: (i, 0)),
                  pl.BlockSpec((1, d, f2), exp3), pl.BlockSpec((1, 1, f2), exp3),
                  pl.BlockSpec((1, f2 // 2, d), exp3), pl.BlockSpec((1, 1, d), exp3),
                  pl.BlockSpec(memory_space=pl.ANY)],
        out_specs=pl.BlockSpec((tm, d), lambda i, te: (i, 0)),
        scratch_shapes=[pltpu.VMEM((2, tm, dp), jnp.int32), pltpu.SemaphoreType.DMA((2,))],
    )
    return pl.pallas_call(
        functools.partial(_moe_expert_kernel, tm=tm),
        grid_spec=grid_spec,
        out_shape=jax.ShapeDtypeStruct((n_tiles * tm, d), F32),
        compiler_params=_cparams(("arbitrary",)),
        name="moe_experts_sparse",
    )(tile_expert, tok_tiles, tok_tiles, w_rows, wgu, bgu.reshape(ne, 1, f2), wd, bd.reshape(ne, 1, d), h2)


def _moe_combine_kernel(pos_ref, pos_next_ref, x_ref, g2_ref, y_hbm, o_ref, ybuf, sem, *, tm):
    i = pl.program_id(0)
    slot = i % 2
    n_rows = TOP_K * tm

    @pl.when(i == 0)
    def _():
        _row_gather_start(y_hbm, pos_ref, ybuf.at[0], sem.at[0], n_rows)

    _row_gather_start(y_hbm, pos_next_ref, ybuf.at[1 - slot], sem.at[1 - slot], n_rows)
    _row_gather_wait(y_hbm, ybuf.at[slot], sem.at[slot])
    acc = ybuf[slot, 0:tm, :]
    for k in range(1, TOP_K):
        acc = acc + ybuf[slot, k * tm:(k + 1) * tm, :]
    o_ref[...] = x_ref[...] + g2_ref[0] * acc

    @pl.when(i == pl.num_programs(0) - 1)
    def _():
        _row_gather_wait(y_hbm, ybuf.at[1 - slot], sem.at[1 - slot])


def _moe_combine(y_rows, pos_tiles, x2, g2, rows_per_batch, tm):
    n, d = x2.shape
    nb = g2.shape[0]
    tiles_per_batch = rows_per_batch // tm
    smem_rows = lambda off: pl.BlockSpec((1, 1, TOP_K * tm), lambda i: (i + off, 0, 0), memory_space=pltpu.SMEM)
    return pl.pallas_call(
        functools.partial(_moe_combine_kernel, tm=tm),
        grid=(n // tm,),
        in_specs=[smem_rows(0), smem_rows(1),
                  pl.BlockSpec((tm, d), lambda i: (i, 0)),
                  pl.BlockSpec((1, 1, d), lambda i: (i // tiles_per_batch, 0, 0)),
                  pl.BlockSpec(memory_space=pl.ANY)],
        out_specs=pl.BlockSpec((tm, d), lambda i: (i, 0)),
        out_shape=jax.ShapeDtypeStruct((n, d), F32),
        scratch_shapes=[pltpu.VMEM((2, TOP_K * tm, d), F32), pltpu.SemaphoreType.DMA((2,))],
        compiler_params=_cparams(("arbitrary",)),
        name="moe_combine",
    )(pos_tiles, pos_tiles, x2, g2.reshape(nb, 1, d), y_rows)


def _moe_dispatch_plan(idx4, rank4, w4, counts, tm):
    n = idx4.shape[0]
    ne = counts.shape[0]
    n_pairs = n * TOP_K
    big = 1 << (n_pairs + tm - 1).bit_length()
    unused = jnp.int32(ne * big)
    eid = idx4.reshape(-1)
    experts = jnp.arange(ne, dtype=jnp.int32)
    n_pad = (-counts) % tm
    row_off = jnp.cumsum(counts + n_pad) - (counts + n_pad)
    pos = rank4 + jnp.sum(jnp.where(idx4[:, :, None] == experts, row_off, 0), axis=-1)
    j = jnp.arange(tm, dtype=jnp.int32)
    pad_keys = jnp.where(j[None, :] < n_pad[:, None], experts[:, None] * big + n_pairs + j[None, :], unused)
    keys = jnp.concatenate([eid * big + jnp.arange(n_pairs, dtype=jnp.int32), pad_keys.reshape(-1)])
    wvals = jnp.concatenate([w4.reshape(-1), jnp.zeros((ne * tm,), F32)])
    skeys, w_rows = lax.sort((keys, wvals), num_keys=1)
    slot = skeys % big
    is_pair = (skeys < unused) & (slot < n_pairs)
    tok_rows = jnp.where(is_pair, slot // TOP_K, 0)
    tile_e = jnp.minimum(skeys[::tm] // big, ne - 1)
    return tile_e, tok_rows, w_rows, pos


def _dsa_layer(h, x, g1, w_in, kv_gain, w_uk, w_uv, w_out, rel_bias, bias_tiles):
    b, s, d = h.shape
    n_heads = d // HEAD_DIM
    rank = kv_gain.shape[0]
    idx_heads = n_heads // 2
    n_sel = min(TOPK_MAX, s // 4)
    o1 = n_heads * HEAD_DIM
    h2 = h.reshape(b * s, d)
    q = _matmul(h2, w_in[:, :o1].astype(BF16), BF16, col_scale=_query_col_scale(o1, o1)).reshape(b, s, o1)
    w_rest = _pad_cols(w_in[:, o1:]).astype(BF16)
    rest = _matmul(h2, w_rest, F32, tn=w_rest.shape[1]).reshape(b, s, -1)
    assert rank % LANES == 0 and (idx_heads * IDX_DIM) == rank
    ckv = _norm(rest, kv_gain, BF16)
    w_k = jnp.transpose(w_uk, (2, 0, 1)).reshape(rank, o1)
    w_v = jnp.transpose(w_uv, (1, 0, 2)).reshape(rank, o1)
    ckv2 = ckv.reshape(b * s, rank)
    k = _matmul(ckv2, w_k.astype(BF16), BF16).reshape(b, s, o1)
    vt = _matmul_transposed(ckv2, w_v.T.astype(BF16), BF16, s)
    t = bias_tiles.shape[-1]
    w_off = 2 * rank + IDX_DIM
    w_t = jnp.transpose(rest[:, :, w_off:w_off + idx_heads], (0, 2, 1))
    mask_t = _indexer_mask(rest, w_t, t, idx_heads, n_sel, qidx_block=1, kx_block=2 * rank // LANES)
    o = _dsa_attention(q, k, vt, mask_t, bias_tiles, rel_bias, n_heads)
    return _matmul_residual(o.reshape(b * s, o1), w_out.astype(BF16), x.reshape(b * s, d), g1, s)


def _fox_layer(h, x, g1, w_in, f_bias, w_out):
    b, s, d = h.shape
    n_heads = f_bias.shape[0]
    hd3 = 3 * n_heads * HEAD_DIM
    h2 = h.reshape(b * s, d)
    hd2 = 2 * hd3 // 3
    qk = _matmul(h2, w_in[:, :hd2].astype(BF16), BF16, col_scale=_query_col_scale(hd2 // 2, hd2)).reshape(b, s, hd2)
    vt = _matmul_transposed(h2, w_in[:, hd2:hd3].T.astype(BF16), BF16, s)
    w_f = _pad_cols(w_in[:, hd3:]).astype(BF16)
    fg = _matmul(h2, w_f, F32, tn=w_f.shape[1]).reshape(b, s, -1)
    cum = _logsig_cumsum(fg, f_bias)
    o = _fox_attention(qk, vt, cum, n_heads)
    return _matmul_residual(o.reshape(b * s, -1), w_out.astype(BF16), x.reshape(b * s, d), g1, s)


def _diff_layer(h, x, g1, w_in, lam_p, sub_gain, w_out, rel_bias, bias_tiles, lam_init):
    b, s, d = h.shape
    n_heads = w_in.shape[1] // (6 * HEAD_DIM)
    h2 = h.reshape(b * s, d)
    n_qk = 2 * w_in.shape[1] // 3
    qk = _matmul(h2, w_in[:, :n_qk].astype(BF16), BF16, col_scale=_query_col_scale(n_qk // 2, n_qk)).reshape(b, s, n_qk)
    vt = _matmul_transposed(h2, w_in[:, n_qk:].T.astype(BF16), BF16, s)
    o = _diff_attention(qk, vt, bias_tiles, rel_bias, lam_p, sub_gain, n_heads, lam_init)
    return _matmul_residual(o.reshape(b * s, -1), w_out.astype(BF16), x.reshape(b * s, d), g1, s)


def _moe_layer(h, x2, g2, w_router, b_router, wgu, b_gu, w_dn, b_dn):
    b, s, dp = h.shape
    n = b * s
    ne = wgu.shape[0]
    h2 = h.reshape(n, dp)
    idx, wts, counts = _router(h2, w_router, b_router)
    tm = min(MOE_ROW_TILE, n * TOP_K)
    tmc = _row_tile(s, MOE_TOKEN_TILE)
    tile_e, tok_rows, w_rows, pos = _moe_dispatch_plan(idx[:, :TOP_K], idx[:, TOP_K:2 * TOP_K], wts[:, :TOP_K],
                                                       counts[0, :ne], tm)
    tok_tiles = jnp.pad(tok_rows.reshape(-1, 1, tm), ((0, 1), (0, 0), (0, 0)))
    w_rows = jnp.broadcast_to(w_rows[:, None], (w_rows.shape[0], LANES))
    bgu = _split_even_odd_bias(b_gu)
    y_rows = _moe_experts_sparse(h2, tile_e, tok_tiles, w_rows, wgu, bgu, w_dn.astype(BF16), b_dn, tm)
    pos_tiles = jnp.transpose(pos.reshape(n // tmc, tmc, TOP_K), (0, 2, 1)).reshape(n // tmc, TOP_K * tmc)
    pos_tiles = jnp.pad(pos_tiles, ((0, 1), (0, 0)))[:, None, :]
    return _moe_combine(y_rows, pos_tiles, x2, g2, s, tmc)


def kernel(x, c, rel_bias, norm_mix, norm_ffn, w_mod, b_mod, dsa_w_in, dsa_kv_gain, dsa_w_uk, dsa_w_uv, dsa_w_out, fox_w_in, fox_forget_bias, fox_w_out, diff_w_in, diff_lambda, diff_subln_gain, diff_w_out, w_router, b_router, w_gate_up, b_gate_up, w_down, b_down, final_norm):
    b, s, d = x.shape
    depth = w_mod.shape[0]
    mod = _modulation(c, w_mod, b_mod)
    t_attn = _row_tile(s, 256)
    bias_tiles = _bias_tiles(rel_bias, t_attn)
    ne, ff2 = w_gate_up.shape[1], w_gate_up.shape[3]
    wgu_all = _split_even_odd_cols(w_gate_up.reshape(depth * ne, d, ff2)).reshape(depth, ne, d, ff2)
    ia = ib = ic = 0
    for i in range(depth):
        sh1, sc1, g1, sh2, sc2, g2 = (mod[i, :, k * d:(k + 1) * d] for k in range(6))
        h = _norm_mod(x, norm_mix[i], sc1, sh1)
        kind = i % N_MIXERS
        if kind == 0:
            x2 = _dsa_layer(h, x, g1, dsa_w_in[ia], dsa_kv_gain[ia], dsa_w_uk[ia], dsa_w_uv[ia],
                            dsa_w_out[ia], rel_bias, bias_tiles)
            ia += 1
        elif kind == 1:
            x2 = _fox_layer(h, x, g1, fox_w_in[ib], fox_forget_bias[ib], fox_w_out[ib])
            ib += 1
        else:
            lam_init = 0.8 - 0.6 * math.exp(-0.3 * i)
            x2 = _diff_layer(h, x, g1, diff_w_in[ic], diff_lambda[ic], diff_subln_gain[ic],
                             diff_w_out[ic], rel_bias, bias_tiles, lam_init)
            ic += 1
        x = x2.reshape(b, s, d)
        h = _norm_mod(x, norm_ffn[i], sc2, sh2, packed=True)
        x = _moe_layer(h, x2, g2, w_router[i], b_router[i], wgu_all[i], b_gate_up[i],
                       w_down[i], b_down[i]).reshape(b, s, d)
    return _norm(x, final_norm, x.dtype)
```
